```python
import math
import jax, jax.numpy as jnp
from jax import lax
import numpy as np

D_MODEL = 1024
BATCH = 16
SEQ = 2048
DEPTH = 4

HEAD_DIM = 64
N_HEADS_MIX = D_MODEL // HEAD_DIM
N_FOX = N_HEADS_MIX // 2
N_MOBA = N_HEADS_MIX - N_FOX
FOX_W = N_FOX * HEAD_DIM
MOBA_W = N_MOBA * HEAD_DIM
IN_COLS = 3 * FOX_W + N_FOX + 3 * MOBA_W
FOX_Q_BLOCK = 128
MOBA_BLOCK = 256
MOBA_TOPK = 3
MOBA_Q_CHUNK = 16
N_BUCKETS = 32
MAX_DISTANCE = 128
N_MEM = 256
N_CROSS_HEADS = 4
CROSS_HEAD_DIM = D_MODEL // N_CROSS_HEADS
N_EXPERTS = 32
TOP_K = 4
D_EXPERT = D_MODEL
SWIGLU_LIMIT = 7.0
SWIGLU_ALPHA = 1.702
EXPERT_BLOCK = 128
DEEPNORM_ALPHA = (2 * DEPTH) ** 0.25
DEEPNORM_BETA = (8 * DEPTH) ** -0.25
LN_EPS = 1e-5

kernel_name = "fox_moba_hymba_deepnorm_moe"


def layer_norm(x, g, b):
    x32 = x.astype(jnp.float32)
    mu = jnp.mean(x32, axis=-1, keepdims=True)
    var = jnp.mean(jnp.square(x32 - mu), axis=-1, keepdims=True)
    y = (x32 - mu) * lax.rsqrt(var + LN_EPS)
    return (y * g.astype(jnp.float32) + b.astype(jnp.float32)).astype(x.dtype)


def split_heads(t, n_heads, head_dim):
    B, S, _ = t.shape
    return t.reshape(B, S, n_heads, head_dim).transpose(0, 2, 1, 3)


def merge_heads(t):
    B, H, S, d = t.shape
    return t.transpose(0, 2, 1, 3).reshape(B, S, H * d)


def t5_bucket(dist):
    max_exact = N_BUCKETS // 2
    n = jnp.maximum(dist, 0)
    nf = jnp.maximum(n, 1).astype(jnp.float32)
    large = max_exact + (jnp.log(nf / max_exact) / math.log(MAX_DISTANCE / max_exact)
                         * (N_BUCKETS - max_exact)).astype(jnp.int32)
    large = jnp.minimum(large, N_BUCKETS - 1)
    return jnp.where(n < max_exact, n, large)


def fox_attention(q, k, v, log_f):
    B, H, S, d = q.shape
    c = jnp.cumsum(log_f, axis=-1)
    scale = d ** -0.5
    kpos = jnp.arange(S)

    def block(i):
        start = i * FOX_Q_BLOCK
        qb = lax.dynamic_slice_in_dim(q, start, FOX_Q_BLOCK, axis=2)
        cb = lax.dynamic_slice_in_dim(c, start, FOX_Q_BLOCK, axis=2)
        qpos = start + jnp.arange(FOX_Q_BLOCK)
        logits = jnp.einsum('bhqd,bhkd->bhqk', qb, k,
                            preferred_element_type=jnp.float32) * scale
        logits = logits + cb[..., :, None] - c[..., None, :]
        logits = jnp.where(kpos[None, :] <= qpos[:, None], logits, -jnp.inf)
        p = jax.nn.softmax(logits, axis=-1)
        return jnp.einsum('bhqk,bhkd->bhqd', p.astype(v.dtype), v)

    out = lax.map(block, jnp.arange(S // FOX_Q_BLOCK))
    return jnp.moveaxis(out, 0, 2).reshape(B, H, S, d)


def moba_attention(q, k, v, rel_bias):
    B, H, S, d = q.shape
    nblk = -(-S // MOBA_BLOCK)
    pad = nblk * MOBA_BLOCK - S
    kp = jnp.pad(k, ((0, 0), (0, 0), (0, pad), (0, 0)))
    vp = jnp.pad(v, ((0, 0), (0, 0), (0, pad), (0, 0)))
    kb = kp.reshape(B, H, nblk, MOBA_BLOCK, d)
    vb = vp.reshape(B, H, nblk, MOBA_BLOCK, d)
    kmean = jnp.mean(kb.astype(jnp.float32), axis=3)
    n_sel = min(MOBA_TOPK, nblk)
    scale = d ** -0.5
    bias_tab = rel_bias.T.astype(jnp.float32)
    bidx = jnp.arange(B)[:, None, None, None]
    hidx = jnp.arange(H)[None, :, None, None]
    hidx5 = jnp.arange(H)[None, :, None, None, None]
    offs = jnp.arange(MOBA_BLOCK)

    def chunk(ci):
        start = ci * MOBA_Q_CHUNK
        qc = lax.dynamic_slice_in_dim(q, start, MOBA_Q_CHUNK, axis=2)
        qpos = start + jnp.arange(MOBA_Q_CHUNK)
        own = start // MOBA_BLOCK
        gate = jnp.einsum('bhcd,bhnd->bhcn', qc.astype(jnp.float32), kmean)
        gate = jnp.where(jnp.arange(nblk) < own, gate, -jnp.inf)
        gval, sel = lax.top_k(gate, n_sel)
        sel_ok = gval > -jnp.inf
        ks = kb[bidx, hidx, sel]
        vs = vb[bidx, hidx, sel]
        lg_sel = jnp.einsum('bhcd,bhcnkd->bhcnk', qc, ks,
                            preferred_element_type=jnp.float32) * scale
        dist_sel = qpos[None, None, :, None, None] - (sel[..., None] * MOBA_BLOCK + offs)
        lg_sel = lg_sel + bias_tab[hidx5, t5_bucket(dist_sel)]
        lg_sel = jnp.where(sel_ok[..., None], lg_sel, -jnp.inf)
        k_own = lax.dynamic_slice_in_dim(kp, own * MOBA_BLOCK, MOBA_BLOCK, axis=2)
        v_own = lax.dynamic_slice_in_dim(vp, own * MOBA_BLOCK, MOBA_BLOCK, axis=2)
        lg_own = jnp.einsum('bhcd,bhkd->bhck', qc, k_own,
                            preferred_element_type=jnp.float32) * scale
        dist_own = qpos[:, None] - (own * MOBA_BLOCK + offs)[None, :]
        lg_own = lg_own + bias_tab[:, t5_bucket(dist_own)][None]
        lg_own = jnp.where(dist_own >= 0, lg_own, -jnp.inf)
        logits = jnp.concatenate(
            [lg_sel.reshape(B, H, MOBA_Q_CHUNK, n_sel * MOBA_BLOCK), lg_own], axis=-1)
        p = jax.nn.softmax(logits, axis=-1).astype(v.dtype)
        p_sel = p[..., :n_sel * MOBA_BLOCK].reshape(B, H, MOBA_Q_CHUNK, n_sel, MOBA_BLOCK)
        p_own = p[..., n_sel * MOBA_BLOCK:]
        return (jnp.einsum('bhcnk,bhcnkd->bhcd', p_sel, vs)
                + jnp.einsum('bhck,bhkd->bhcd', p_own, v_own))

    out = lax.map(chunk, jnp.arange(S // MOBA_Q_CHUNK))
    return jnp.moveaxis(out, 0, 2).reshape(B, H, S, d)


def hybrid_mixer(x, w_in, b_forget, w_out, rel_bias):
    proj = jnp.einsum('bsd,de->bse', x, w_in)
    cuts = [FOX_W, 2 * FOX_W, 3 * FOX_W, 3 * FOX_W + N_FOX,
            3 * FOX_W + N_FOX + MOBA_W, 3 * FOX_W + N_FOX + 2 * MOBA_W]
    q_f, k_f, v_f, f_logit, q_m, k_m, v_m = jnp.split(proj, cuts, axis=-1)
    log_f = jax.nn.log_sigmoid((f_logit + b_forget).astype(jnp.float32))
    log_f = log_f.transpose(0, 2, 1)
    o_f = fox_attention(split_heads(q_f, N_FOX, HEAD_DIM), split_heads(k_f, N_FOX, HEAD_DIM),
                        split_heads(v_f, N_FOX, HEAD_DIM), log_f)
    o_m = moba_attention(split_heads(q_m, N_MOBA, HEAD_DIM), split_heads(k_m, N_MOBA, HEAD_DIM),
                         split_heads(v_m, N_MOBA, HEAD_DIM), rel_bias)
    o = merge_heads(jnp.concatenate([o_f, o_m.astype(o_f.dtype)], axis=1))
    return jnp.einsum('bse,ed->bsd', o, w_out)


def memory_cross_attention(x, mem, w_q, w_k, w_v, w_o):
    q = split_heads(jnp.einsum('bsd,de->bse', x, w_q), N_CROSS_HEADS, CROSS_HEAD_DIM)
    k = split_heads(jnp.einsum('bmd,de->bme', mem, w_k), N_CROSS_HEADS, CROSS_HEAD_DIM)
    v = split_heads(jnp.einsum('bmd,de->bme', mem, w_v), N_CROSS_HEADS, CROSS_HEAD_DIM)
    logits = jnp.einsum('bhsd,bhmd->bhsm', q, k,
                        preferred_element_type=jnp.float32) * CROSS_HEAD_DIM ** -0.5
    p = jax.nn.softmax(logits, axis=-1).astype(v.dtype)
    o = merge_heads(jnp.einsum('bhsm,bhmd->bhsd', p, v))
    return jnp.einsum('bse,ed->bsd', o, w_o)


def moe_ffn(x, w_router, b_router, w_gate_up, b_gate_up, w_down, b_down):
    B, S, D = x.shape
    T = B * S
    TK = T * TOP_K
    xt = x.reshape(T, D)
    logits = (xt @ w_router + b_router).astype(jnp.float32)
    top_val, top_idx = lax.top_k(logits, TOP_K)
    gates = jax.nn.softmax(top_val, axis=-1)
    e_flat = top_idx.reshape(-1)
    order = jnp.argsort(e_flat)
    e_sorted = e_flat[order]
    counts = jnp.zeros((N_EXPERTS,), jnp.int32).at[e_flat].add(1)
    padded = ((counts + EXPERT_BLOCK - 1) // EXPERT_BLOCK) * EXPERT_BLOCK
    start = jnp.cumsum(counts) - counts
    pend = jnp.cumsum(padded)
    pstart = pend - padded
    rank = jnp.arange(TK, dtype=jnp.int32) - start[e_sorted]
    dest = jnp.zeros((TK,), jnp.int32).at[order].set(pstart[e_sorted] + rank)
    n_blocks = -(-TK // EXPERT_BLOCK) + N_EXPERTS
    n_pad = n_blocks * EXPERT_BLOCK
    slot_token = jnp.full((n_pad,), T, jnp.int32).at[dest].set(
        jnp.arange(TK, dtype=jnp.int32) // TOP_K)
    x_pad = jnp.concatenate([xt, jnp.zeros((1, D), xt.dtype)], axis=0)
    xs = x_pad[slot_token].reshape(n_blocks, EXPERT_BLOCK, D)
    block_expert = jnp.minimum(
        jnp.searchsorted(pend, jnp.arange(n_blocks, dtype=jnp.int32) * EXPERT_BLOCK, side='right'),
        N_EXPERTS - 1)

    def expert_block(args):
        xb, e = args
        h = xb @ w_gate_up[e] + b_gate_up[e]
        g, u = h[:, :D_EXPERT], h[:, D_EXPERT:]
        g = jnp.minimum(g, SWIGLU_LIMIT)
        u = jnp.clip(u, -SWIGLU_LIMIT, SWIGLU_LIMIT)
        glu = g * jax.nn.sigmoid(g * SWIGLU_ALPHA)
        return ((u + 1.0) * glu) @ w_down[e] + b_down[e]

    ys = lax.map(expert_block, (xs, block_expert)).reshape(n_pad, D)
    y = jnp.einsum('tkd,tk->td', ys[dest].reshape(T, TOP_K, D), gates.astype(ys.dtype))
    return y.reshape(B, S, D)


def setup_inputs(seed: int = 0) -> dict:
    key = jax.random.key(seed)
    ks = jax.random.split(key, 24)
    D, L, F, E = D_MODEL, DEPTH, D_EXPERT, N_EXPERTS
    beta = DEEPNORM_BETA

    def nrm(k, shape, scale):
        return jax.random.normal(k, shape, jnp.float32) * scale

    col_scale = jnp.concatenate([
        jnp.ones((2 * FOX_W,), jnp.float32), jnp.full((FOX_W,), beta, jnp.float32),
        jnp.ones((N_FOX + 2 * MOBA_W,), jnp.float32), jnp.full((MOBA_W,), beta, jnp.float32)])
    return {
        "x": nrm(ks[0], (BATCH, SEQ, D), 1.0),
        "mem": nrm(ks[1], (BATCH, N_MEM, D), 1.0),
        "w_in": nrm(ks[2], (L, D, IN_COLS), D ** -0.5) * col_scale,
        "b_forget": 3.0 + nrm(ks[3], (L, N_FOX), 1.0),
        "w_mix_out": nrm(ks[4], (L, D, D), beta * D ** -0.5),
        "rel_bias": nrm(ks[5], (N_BUCKETS, N_MOBA), 0.5),
        "ln1_g": 1.0 + nrm(ks[6], (L, D), 0.02),
        "ln1_b": nrm(ks[7], (L, D), 0.02),
        "w_cq": nrm(ks[8], (L, D, D), D ** -0.5),
        "w_ck": nrm(ks[9], (L, D, D), D ** -0.5),
        "w_cv": nrm(ks[10], (L, D, D), beta * D ** -0.5),
        "w_co": nrm(ks[11], (L, D, D), beta * D ** -0.5),
        "ln2_g": 1.0 + nrm(ks[12], (L, D), 0.02),
        "ln2_b": nrm(ks[13], (L, D), 0.02),
        "w_router": nrm(ks[14], (L, D, E), D ** -0.5),
        "b_router": nrm(ks[15], (L, E), 0.01),
        "w_gate_up": nrm(ks[16], (L, E, D, 2 * F), beta * D ** -0.5),
        "b_gate_up": nrm(ks[17], (L, E, 2 * F), 0.01),
        "w_down": nrm(ks[18], (L, E, F, D), beta * F ** -0.5),
        "b_down": nrm(ks[19], (L, E, D), 0.01),
        "ln3_g": 1.0 + nrm(ks[20], (L, D), 0.02),
        "ln3_b": nrm(ks[21], (L, D), 0.02),
    }


def reference(x, mem, w_in, b_forget, w_mix_out, rel_bias, ln1_g, ln1_b,
              w_cq, w_ck, w_cv, w_co, ln2_g, ln2_b,
              w_router, b_router, w_gate_up, b_gate_up, w_down, b_down,
              ln3_g, ln3_b):
    for l in range(DEPTH):
        h = hybrid_mixer(x, w_in[l], b_forget[l], w_mix_out[l], rel_bias)
        x = layer_norm(DEEPNORM_ALPHA * x + h, ln1_g[l], ln1_b[l])
        h = memory_cross_attention(x, mem, w_cq[l], w_ck[l], w_cv[l], w_co[l])
        x = layer_norm(DEEPNORM_ALPHA * x + h, ln2_g[l], ln2_b[l])
        h = moe_ffn(x, w_router[l], b_router[l], w_gate_up[l], b_gate_up[l],
                    w_down[l], b_down[l])
        x = layer_norm(DEEPNORM_ALPHA * x + h, ln3_g[l], ln3_b[l])
    return x
```

```python
import functools
import math

import jax
import jax.numpy as jnp
import numpy as np
from jax import lax
from jax.experimental import pallas as pl
from jax.experimental.pallas import tpu as pltpu

F32 = jnp.float32
BF16 = jnp.bfloat16

HEAD_DIM = 64
N_BUCKETS = 32
MAX_DISTANCE = 128
MOBA_BLOCK = 256
MOBA_TOPK = 3
N_CROSS_HEADS = 4
TOP_K = 4
SWIGLU_LIMIT = 7.0
SWIGLU_ALPHA = 1.702
LN_EPS = 1e-5

LANES = 128
VMEM_LIMIT = 56 * 1024 * 1024

ROW_TILE = 512
ATT_TILE = MOBA_BLOCK
MOE_TILE = 512
MOE_FCHUNK = 512
MASKED = -1e30


def _params(*sem):
    return pltpu.CompilerParams(dimension_semantics=sem, vmem_limit_bytes=VMEM_LIMIT)


def _nt_dot(a, b):
    return lax.dot_general(a, b, (((1,), (1,)), ((), ())), preferred_element_type=F32)


def _layer_norm(y, g, b):
    mu = jnp.mean(y, axis=-1, keepdims=True)
    d = y - mu
    var = jnp.mean(d * d, axis=-1, keepdims=True)
    return d * lax.rsqrt(var + LN_EPS) * g + b


def _in_proj_kernel(x_ref, w_ref, qkv_ref, f_ref):
    xb = x_ref[...].astype(BF16)
    n_qkv = qkv_ref.shape[1]
    for n0 in range(0, n_qkv, 512):
        qkv_ref[:, n0:n0 + 512] = jnp.dot(
            xb, w_ref[:, n0:n0 + 512], preferred_element_type=F32).astype(BF16)
    f_ref[...] = jnp.dot(xb, w_ref[:, n_qkv:], preferred_element_type=F32)


def _in_proj(x, w):
    T, D = x.shape
    n_all = w.shape[1]
    n_qkv = n_all - LANES
    return pl.pallas_call(
        _in_proj_kernel,
        grid=(T // ROW_TILE,),
        in_specs=[pl.BlockSpec((ROW_TILE, D), lambda i: (i, 0)),
                  pl.BlockSpec((D, n_all), lambda i: (0, 0))],
        out_specs=[pl.BlockSpec((ROW_TILE, n_qkv), lambda i: (i, 0)),
                   pl.BlockSpec((ROW_TILE, LANES), lambda i: (i, 0))],
        out_shape=[jax.ShapeDtypeStruct((T, n_qkv), BF16),
                   jax.ShapeDtypeStruct((T, LANES), F32)],
        compiler_params=_params("parallel"),
        name="in_proj",
    )(x, w)


def _decay_kernel(f_ref, b_ref, c_ref):
    z = f_ref[0] + b_ref[...]
    ls = jnp.minimum(z, 0.0) - jnp.log1p(jnp.exp(-jnp.abs(z)))
    S = ls.shape[1]
    lane = lax.broadcasted_iota(jnp.int32, ls.shape, 1)
    shift = 1
    while shift < S:
        ls = ls + jnp.where(lane >= shift, pltpu.roll(ls, shift, axis=1), 0.0)
        shift *= 2
    c_ref[0] = ls


def _fox_decay(f_t, b_forget):
    B, H, S = f_t.shape
    return pl.pallas_call(
        _decay_kernel,
        grid=(B,),
        in_specs=[pl.BlockSpec((1, H, S), lambda b: (b, 0, 0)),
                  pl.BlockSpec((H, 1), lambda b: (0, 0))],
        out_specs=pl.BlockSpec((1, H, S), lambda b: (b, 0, 0)),
        out_shape=jax.ShapeDtypeStruct((B, H, S), F32),
        compiler_params=_params("parallel"),
        name="fox_decay",
    )(f_t, b_forget.reshape(H, 1))


def _online_update(s, v, m, l, acc):
    m_new = jnp.maximum(m, jnp.max(s, axis=1, keepdims=True))
    alpha = jnp.exp(m - m_new)
    p = jnp.exp(s - m_new)
    l_new = alpha * l + jnp.sum(p, axis=1, keepdims=True)
    acc_new = alpha * acc + jnp.dot(p.astype(BF16), v, preferred_element_type=F32)
    return m_new, l_new, acc_new


def _head_masks(tq):
    lane = lax.broadcasted_iota(jnp.int32, (tq, LANES), 1)
    return lane < HEAD_DIM


def _causal_mask(tq):
    row = lax.broadcasted_iota(jnp.int32, (tq, tq), 0)
    col = lax.broadcasted_iota(jnp.int32, (tq, tq), 1)
    return col <= row


def _fox_kernel(q_ref, k_ref, v_ref, c_ref, o_ref):
    tq = ATT_TILE
    S = q_ref.shape[0]
    pair = pl.program_id(1)
    first_head = _head_masks(tq)
    causal = _causal_mask(tq)
    zero = jnp.zeros((), BF16)

    def q_body(qi, carry):
        q0 = pl.multiple_of(qi * tq, tq)
        q = q_ref[pl.ds(q0, tq), :]
        qh = (jnp.where(first_head, q, zero), jnp.where(first_head, zero, q))
        k_d = k_ref[pl.ds(q0, tq), :]
        v_d = v_ref[pl.ds(q0, tq), :]
        state = []
        for hh in range(2):
            c_d = c_ref[0, pl.ds(2 * pair + hh, 1), pl.ds(q0, tq)]
            s = jnp.where(causal, _nt_dot(qh[hh], k_d) - c_d, MASKED)
            m0 = jnp.full((tq, 1), MASKED, F32)
            state.append(_online_update(s, v_d, m0, jnp.zeros((tq, 1), F32),
                                        jnp.zeros((tq, LANES), F32)))

        def k_body(kj, st):
            k0 = pl.multiple_of(kj * tq, tq)
            k = k_ref[pl.ds(k0, tq), :]
            v = v_ref[pl.ds(k0, tq), :]
            out = []
            for hh in range(2):
                c_k = c_ref[0, pl.ds(2 * pair + hh, 1), pl.ds(k0, tq)]
                s = _nt_dot(qh[hh], k) - c_k
                out.append(_online_update(s, v, *st[hh]))
            return tuple(out)

        state = lax.fori_loop(0, qi, k_body, tuple(state))
        o0 = state[0][2] * (1.0 / state[0][1])
        o1 = state[1][2] * (1.0 / state[1][1])
        o_ref[pl.ds(q0, tq), :] = jnp.where(first_head, o0, o1).astype(o_ref.dtype)
        return carry

    lax.fori_loop(0, S // tq, q_body, 0)


def _fox_attention(qkv, c, B, S, n_fox):
    T = B * S
    pairs = n_fox * HEAD_DIM // LANES
    kb = pairs
    vb = 2 * pairs
    return pl.pallas_call(
        _fox_kernel,
        grid=(B, pairs),
        in_specs=[pl.BlockSpec((S, LANES), lambda b, p: (b, p)),
                  pl.BlockSpec((S, LANES), lambda b, p: (b, kb + p)),
                  pl.BlockSpec((S, LANES), lambda b, p: (b, vb + p)),
                  pl.BlockSpec((1, n_fox, S), lambda b, p: (b, 0, 0))],
        out_specs=pl.BlockSpec((S, LANES), lambda b, p: (b, p)),
        out_shape=jax.ShapeDtypeStruct((T, pairs * LANES), BF16),
        compiler_params=_params("parallel", "parallel"),
        name="fox_attention",
    )(qkv, qkv, qkv, c)


def _moba_kernel(q_ref, k_ref, v_ref, a_ref, d_ref, far_ref, o_ref):
    tq = ATT_TILE
    S = q_ref.shape[0]
    nblk = S // tq
    first_head = _head_masks(tq)
    causal = _causal_mask(tq)
    lane = lax.broadcasted_iota(jnp.int32, (tq, LANES), 1)
    zero = jnp.zeros((), BF16)

    kmean = jnp.dot(a_ref[...], k_ref[...], preferred_element_type=F32)
    kmean_hi = kmean.astype(BF16)
    kmean_lo = (kmean - kmean_hi.astype(F32)).astype(BF16)

    def q_body(qi, carry):
        q0 = pl.multiple_of(qi * tq, tq)
        q = q_ref[pl.ds(q0, tq), :]
        qh = (jnp.where(first_head, q, zero), jnp.where(first_head, zero, q))
        k_d = k_ref[pl.ds(q0, tq), :]
        v_d = v_ref[pl.ds(q0, tq), :]
        state, chosen = [], []
        for hh in range(2):
            gate = _nt_dot(qh[hh], kmean_hi) + _nt_dot(qh[hh], kmean_lo)
            rank = jnp.zeros((tq, LANES), jnp.int32)
            for mb in range(nblk):
                g_m = gate[:, mb:mb + 1]
                tie = jnp.where(lane > mb, 1, 0)
                beats = jnp.where(g_m > gate, 1, jnp.where(g_m == gate, tie, 0))
                rank = rank + beats * jnp.where(mb < qi, 1, 0)
            keep = jnp.where(rank < MOBA_TOPK, jnp.where(lane < qi, 1.0, 0.0), 0.0)
            chosen.append(keep)
            s = jnp.where(causal, _nt_dot(qh[hh], k_d) + d_ref[hh, 0], MASKED)
            m0 = jnp.full((tq, 1), MASKED, F32)
            state.append(_online_update(s, v_d, m0, jnp.zeros((tq, 1), F32),
                                        jnp.zeros((tq, LANES), F32)))

        def k_body(kj, st):
            k0 = pl.multiple_of(kj * tq, tq)
            k = k_ref[pl.ds(k0, tq), :]
            v = v_ref[pl.ds(k0, tq), :]
            out = []
            for hh in range(2):
                picked = jnp.sum(jnp.where(lane == kj, chosen[hh], 0.0), axis=1, keepdims=True)
                bias = jnp.where(kj == qi - 1, d_ref[hh, 1], far_ref[hh])
                s = jnp.where(picked > 0.0, _nt_dot(qh[hh], k) + bias, MASKED)
                out.append(_online_update(s, v, *st[hh]))
            return tuple(out)

        state = lax.fori_loop(0, qi, k_body, tuple(state))
        o0 = state[0][2] * (1.0 / state[0][1])
        o1 = state[1][2] * (1.0 / state[1][1])
        o_ref[pl.ds(q0, tq), :] = jnp.where(first_head, o0, o1).astype(o_ref.dtype)
        return carry

    lax.fori_loop(0, nblk, q_body, 0)


def _moba_attention(qkv, blk_avg, bias_near, bias_far, B, S, n_fox, n_moba):
    T = B * S
    pairs = n_moba * HEAD_DIM // LANES
    qb = 3 * n_fox * HEAD_DIM // LANES
    kb = qb + pairs
    vb = qb + 2 * pairs
    tq = ATT_TILE
    return pl.pallas_call(
        _moba_kernel,
        grid=(B, pairs),
        in_specs=[pl.BlockSpec((S, LANES), lambda b, p: (b, qb + p)),
                  pl.BlockSpec((S, LANES), lambda b, p: (b, kb + p)),
                  pl.BlockSpec((S, LANES), lambda b, p: (b, vb + p)),
                  pl.BlockSpec((LANES, S), lambda b, p: (0, 0)),
                  pl.BlockSpec((2, 2, tq, tq), lambda b, p: (p, 0, 0, 0)),
                  pl.BlockSpec((2, 1, tq), lambda b, p: (p, 0, 0))],
        out_specs=pl.BlockSpec((S, LANES), lambda b, p: (b, p)),
        out_shape=jax.ShapeDtypeStruct((T, pairs * LANES), BF16),
        compiler_params=_params("parallel", "parallel"),
        name="moba_attention",
    )(qkv, qkv, qkv, blk_avg, bias_near, bias_far)


def _mix_out_kernel(of_ref, om_ref, x_ref, w_ref, g_ref, b_ref, y_ref, *, alpha):
    half = of_ref.shape[1]
    h = (jnp.dot(of_ref[...], w_ref[:half, :], preferred_element_type=F32)
         + jnp.dot(om_ref[...], w_ref[half:, :], preferred_element_type=F32))
    y_ref[...] = _layer_norm(alpha * x_ref[...] + h, g_ref[...], b_ref[...])


def _mix_out(o_f, o_m, x, w, g, b, alpha):
    T, D = x.shape
    half = o_f.shape[1]
    row = lambda i: (i, 0)
    fixed = lambda i: (0, 0)
    return pl.pallas_call(
        functools.partial(_mix_out_kernel, alpha=alpha),
        grid=(T // ROW_TILE,),
        in_specs=[pl.BlockSpec((ROW_TILE, half), row),
                  pl.BlockSpec((ROW_TILE, half), row),
                  pl.BlockSpec((ROW_TILE, D), row),
                  pl.BlockSpec((D, D), fixed),
                  pl.BlockSpec((1, D), fixed),
                  pl.BlockSpec((1, D), fixed)],
        out_specs=pl.BlockSpec((ROW_TILE, D), row),
        out_shape=jax.ShapeDtypeStruct((T, D), F32),
        compiler_params=_params("parallel"),
        name="mix_out_ln",
    )(o_f, o_m, x, w, g.reshape(1, D), b.reshape(1, D))


def _mem_proj_kernel(m_ref, w_ref, kv_ref):
    mb = m_ref[...].astype(BF16)
    n = kv_ref.shape[1]
    for n0 in range(0, n, 512):
        kv_ref[:, n0:n0 + 512] = jnp.dot(
            mb, w_ref[:, n0:n0 + 512], preferred_element_type=F32).astype(BF16)


def _mem_proj(mem2d, w_kv):
    M, D = mem2d.shape
    n = w_kv.shape[1]
    tm = min(ROW_TILE, M)
    return pl.pallas_call(
        _mem_proj_kernel,
        grid=(M // tm,),
        in_specs=[pl.BlockSpec((tm, D), lambda i: (i, 0)),
                  pl.BlockSpec((D, n), lambda i: (0, 0))],
        out_specs=pl.BlockSpec((tm, n), lambda i: (i, 0)),
        out_shape=jax.ShapeDtypeStruct((M, n), BF16),
        compiler_params=_params("parallel"),
        name="mem_proj",
    )(mem2d, w_kv)


def _cross_kernel(x_ref, wq_ref, kv_ref, wo_ref, g_ref, b_ref, wr_hi_ref, wr_lo_ref, br_ref,
                  y_ref, yb_ref, idx_ref, gate_ref, *, alpha, n_experts):
    D = x_ref.shape[1]
    dh = D // N_CROSS_HEADS
    x = x_ref[...]
    q = jnp.dot(x.astype(BF16), wq_ref[...], preferred_element_type=F32).astype(BF16)
    heads = []
    for h in range(N_CROSS_HEADS):
        k_h = kv_ref[:, h * dh:(h + 1) * dh]
        v_h = kv_ref[:, D + h * dh:D + (h + 1) * dh]
        s = _nt_dot(q[:, h * dh:(h + 1) * dh], k_h)
        p = jnp.exp(s - jnp.max(s, axis=1, keepdims=True))
        p = p * (1.0 / jnp.sum(p, axis=1, keepdims=True))
        heads.append(jnp.dot(p.astype(BF16), v_h, preferred_element_type=F32).astype(BF16))
    o = jnp.concatenate(heads, axis=1)
    hproj = jnp.dot(o, wo_ref[...], preferred_element_type=F32)
    y = _layer_norm(alpha * x + hproj, g_ref[...], b_ref[...])
    y_ref[...] = y
    y_hi = y.astype(BF16)
    yb_ref[...] = y_hi

    y_lo = (y - y_hi.astype(F32)).astype(BF16)
    logits = (jnp.dot(y_hi, wr_hi_ref[...], preferred_element_type=F32)
              + jnp.dot(y_lo, wr_hi_ref[...], preferred_element_type=F32)
              + jnp.dot(y_hi, wr_lo_ref[...], preferred_element_type=F32)
              + br_ref[...])
    lane = lax.broadcasted_iota(jnp.int32, logits.shape, 1)
    lane_f = lane.astype(F32)
    work = jnp.where(lane < n_experts, logits, -jnp.inf)
    idx_out = jnp.zeros(logits.shape, F32)
    val_out = jnp.zeros(logits.shape, F32)
    top = None
    for kk in range(TOP_K):
        best = jnp.max(work, axis=1, keepdims=True)
        arg = jnp.min(jnp.where(work == best, lane_f, float(LANES)), axis=1, keepdims=True)
        work = jnp.where(lane_f == arg, -jnp.inf, work)
        if top is None:
            top = best
        idx_out = jnp.where(lane == kk, arg, idx_out)
        val_out = jnp.where(lane == kk, jnp.exp(best - top), val_out)
    idx_ref[...] = idx_out.astype(jnp.int32)
    gate_ref[...] = val_out * (1.0 / jnp.sum(val_out, axis=1, keepdims=True))


def _cross_and_route(x, wq, kv, wo, g, b, wr_hi, wr_lo, br, alpha, S, n_mem, n_experts):
    T, D = x.shape
    tiles_per_batch = S // ROW_TILE
    row = lambda i: (i, 0)
    fixed = lambda i: (0, 0)
    return pl.pallas_call(
        functools.partial(_cross_kernel, alpha=alpha, n_experts=n_experts),
        grid=(T // ROW_TILE,),
        in_specs=[pl.BlockSpec((ROW_TILE, D), row),
                  pl.BlockSpec((D, D), fixed),
                  pl.BlockSpec((n_mem, 2 * D), lambda i: (i // tiles_per_batch, 0)),
                  pl.BlockSpec((D, D), fixed),
                  pl.BlockSpec((1, D), fixed),
                  pl.BlockSpec((1, D), fixed),
                  pl.BlockSpec((D, LANES), fixed),
                  pl.BlockSpec((D, LANES), fixed),
                  pl.BlockSpec((1, LANES), fixed)],
        out_specs=[pl.BlockSpec((ROW_TILE, D), row),
                   pl.BlockSpec((ROW_TILE, D), row),
                   pl.BlockSpec((ROW_TILE, LANES), row),
                   pl.BlockSpec((ROW_TILE, LANES), row)],
        out_shape=[jax.ShapeDtypeStruct((T, D), F32),
                   jax.ShapeDtypeStruct((T, D), BF16),
                   jax.ShapeDtypeStruct((T, LANES), jnp.int32),
                   jax.ShapeDtypeStruct((T, LANES), F32)],
        compiler_params=_params("parallel"),
        name="cross_attn_router",
    )(x, wq, kv, wo, g.reshape(1, D), b.reshape(1, D), wr_hi, wr_lo, br)


def _moe_kernel(be_ref, nv_ref, x_ref, wgu_ref, bgu_ref, wd_ref, bd_ref, y_ref):
    i = pl.program_id(0)
    F = wd_ref.shape[1]

    @pl.when(i < nv_ref[0])
    def _():
        x = x_ref[...]
        acc = None
        for f0 in range(0, F, MOE_FCHUNK):
            f1 = f0 + MOE_FCHUNK
            g = jnp.dot(x, wgu_ref[0, :, f0:f1], preferred_element_type=F32) + bgu_ref[0, :, f0:f1]
            u = (jnp.dot(x, wgu_ref[0, :, F + f0:F + f1], preferred_element_type=F32)
                 + bgu_ref[0, :, F + f0:F + f1])
            g = jnp.minimum(g, SWIGLU_LIMIT)
            u = jnp.clip(u, -SWIGLU_LIMIT, SWIGLU_LIMIT)
            glu = g * jax.nn.sigmoid(g * SWIGLU_ALPHA)
            act = ((u + 1.0) * glu).astype(BF16)
            part = jnp.dot(act, wd_ref[0, f0:f1, :], preferred_element_type=F32)
            acc = part if acc is None else acc + part
        y_ref[...] = acc + bd_ref[0]


def _moe_experts(block_expert, n_valid, xs, wgu, bgu, wd, bd):
    n_rows, D = xs.shape
    E, _, F2 = wgu.shape
    F = F2 // 2
    n_blocks = n_rows // MOE_TILE

    def row(i, be, nv):
        return (jnp.minimum(i, nv[0] - 1), 0)

    def expert3(i, be, nv):
        return (be[i], 0, 0)

    grid_spec = pltpu.PrefetchScalarGridSpec(
        num_scalar_prefetch=2,
        grid=(n_blocks,),
        in_specs=[pl.BlockSpec((MOE_TILE, D), row),
                  pl.BlockSpec((1, D, F2), expert3),
                  pl.BlockSpec((1, 1, F2), expert3),
                  pl.BlockSpec((1, F, D), expert3),
                  pl.BlockSpec((1, 1, D), expert3)],
        out_specs=pl.BlockSpec((MOE_TILE, D), row),
    )
    return pl.pallas_call(
        _moe_kernel,
        grid_spec=grid_spec,
        out_shape=jax.ShapeDtypeStruct((n_rows, D), F32),
        compiler_params=_params("arbitrary"),
        name="moe_experts",
    )(block_expert, n_valid, xs, wgu, bgu.reshape(E, 1, F2), wd, bd.reshape(E, 1, D))


def _combine_kernel(x_ref, ys_ref, gate_ref, g_ref, b_ref, y_ref, *, alpha):
    D = x_ref.shape[1]
    gates = gate_ref[...]
    y = alpha * x_ref[...]
    for kk in range(TOP_K):
        y = y + ys_ref[:, kk * D:(kk + 1) * D] * gates[:, kk:kk + 1]
    y_ref[...] = _layer_norm(y, g_ref[...], b_ref[...])


def _combine_ln(x, ys_tok, gates, g, b, alpha):
    T, D = x.shape
    tm = ROW_TILE // 2
    row = lambda i: (i, 0)
    fixed = lambda i: (0, 0)
    return pl.pallas_call(
        functools.partial(_combine_kernel, alpha=alpha),
        grid=(T // tm,),
        in_specs=[pl.BlockSpec((tm, D), row),
                  pl.BlockSpec((tm, TOP_K * D), row),
                  pl.BlockSpec((tm, LANES), row),
                  pl.BlockSpec((1, D), fixed),
                  pl.BlockSpec((1, D), fixed)],
        out_specs=pl.BlockSpec((tm, D), row),
        out_shape=jax.ShapeDtypeStruct((T, D), F32),
        compiler_params=_params("parallel"),
        name="moe_combine_ln",
    )(x, ys_tok, gates, g.reshape(1, D), b.reshape(1, D))


def _t5_bucket(dist):
    max_exact = N_BUCKETS // 2
    n = jnp.maximum(dist, 0)
    nf = jnp.maximum(n, 1).astype(F32)
    large = max_exact + (jnp.log(nf / max_exact) / math.log(MAX_DISTANCE / max_exact)
                         * (N_BUCKETS - max_exact)).astype(jnp.int32)
    large = jnp.minimum(large, N_BUCKETS - 1)
    return jnp.where(n < max_exact, n, large)


def _moba_bias_tables(rel_bias, S):
    tq = ATT_TILE
    n_heads = rel_bias.shape[1]
    by_dist = rel_bias.T.astype(F32)[:, _t5_bucket(jnp.arange(2 * tq + 1))]
    delta = np.arange(tq)[:, None] - np.arange(tq)[None, :]
    near = jnp.stack([by_dist[:, np.maximum(delta, 0)], by_dist[:, delta + tq]], axis=1)
    far = jnp.broadcast_to(by_dist[:, 2 * tq][:, None, None], (n_heads, 1, tq))
    return near, far


def _block_average_matrix(S):
    a = np.zeros((LANES, S), np.float32)
    for n in range(S // MOBA_BLOCK):
        a[n, n * MOBA_BLOCK:(n + 1) * MOBA_BLOCK] = 1.0 / MOBA_BLOCK
    return jnp.asarray(a, BF16)


def _dispatch(top_idx, n_experts):
    T = top_idx.shape[0]
    TK = T * TOP_K
    e_flat = top_idx.reshape(-1)
    order = jnp.argsort(e_flat).astype(jnp.int32)
    e_sorted = e_flat[order]
    counts = jnp.zeros((n_experts,), jnp.int32).at[e_flat].add(1)
    padded = ((counts + MOE_TILE - 1) // MOE_TILE) * MOE_TILE
    start = jnp.cumsum(counts) - counts
    pend = jnp.cumsum(padded)
    pstart = pend - padded
    n_blocks = TK // MOE_TILE + n_experts
    n_valid = (pend[-1] // MOE_TILE).astype(jnp.int32)
    blk = jnp.minimum(jnp.arange(n_blocks, dtype=jnp.int32), n_valid - 1)
    block_expert = jnp.minimum(
        jnp.searchsorted(pend, blk * MOE_TILE, side='right'), n_experts - 1).astype(jnp.int32)
    slot = jnp.arange(n_blocks * MOE_TILE, dtype=jnp.int32)
    slot_e = jnp.repeat(block_expert, MOE_TILE)
    within = slot - pstart[slot_e]
    src = jnp.clip(start[slot_e] + within, 0, TK - 1)
    slot_token = order[src] // TOP_K
    dest_sorted = pstart[e_sorted] + jnp.arange(TK, dtype=jnp.int32) - start[e_sorted]
    dest = jnp.zeros((TK,), jnp.int32).at[order].set(dest_sorted)
    return slot_token, dest, block_expert, n_valid.reshape(1)


def kernel(x, mem, w_in, b_forget, w_mix_out, rel_bias, ln1_g, ln1_b, w_cq, w_ck, w_cv, w_co,
           ln2_g, ln2_b, w_router, b_router, w_gate_up, b_gate_up, w_down, b_down, ln3_g, ln3_b):
    B, S, D = x.shape
    depth = w_in.shape[0]
    n_mem = mem.shape[1]
    n_experts = w_router.shape[2]
    n_heads = D // HEAD_DIM
    n_fox = n_heads // 2
    n_moba = n_heads - n_fox
    fox_w, moba_w = n_fox * HEAD_DIM, n_moba * HEAD_DIM
    T = B * S
    alpha = (2 * depth) ** 0.25
    assert S % ROW_TILE == 0 and S % ATT_TILE == 0 and D % LANES == 0
    assert ATT_TILE == MOBA_BLOCK and MOBA_BLOCK >= MAX_DISTANCE
    assert n_experts <= LANES and (T * TOP_K) % MOE_TILE == 0

    scale = HEAD_DIM ** -0.5
    c0 = 3 * fox_w
    c1 = c0 + n_fox
    w_fq = w_in[:, :, :fox_w] * scale
    w_fkv = w_in[:, :, fox_w:c0]
    w_fg = jnp.pad(w_in[:, :, c0:c1], ((0, 0), (0, 0), (0, LANES - n_fox)))
    w_mq = w_in[:, :, c1:c1 + moba_w] * scale
    w_mkv = w_in[:, :, c1 + moba_w:]
    w_in_b = jnp.concatenate([w_fq, w_fkv, w_mq, w_mkv, w_fg], axis=2).astype(BF16)

    w_out_b = w_mix_out.astype(BF16)
    w_cq_b = (w_cq * (D // N_CROSS_HEADS) ** -0.5).astype(BF16)
    w_ckv_b = jnp.concatenate([w_ck, w_cv], axis=2).astype(BF16)
    w_co_b = w_co.astype(BF16)
    w_r = jnp.pad(w_router, ((0, 0), (0, 0), (0, LANES - n_experts)))
    w_r_hi = w_r.astype(BF16)
    w_r_lo = (w_r - w_r_hi.astype(F32)).astype(BF16)
    b_r = jnp.pad(b_router, ((0, 0), (0, LANES - n_experts))).reshape(depth, 1, LANES)
    w_gu_b = w_gate_up.astype(BF16)
    w_dn_b = w_down.astype(BF16)

    bias_near, bias_far = _moba_bias_tables(rel_bias, S)
    blk_avg = _block_average_matrix(S)
    mem2d = mem.reshape(B * n_mem, D)
    xt = x.reshape(T, D)

    for l in range(depth):
        qkv, f_logit = _in_proj(xt, w_in_b[l])
        f_t = f_logit[:, :n_fox].reshape(B, S, n_fox).transpose(0, 2, 1)
        c = _fox_decay(f_t, b_forget[l])
        o_f = _fox_attention(qkv, c, B, S, n_fox)
        o_m = _moba_attention(qkv, blk_avg, bias_near, bias_far, B, S, n_fox, n_moba)
        xt = _mix_out(o_f, o_m, xt, w_out_b[l], ln1_g[l], ln1_b[l], alpha)

        kv = _mem_proj(mem2d, w_ckv_b[l])
        xt, xt_b, top_idx, gates = _cross_and_route(
            xt, w_cq_b[l], kv, w_co_b[l], ln2_g[l], ln2_b[l], w_r_hi[l], w_r_lo[l], b_r[l],
            alpha, S, n_mem, n_experts)

        slot_token, dest, block_expert, n_valid = _dispatch(top_idx[:, :TOP_K], n_experts)
        xs = xt_b[slot_token]
        ys = _moe_experts(block_expert, n_valid, xs, w_gu_b[l], b_gate_up[l], w_dn_b[l], b_down[l])
        ys_tok = ys[dest].reshape(T, TOP_K * D)
        xt = _combine_ln(xt, ys_tok, gates, ln3_g[l], ln3_b[l], alpha)

    return xt.reshape(B, S, D)
```

```python
import functools
import math

import jax
import jax.numpy as jnp
import numpy as np
from jax import lax
from jax.experimental import pallas as pl
from jax.experimental.pallas import tpu as pltpu

F32 = jnp.float32
BF16 = jnp.bfloat16

HEAD_DIM = 64
N_BUCKETS = 32
MAX_DISTANCE = 128
MOBA_BLOCK = 256
MOBA_TOPK = 3
N_CROSS_HEADS = 4
TOP_K = 4
SWIGLU_LIMIT = 7.0
SWIGLU_ALPHA = 1.702
LN_EPS = 1e-5

LANES = 128
VMEM_LIMIT = 56 * 1024 * 1024

ROW_TILE = 512
ATT_TILE = MOBA_BLOCK
MOE_TILE = 512
MOE_FCHUNK = 512
MASKED = -1e30


def _params(*sem):
    return pltpu.CompilerParams(dimension_semantics=sem, vmem_limit_bytes=VMEM_LIMIT)


def _nt_dot(a, b):
    return lax.dot_general(a, b, (((1,), (1,)), ((), ())), preferred_element_type=F32)


def _layer_norm(y, g, b):
    mu = jnp.mean(y, axis=-1, keepdims=True)
    d = y - mu
    var = jnp.mean(d * d, axis=-1, keepdims=True)
    return d * lax.rsqrt(var + LN_EPS) * g + b


def _in_proj_kernel(x_ref, w_ref, qkv_ref, f_ref):
    xb = x_ref[...].astype(BF16)
    n_qkv = qkv_ref.shape[1]
    for n0 in range(0, n_qkv, 512):
        qkv_ref[:, n0:n0 + 512] = jnp.dot(
            xb, w_ref[:, n0:n0 + 512], preferred_element_type=F32).astype(BF16)
    f_ref[...] = jnp.dot(xb, w_ref[:, n_qkv:], preferred_element_type=F32)


def _in_proj(x, w):
    T, D = x.shape
    n_all = w.shape[1]
    n_qkv = n_all - LANES
    return pl.pallas_call(
        _in_proj_kernel,
        grid=(T // ROW_TILE,),
        in_specs=[pl.BlockSpec((ROW_TILE, D), lambda i: (i, 0)),
                  pl.BlockSpec((D, n_all), lambda i: (0, 0))],
        out_specs=[pl.BlockSpec((ROW_TILE, n_qkv), lambda i: (i, 0)),
                   pl.BlockSpec((ROW_TILE, LANES), lambda i: (i, 0))],
        out_shape=[jax.ShapeDtypeStruct((T, n_qkv), BF16),
                   jax.ShapeDtypeStruct((T, LANES), F32)],
        compiler_params=_params("parallel"),
        name="in_proj",
    )(x, w)


def _split3(x):
    p1 = x.astype(BF16)
    r1 = x - p1.astype(F32)
    p2 = r1.astype(BF16)
    p3 = (r1 - p2.astype(F32)).astype(BF16)
    return p1, p2, p3


def _decay_kernel(f_ref, b_ref, tri_ref, place_ref, c_ref):
    S = f_ref.shape[0]
    blk = tri_ref.shape[0]
    carry = jnp.zeros((1, LANES), F32)
    for j in range(S // blk):
        z = f_ref[j * blk:(j + 1) * blk, :] + b_ref[...]
        ls = jnp.minimum(z, 0.0) - jnp.log1p(jnp.exp(-jnp.abs(z)))
        c = carry
        for piece in _split3(ls):
            c = c + jnp.dot(tri_ref[...], piece, preferred_element_type=F32)
        carry = c[blk - 1:blk, :]
        aug = None
        for i, piece in enumerate(_split3(c)):
            t = jnp.dot(piece, place_ref[i], preferred_element_type=F32)
            aug = t if aug is None else aug + t
        c_ref[j * blk:(j + 1) * blk, :] = aug.astype(BF16)


def _fox_decay(f_logit, b_forget, S, n_fox):
    T = f_logit.shape[0]
    pairs = n_fox * HEAD_DIM // LANES
    blk = ATT_TILE
    tri = jnp.asarray(np.tril(np.ones((blk, blk), np.float32)), BF16)
    place = np.zeros((3, LANES, pairs * LANES), np.float32)
    for h in range(n_fox):
        for i in range(3):
            place[i, h, (h // 2) * LANES + 3 * (h % 2) + i] = -1.0
    b_pad = jnp.pad(b_forget, (0, LANES - n_fox)).reshape(1, LANES)
    return pl.pallas_call(
        _decay_kernel,
        grid=(T // S,),
        in_specs=[pl.BlockSpec((S, LANES), lambda b: (b, 0)),
                  pl.BlockSpec((1, LANES), lambda b: (0, 0)),
                  pl.BlockSpec((blk, blk), lambda b: (0, 0)),
                  pl.BlockSpec((3, LANES, pairs * LANES), lambda b: (0, 0, 0))],
        out_specs=pl.BlockSpec((S, pairs * LANES), lambda b: (b, 0)),
        out_shape=jax.ShapeDtypeStruct((T, pairs * LANES), BF16),
        compiler_params=_params("parallel"),
        name="fox_decay",
    )(f_logit, b_pad, tri, jnp.asarray(place, BF16))


def _online_update_t(s, vt, m, l, acc):
    m_new = jnp.maximum(m, jnp.max(s, axis=0, keepdims=True))
    alpha = jnp.exp(m - m_new)
    p = jnp.exp(s - m_new)
    l_new = alpha * l + jnp.sum(p, axis=0, keepdims=True)
    acc_new = alpha * acc + jnp.dot(vt, p.astype(BF16), preferred_element_type=F32)
    return m_new, l_new, acc_new


def _init_state(tq):
    return (jnp.full((1, tq), MASKED, F32), jnp.zeros((1, tq), F32),
            jnp.zeros((HEAD_DIM, tq), F32))


def _split_heads(q):
    lane = lax.broadcasted_iota(jnp.int32, q.shape, 1)
    zero = jnp.zeros((), q.dtype)
    return jnp.where(lane < HEAD_DIM, q, zero), jnp.where(lane < HEAD_DIM, zero, q)


def _causal_mask_t(tq):
    key = lax.broadcasted_iota(jnp.int32, (tq, tq), 0)
    qry = lax.broadcasted_iota(jnp.int32, (tq, tq), 1)
    return key <= qry


def _transpose_values(v_ref, vt_scr):
    tq = ATT_TILE
    for j in range(v_ref.shape[0] // tq):
        vt_scr[:, j * tq:(j + 1) * tq] = v_ref[j * tq:(j + 1) * tq, :].astype(F32).T.astype(BF16)


def _store_heads(o_ref, q0, state):
    tq = ATT_TILE
    o_t = jnp.concatenate([st[2] * (1.0 / st[1]) for st in state], axis=0)
    o_ref[pl.ds(q0, tq), :] = o_t.T.astype(o_ref.dtype)


def _to_weights(x):
    return x.astype(F32).T.astype(BF16)


def _run_causal_tiles(n_blocks, score_tile, vt_scr, o_ref):
    tq = ATT_TILE
    tiles = [(qi, kb) for qi in range(n_blocks) for kb in [qi] + list(range(qi))]
    cur = score_tile(*tiles[0])
    state = None
    for i, (qi, kb) in enumerate(tiles):
        nxt = score_tile(*tiles[i + 1]) if i + 1 < len(tiles) else None
        if kb == qi:
            state = (_init_state(tq), _init_state(tq))
        new_state = []
        for hh in range(2):
            vt = vt_scr[hh * HEAD_DIM:(hh + 1) * HEAD_DIM, kb * tq:(kb + 1) * tq]
            new_state.append(_online_update_t(cur[hh], vt, *state[hh]))
        state = tuple(new_state)
        if kb == max(qi - 1, 0):
            _store_heads(o_ref, qi * tq, state)
        cur = nxt


def _fox_kernel(q_ref, k_ref, v_ref, c_ref, o_ref, vt_scr):
    tq = ATT_TILE
    S = q_ref.shape[0]
    causal = _causal_mask_t(tq)
    lane = lax.broadcasted_iota(jnp.int32, (tq, LANES), 1)
    ones = [jnp.where((lane >= 3 * hh) & (lane < 3 * hh + 3), 1.0, 0.0).astype(BF16)
            for hh in range(2)]
    _transpose_values(v_ref, vt_scr)
    q_weights = {}

    def score_tile(qi, kb):
        if qi not in q_weights:
            qh = _split_heads(q_ref[qi * tq:(qi + 1) * tq, :])
            q_weights.clear()
            q_weights[qi] = [_to_weights(jnp.concatenate([qh[hh], ones[hh]], axis=1))
                             for hh in range(2)]
        rows = slice(kb * tq, (kb + 1) * tq)
        ka = jnp.concatenate([k_ref[rows, :], c_ref[rows, :]], axis=1)
        out = []
        for hh in range(2):
            s = jnp.dot(ka, q_weights[qi][hh], preferred_element_type=F32)
            out.append(jnp.where(causal, s, MASKED) if kb == qi else s)
        return tuple(out)

    _run_causal_tiles(S // tq, score_tile, vt_scr, o_ref)


def _fox_attention(qkv, c_aug, B, S, n_fox):
    T = B * S
    pairs = n_fox * HEAD_DIM // LANES
    kb = pairs
    vb = 2 * pairs
    return pl.pallas_call(
        _fox_kernel,
        grid=(B, pairs),
        in_specs=[pl.BlockSpec((S, LANES), lambda b, p: (b, p)),
                  pl.BlockSpec((S, LANES), lambda b, p: (b, kb + p)),
                  pl.BlockSpec((S, LANES), lambda b, p: (b, vb + p)),
                  pl.BlockSpec((S, LANES), lambda b, p: (b, p))],
        out_specs=pl.BlockSpec((S, LANES), lambda b, p: (b, p)),
        out_shape=jax.ShapeDtypeStruct((T, pairs * LANES), BF16),
        scratch_shapes=[pltpu.VMEM((LANES, S), BF16)],
        compiler_params=_params("parallel", "parallel"),
        name="fox_attention",
    )(qkv, qkv, qkv, c_aug)


def _moba_kernel(q_ref, k_ref, v_ref, a_ref, bias_ref, o_ref, vt_scr):
    tq = ATT_TILE
    S = q_ref.shape[0]
    rows = -(-(S // tq) // 8) * 8
    causal = _causal_mask_t(tq)
    blk_row = lax.broadcasted_iota(jnp.int32, (rows, tq), 0)
    _transpose_values(v_ref, vt_scr)

    kmean = jnp.dot(a_ref[...], k_ref[...], preferred_element_type=F32)
    kmean_hi = kmean.astype(BF16)
    kmean_lo = (kmean - kmean_hi.astype(F32)).astype(BF16)
    per_block = {}

    def prepare(qi):
        qh = _split_heads(q_ref[qi * tq:(qi + 1) * tq, :])
        qw = [_to_weights(qh[hh]) for hh in range(2)]
        offsets = []
        for hh in range(2):
            gate = (jnp.dot(kmean_hi, qw[hh], preferred_element_type=F32)
                    + jnp.dot(kmean_lo, qw[hh], preferred_element_type=F32))[:rows, :]
            rank = jnp.zeros((rows, tq), jnp.int32)
            for mb in range(qi):
                g_m = gate[mb:mb + 1, :]
                tie = jnp.where(blk_row > mb, 1, 0)
                rank = rank + jnp.where(g_m > gate, 1, jnp.where(g_m == gate, tie, 0))
            offsets.append(jnp.where(rank < MOBA_TOPK, 0.0, MASKED))
        return qw, offsets

    def score_tile(qi, kb):
        if qi not in per_block:
            per_block.clear()
            per_block[qi] = prepare(qi)
        qw, offsets = per_block[qi]
        k = k_ref[kb * tq:(kb + 1) * tq, :]
        out = []
        for hh in range(2):
            s = jnp.dot(k, qw[hh], preferred_element_type=F32)
            if kb == qi:
                s = jnp.where(causal, s + bias_ref[hh, 0], MASKED)
            elif kb == qi - 1:
                s = s + bias_ref[hh, 1] + offsets[hh][kb:kb + 1, :]
            else:
                s = s + (bias_ref[hh, 2, 0:1, :] + offsets[hh][kb:kb + 1, :])
            out.append(s)
        return tuple(out)

    _run_causal_tiles(S // tq, score_tile, vt_scr, o_ref)


def _moba_attention(qkv, blk_avg, bias_tiles, B, S, n_fox, n_moba):
    T = B * S
    pairs = n_moba * HEAD_DIM // LANES
    qb = 3 * n_fox * HEAD_DIM // LANES
    kb = qb + pairs
    vb = qb + 2 * pairs
    tq = ATT_TILE
    return pl.pallas_call(
        _moba_kernel,
        grid=(B, pairs),
        in_specs=[pl.BlockSpec((S, LANES), lambda b, p: (b, qb + p)),
                  pl.BlockSpec((S, LANES), lambda b, p: (b, kb + p)),
                  pl.BlockSpec((S, LANES), lambda b, p: (b, vb + p)),
                  pl.BlockSpec((LANES, S), lambda b, p: (0, 0)),
                  pl.BlockSpec((2, 3, tq, tq), lambda b, p: (p, 0, 0, 0))],
        out_specs=pl.BlockSpec((S, LANES), lambda b, p: (b, p)),
        out_shape=jax.ShapeDtypeStruct((T, pairs * LANES), BF16),
        scratch_shapes=[pltpu.VMEM((LANES, S), BF16)],
        compiler_params=_params("parallel", "parallel"),
        name="moba_attention",
    )(qkv, qkv, qkv, blk_avg, bias_tiles)


def _mix_out_kernel(of_ref, om_ref, x_ref, w_ref, g_ref, b_ref, y_ref, *, alpha):
    half = of_ref.shape[1]
    h = (jnp.dot(of_ref[...], w_ref[:half, :], preferred_element_type=F32)
         + jnp.dot(om_ref[...], w_ref[half:, :], preferred_element_type=F32))
    y_ref[...] = _layer_norm(alpha * x_ref[...] + h, g_ref[...], b_ref[...])


def _mix_out(o_f, o_m, x, w, g, b, alpha):
    T, D = x.shape
    half = o_f.shape[1]
    row = lambda i: (i, 0)
    fixed = lambda i: (0, 0)
    return pl.pallas_call(
        functools.partial(_mix_out_kernel, alpha=alpha),
        grid=(T // ROW_TILE,),
        in_specs=[pl.BlockSpec((ROW_TILE, half), row),
                  pl.BlockSpec((ROW_TILE, half), row),
                  pl.BlockSpec((ROW_TILE, D), row),
                  pl.BlockSpec((D, D), fixed),
                  pl.BlockSpec((1, D), fixed),
                  pl.BlockSpec((1, D), fixed)],
        out_specs=pl.BlockSpec((ROW_TILE, D), row),
        out_shape=jax.ShapeDtypeStruct((T, D), F32),
        compiler_params=_params("parallel"),
        name="mix_out_ln",
    )(o_f, o_m, x, w, g.reshape(1, D), b.reshape(1, D))


def _mem_proj_kernel(m_ref, w_ref, kv_ref):
    mb = m_ref[...].astype(BF16)
    n = kv_ref.shape[1]
    for n0 in range(0, n, 512):
        kv_ref[:, n0:n0 + 512] = jnp.dot(
            mb, w_ref[:, n0:n0 + 512], preferred_element_type=F32).astype(BF16)


def _mem_proj(mem2d, w_kv):
    M, D = mem2d.shape
    n = w_kv.shape[1]
    tm = min(ROW_TILE, M)
    return pl.pallas_call(
        _mem_proj_kernel,
        grid=(M // tm,),
        in_specs=[pl.BlockSpec((tm, D), lambda i: (i, 0)),
                  pl.BlockSpec((D, n), lambda i: (0, 0))],
        out_specs=pl.BlockSpec((tm, n), lambda i: (i, 0)),
        out_shape=jax.ShapeDtypeStruct((M, n), BF16),
        compiler_params=_params("parallel"),
        name="mem_proj",
    )(mem2d, w_kv)


def _cross_kernel(x_ref, wq_ref, kv_ref, wo_ref, g_ref, b_ref, wr_hi_ref, wr_lo_ref, br_ref,
                  y_ref, yb_ref, idx_ref, gate_ref, *, alpha, n_experts):
    D = x_ref.shape[1]
    dh = D // N_CROSS_HEADS
    x = x_ref[...]
    q = jnp.dot(x.astype(BF16), wq_ref[...], preferred_element_type=F32).astype(BF16)
    heads = []
    for h in range(N_CROSS_HEADS):
        k_h = kv_ref[:, h * dh:(h + 1) * dh]
        v_h = kv_ref[:, D + h * dh:D + (h + 1) * dh]
        s = _nt_dot(q[:, h * dh:(h + 1) * dh], k_h)
        p = jnp.exp(s - jnp.max(s, axis=1, keepdims=True))
        p = p * (1.0 / jnp.sum(p, axis=1, keepdims=True))
        heads.append(jnp.dot(p.astype(BF16), v_h, preferred_element_type=F32).astype(BF16))
    o = jnp.concatenate(heads, axis=1)
    hproj = jnp.dot(o, wo_ref[...], preferred_element_type=F32)
    y = _layer_norm(alpha * x + hproj, g_ref[...], b_ref[...])
    y_ref[...] = y
    y_hi = y.astype(BF16)
    yb_ref[...] = y_hi

    y_lo = (y - y_hi.astype(F32)).astype(BF16)
    logits = (jnp.dot(y_hi, wr_hi_ref[...], preferred_element_type=F32)
              + jnp.dot(y_lo, wr_hi_ref[...], preferred_element_type=F32)
              + jnp.dot(y_hi, wr_lo_ref[...], preferred_element_type=F32)
              + br_ref[...])
    lane = lax.broadcasted_iota(jnp.int32, logits.shape, 1)
    lane_f = lane.astype(F32)
    work = jnp.where(lane < n_experts, logits, -jnp.inf)
    idx_out = jnp.zeros(logits.shape, F32)
    val_out = jnp.zeros(logits.shape, F32)
    top = None
    for kk in range(TOP_K):
        best = jnp.max(work, axis=1, keepdims=True)
        arg = jnp.min(jnp.where(work == best, lane_f, float(LANES)), axis=1, keepdims=True)
        work = jnp.where(lane_f == arg, -jnp.inf, work)
        if top is None:
            top = best
        idx_out = jnp.where(lane == kk, arg, idx_out)
        val_out = jnp.where(lane == kk, jnp.exp(best - top), val_out)
    idx_ref[...] = idx_out.astype(jnp.int32)
    gate_ref[...] = val_out * (1.0 / jnp.sum(val_out, axis=1, keepdims=True))


def _cross_and_route(x, wq, kv, wo, g, b, wr_hi, wr_lo, br, alpha, S, n_mem, n_experts):
    T, D = x.shape
    tiles_per_batch = S // ROW_TILE
    row = lambda i: (i, 0)
    fixed = lambda i: (0, 0)
    return pl.pallas_call(
        functools.partial(_cross_kernel, alpha=alpha, n_experts=n_experts),
        grid=(T // ROW_TILE,),
        in_specs=[pl.BlockSpec((ROW_TILE, D), row),
                  pl.BlockSpec((D, D), fixed),
                  pl.BlockSpec((n_mem, 2 * D), lambda i: (i // tiles_per_batch, 0)),
                  pl.BlockSpec((D, D), fixed),
                  pl.BlockSpec((1, D), fixed),
                  pl.BlockSpec((1, D), fixed),
                  pl.BlockSpec((D, LANES), fixed),
                  pl.BlockSpec((D, LANES), fixed),
                  pl.BlockSpec((1, LANES), fixed)],
        out_specs=[pl.BlockSpec((ROW_TILE, D), row),
                   pl.BlockSpec((ROW_TILE, D), row),
                   pl.BlockSpec((ROW_TILE, LANES), row),
                   pl.BlockSpec((ROW_TILE, LANES), row)],
        out_shape=[jax.ShapeDtypeStruct((T, D), F32),
                   jax.ShapeDtypeStruct((T, D), BF16),
                   jax.ShapeDtypeStruct((T, LANES), jnp.int32),
                   jax.ShapeDtypeStruct((T, LANES), F32)],
        compiler_params=_params("parallel"),
        name="cross_attn_router",
    )(x, wq, kv, wo, g.reshape(1, D), b.reshape(1, D), wr_hi, wr_lo, br)


def _moe_kernel(be_ref, nv_ref, x_ref, wgu_ref, bgu_ref, wd_ref, bd_ref, y_ref):
    i = pl.program_id(0)
    F = wd_ref.shape[1]

    @pl.when(i < nv_ref[0])
    def _():
        x = x_ref[...]
        acc = None
        for f0 in range(0, F, MOE_FCHUNK):
            f1 = f0 + MOE_FCHUNK
            g = jnp.dot(x, wgu_ref[0, :, f0:f1], preferred_element_type=F32) + bgu_ref[0, :, f0:f1]
            u = (jnp.dot(x, wgu_ref[0, :, F + f0:F + f1], preferred_element_type=F32)
                 + bgu_ref[0, :, F + f0:F + f1])
            g = jnp.minimum(g, SWIGLU_LIMIT)
            u = jnp.clip(u, -SWIGLU_LIMIT, SWIGLU_LIMIT)
            glu = g * jax.nn.sigmoid(g * SWIGLU_ALPHA)
            act = ((u + 1.0) * glu).astype(BF16)
            part = jnp.dot(act, wd_ref[0, f0:f1, :], preferred_element_type=F32)
            acc = part if acc is None else acc + part
        y_ref[...] = acc + bd_ref[0]


def _moe_experts(block_expert, n_valid, xs, wgu, bgu, wd, bd):
    n_rows, D = xs.shape
    E, _, F2 = wgu.shape
    F = F2 // 2
    n_blocks = n_rows // MOE_TILE

    def row(i, be, nv):
        return (jnp.minimum(i, nv[0] - 1), 0)

    def expert3(i, be, nv):
        return (be[i], 0, 0)

    grid_spec = pltpu.PrefetchScalarGridSpec(
        num_scalar_prefetch=2,
        grid=(n_blocks,),
        in_specs=[pl.BlockSpec((MOE_TILE, D), row),
                  pl.BlockSpec((1, D, F2), expert3),
                  pl.BlockSpec((1, 1, F2), expert3),
                  pl.BlockSpec((1, F, D), expert3),
                  pl.BlockSpec((1, 1, D), expert3)],
        out_specs=pl.BlockSpec((MOE_TILE, D), row),
    )
    return pl.pallas_call(
        _moe_kernel,
        grid_spec=grid_spec,
        out_shape=jax.ShapeDtypeStruct((n_rows, D), F32),
        compiler_params=_params("arbitrary"),
        name="moe_experts",
    )(block_expert, n_valid, xs, wgu, bgu.reshape(E, 1, F2), wd, bd.reshape(E, 1, D))


def _combine_kernel(x_ref, ys_ref, gate_ref, g_ref, b_ref, y_ref, *, alpha):
    D = x_ref.shape[1]
    gates = gate_ref[...]
    y = alpha * x_ref[...]
    for kk in range(TOP_K):
        y = y + ys_ref[:, kk * D:(kk + 1) * D] * gates[:, kk:kk + 1]
    y_ref[...] = _layer_norm(y, g_ref[...], b_ref[...])


def _combine_ln(x, ys_tok, gates, g, b, alpha):
    T, D = x.shape
    tm = ROW_TILE // 2
    row = lambda i: (i, 0)
    fixed = lambda i: (0, 0)
    return pl.pallas_call(
        functools.partial(_combine_kernel, alpha=alpha),
        grid=(T // tm,),
        in_specs=[pl.BlockSpec((tm, D), row),
                  pl.BlockSpec((tm, TOP_K * D), row),
                  pl.BlockSpec((tm, LANES), row),
                  pl.BlockSpec((1, D), fixed),
                  pl.BlockSpec((1, D), fixed)],
        out_specs=pl.BlockSpec((tm, D), row),
        out_shape=jax.ShapeDtypeStruct((T, D), F32),
        compiler_params=_params("parallel"),
        name="moe_combine_ln",
    )(x, ys_tok, gates, g.reshape(1, D), b.reshape(1, D))


def _t5_bucket(dist):
    max_exact = N_BUCKETS // 2
    n = jnp.maximum(dist, 0)
    nf = jnp.maximum(n, 1).astype(F32)
    large = max_exact + (jnp.log(nf / max_exact) / math.log(MAX_DISTANCE / max_exact)
                         * (N_BUCKETS - max_exact)).astype(jnp.int32)
    large = jnp.minimum(large, N_BUCKETS - 1)
    return jnp.where(n < max_exact, n, large)


def _moba_bias_tables(rel_bias, S):
    tq = ATT_TILE
    n_heads = rel_bias.shape[1]
    by_dist = rel_bias.T.astype(F32)[:, _t5_bucket(jnp.arange(2 * tq + 1))]
    delta = np.arange(tq)[None, :] - np.arange(tq)[:, None]
    far = jnp.broadcast_to(by_dist[:, 2 * tq][:, None, None], (n_heads, tq, tq))
    return jnp.stack([by_dist[:, np.maximum(delta, 0)], by_dist[:, delta + tq], far], axis=1)


def _block_average_matrix(S):
    a = np.zeros((LANES, S), np.float32)
    for n in range(S // MOBA_BLOCK):
        a[n, n * MOBA_BLOCK:(n + 1) * MOBA_BLOCK] = 1.0 / MOBA_BLOCK
    return jnp.asarray(a, BF16)


def _dispatch(top_idx, n_experts):
    T = top_idx.shape[0]
    TK = T * TOP_K
    e_flat = top_idx.reshape(-1)
    order = jnp.argsort(e_flat).astype(jnp.int32)
    e_sorted = e_flat[order]
    counts = jnp.zeros((n_experts,), jnp.int32).at[e_flat].add(1)
    padded = ((counts + MOE_TILE - 1) // MOE_TILE) * MOE_TILE
    start = jnp.cumsum(counts) - counts
    pend = jnp.cumsum(padded)
    pstart = pend - padded
    n_blocks = TK // MOE_TILE + n_experts
    n_valid = (pend[-1] // MOE_TILE).astype(jnp.int32)
    blk = jnp.minimum(jnp.arange(n_blocks, dtype=jnp.int32), n_valid - 1)
    block_expert = jnp.minimum(
        jnp.searchsorted(pend, blk * MOE_TILE, side='right'), n_experts - 1).astype(jnp.int32)
    slot = jnp.arange(n_blocks * MOE_TILE, dtype=jnp.int32)
    slot_e = jnp.repeat(block_expert, MOE_TILE)
    within = slot - pstart[slot_e]
    src = jnp.clip(start[slot_e] + within, 0, TK - 1)
    slot_token = order[src] // TOP_K
    dest_sorted = pstart[e_sorted] + jnp.arange(TK, dtype=jnp.int32) - start[e_sorted]
    dest = jnp.zeros((TK,), jnp.int32).at[order].set(dest_sorted)
    return slot_token, dest, block_expert, n_valid.reshape(1)


def kernel(x, mem, w_in, b_forget, w_mix_out, rel_bias, ln1_g, ln1_b, w_cq, w_ck, w_cv, w_co,
           ln2_g, ln2_b, w_router, b_router, w_gate_up, b_gate_up, w_down, b_down, ln3_g, ln3_b):
    B, S, D = x.shape
    depth = w_in.shape[0]
    n_mem = mem.shape[1]
    n_experts = w_router.shape[2]
    n_heads = D // HEAD_DIM
    n_fox = n_heads // 2
    n_moba = n_heads - n_fox
    fox_w, moba_w = n_fox * HEAD_DIM, n_moba * HEAD_DIM
    T = B * S
    alpha = (2 * depth) ** 0.25
    assert S % ROW_TILE == 0 and S % ATT_TILE == 0 and D % LANES == 0
    assert ATT_TILE == MOBA_BLOCK and MOBA_BLOCK >= MAX_DISTANCE
    assert n_experts <= LANES and (T * TOP_K) % MOE_TILE == 0

    scale = HEAD_DIM ** -0.5
    c0 = 3 * fox_w
    c1 = c0 + n_fox
    w_fq = w_in[:, :, :fox_w] * scale
    w_fkv = w_in[:, :, fox_w:c0]
    w_fg = jnp.pad(w_in[:, :, c0:c1], ((0, 0), (0, 0), (0, LANES - n_fox)))
    w_mq = w_in[:, :, c1:c1 + moba_w] * scale
    w_mkv = w_in[:, :, c1 + moba_w:]
    w_in_b = jnp.concatenate([w_fq, w_fkv, w_mq, w_mkv, w_fg], axis=2).astype(BF16)

    w_out_b = w_mix_out.astype(BF16)
    w_cq_b = (w_cq * (D // N_CROSS_HEADS) ** -0.5).astype(BF16)
    w_ckv_b = jnp.concatenate([w_ck, w_cv], axis=2).astype(BF16)
    w_co_b = w_co.astype(BF16)
    w_r = jnp.pad(w_router, ((0, 0), (0, 0), (0, LANES - n_experts)))
    w_r_hi = w_r.astype(BF16)
    w_r_lo = (w_r - w_r_hi.astype(F32)).astype(BF16)
    b_r = jnp.pad(b_router, ((0, 0), (0, LANES - n_experts))).reshape(depth, 1, LANES)
    w_gu_b = w_gate_up.astype(BF16)
    w_dn_b = w_down.astype(BF16)

    bias_tiles = _moba_bias_tables(rel_bias, S)
    blk_avg = _block_average_matrix(S)
    mem2d = mem.reshape(B * n_mem, D)
    xt = x.reshape(T, D)

    for l in range(depth):
        qkv, f_logit = _in_proj(xt, w_in_b[l])
        c_aug = _fox_decay(f_logit, b_forget[l], S, n_fox)
        o_f = _fox_attention(qkv, c_aug, B, S, n_fox)
        o_m = _moba_attention(qkv, blk_avg, bias_tiles, B, S, n_fox, n_moba)
        xt = _mix_out(o_f, o_m, xt, w_out_b[l], ln1_g[l], ln1_b[l], alpha)

        kv = _mem_proj(mem2d, w_ckv_b[l])
        xt, xt_b, top_idx, gates = _cross_and_route(
            xt, w_cq_b[l], kv, w_co_b[l], ln2_g[l], ln2_b[l], w_r_hi[l], w_r_lo[l], b_r[l],
            alpha, S, n_mem, n_experts)

        slot_token, dest, block_expert, n_valid = _dispatch(top_idx[:, :TOP_K], n_experts)
        xs = xt_b[slot_token]
        ys = _moe_experts(block_expert, n_valid, xs, w_gu_b[l], b_gate_up[l], w_dn_b[l], b_down[l])
        ys_tok = ys[dest].reshape(T, TOP_K * D)
        xt = _combine_ln(xt, ys_tok, gates, ln3_g[l], ln3_b[l], alpha)

    return xt.reshape(B, S, D)
```

```python
import functools
import math

import jax
import jax.numpy as jnp
import numpy as np
from jax import lax
from jax.experimental import pallas as pl
from jax.experimental.pallas import tpu as pltpu

F32 = jnp.float32
BF16 = jnp.bfloat16

HEAD_DIM = 64
N_BUCKETS = 32
MAX_DISTANCE = 128
MOBA_BLOCK = 256
MOBA_TOPK = 3
N_CROSS_HEADS = 4
TOP_K = 4
SWIGLU_LIMIT = 7.0
SWIGLU_ALPHA = 1.702
LN_EPS = 1e-5

LANES = 128
VMEM_LIMIT = 56 * 1024 * 1024

ROW_TILE = 512
ATT_TILE = MOBA_BLOCK
MOE_TILE = 512
MOE_FCHUNK = 512
MASKED = -1e30
LOG2E = math.log2(math.e)


def _params(*sem):
    return pltpu.CompilerParams(dimension_semantics=sem, vmem_limit_bytes=VMEM_LIMIT)


def _nt_dot(a, b):
    return lax.dot_general(a, b, (((1,), (1,)), ((), ())), preferred_element_type=F32)


def _layer_norm(y, g, b):
    mu = jnp.mean(y, axis=-1, keepdims=True)
    d = y - mu
    var = jnp.mean(d * d, axis=-1, keepdims=True)
    return d * lax.rsqrt(var + LN_EPS) * g + b


def _in_proj_kernel(x_ref, w_ref, qkv_ref, f_ref):
    xb = x_ref[...].astype(BF16)
    n_qkv = qkv_ref.shape[1]
    for n0 in range(0, n_qkv, 512):
        qkv_ref[:, n0:n0 + 512] = jnp.dot(
            xb, w_ref[:, n0:n0 + 512], preferred_element_type=F32).astype(BF16)
    f_ref[...] = jnp.dot(xb, w_ref[:, n_qkv:], preferred_element_type=F32)


def _in_proj(x, w):
    T, D = x.shape
    n_all = w.shape[1]
    n_qkv = n_all - LANES
    return pl.pallas_call(
        _in_proj_kernel,
        grid=(T // ROW_TILE,),
        in_specs=[pl.BlockSpec((ROW_TILE, D), lambda i: (i, 0)),
                  pl.BlockSpec((D, n_all), lambda i: (0, 0))],
        out_specs=[pl.BlockSpec((ROW_TILE, n_qkv), lambda i: (i, 0)),
                   pl.BlockSpec((ROW_TILE, LANES), lambda i: (i, 0))],
        out_shape=[jax.ShapeDtypeStruct((T, n_qkv), BF16),
                   jax.ShapeDtypeStruct((T, LANES), F32)],
        compiler_params=_params("parallel"),
        name="in_proj",
    )(x, w)


def _split3(x):
    p1 = x.astype(BF16)
    r1 = x - p1.astype(F32)
    p2 = r1.astype(BF16)
    p3 = (r1 - p2.astype(F32)).astype(BF16)
    return p1, p2, p3


def _decay_kernel(f_ref, b_ref, tri_ref, place_ref, c_ref):
    S = f_ref.shape[0]
    blk = tri_ref.shape[0]
    carry = jnp.zeros((1, LANES), F32)
    for j in range(S // blk):
        z = f_ref[j * blk:(j + 1) * blk, :] + b_ref[...]
        ls = jnp.minimum(z, 0.0) - jnp.log1p(jnp.exp(-jnp.abs(z)))
        c = carry
        for piece in _split3(ls):
            c = c + jnp.dot(tri_ref[...], piece, preferred_element_type=F32)
        carry = c[blk - 1:blk, :]
        aug = None
        for i, piece in enumerate(_split3(c * LOG2E)):
            t = jnp.dot(piece, place_ref[i], preferred_element_type=F32)
            aug = t if aug is None else aug + t
        c_ref[j * blk:(j + 1) * blk, :] = aug.astype(BF16)


def _fox_decay(f_logit, b_forget, S, n_fox):
    T = f_logit.shape[0]
    pairs = n_fox * HEAD_DIM // LANES
    blk = ATT_TILE
    tri = jnp.asarray(np.tril(np.ones((blk, blk), np.float32)), BF16)
    place = np.zeros((3, LANES, pairs * LANES), np.float32)
    for h in range(n_fox):
        for i in range(3):
            place[i, h, (h // 2) * LANES + 3 * (h % 2) + i] = -1.0
    b_pad = jnp.pad(b_forget, (0, LANES - n_fox)).reshape(1, LANES)
    return pl.pallas_call(
        _decay_kernel,
        grid=(T // S,),
        in_specs=[pl.BlockSpec((S, LANES), lambda b: (b, 0)),
                  pl.BlockSpec((1, LANES), lambda b: (0, 0)),
                  pl.BlockSpec((blk, blk), lambda b: (0, 0)),
                  pl.BlockSpec((3, LANES, pairs * LANES), lambda b: (0, 0, 0))],
        out_specs=pl.BlockSpec((S, pairs * LANES), lambda b: (b, 0)),
        out_shape=jax.ShapeDtypeStruct((T, pairs * LANES), BF16),
        compiler_params=_params("parallel"),
        name="fox_decay",
    )(f_logit, b_pad, tri, jnp.asarray(place, BF16))


VT_ROWS = HEAD_DIM + 16


def _online_update_t(s, vt, m, acc):
    m_new = jnp.maximum(m, jnp.max(s, axis=0, keepdims=True))
    p = jnp.exp2(s - m_new)
    acc_new = jnp.exp2(m - m_new) * acc + jnp.dot(vt, p.astype(BF16), preferred_element_type=F32)
    return m_new, acc_new


def _init_state(tq):
    return jnp.full((1, tq), MASKED, F32), jnp.zeros((VT_ROWS, tq), F32)


def _split_heads(q):
    lane = lax.broadcasted_iota(jnp.int32, q.shape, 1)
    zero = jnp.zeros((), q.dtype)
    return jnp.where(lane < HEAD_DIM, q, zero), jnp.where(lane < HEAD_DIM, zero, q)


def _causal_mask_t(tq):
    key = lax.broadcasted_iota(jnp.int32, (tq, tq), 0)
    qry = lax.broadcasted_iota(jnp.int32, (tq, tq), 1)
    return key <= qry


def _transpose_values(v_ref, vt_scr):
    tq = ATT_TILE
    S = v_ref.shape[0]
    for hh in range(2):
        vt_scr[hh * VT_ROWS + HEAD_DIM:(hh + 1) * VT_ROWS, :] = jnp.ones(
            (VT_ROWS - HEAD_DIM, S), BF16)
    for j in range(S // tq):
        vt = v_ref[j * tq:(j + 1) * tq, :].astype(F32).T.astype(BF16)
        for hh in range(2):
            vt_scr[hh * VT_ROWS:hh * VT_ROWS + HEAD_DIM, j * tq:(j + 1) * tq] = (
                vt[hh * HEAD_DIM:(hh + 1) * HEAD_DIM, :])


def _store_heads(o_ref, q0, state):
    tq = ATT_TILE
    o_t = jnp.concatenate(
        [acc[:HEAD_DIM] * (1.0 / acc[HEAD_DIM:HEAD_DIM + 1]) for _, acc in state], axis=0)
    o_ref[pl.ds(q0, tq), :] = o_t.T.astype(o_ref.dtype)


def _to_weights(x):
    return x.astype(F32).T.astype(BF16)


def _run_causal_tiles(n_blocks, score_tile, vt_scr, o_ref):
    tq = ATT_TILE
    tiles = [(qi, kb) for qi in range(n_blocks) for kb in [qi] + list(range(qi))]
    cur = score_tile(*tiles[0])
    state = None
    for i, (qi, kb) in enumerate(tiles):
        nxt = score_tile(*tiles[i + 1]) if i + 1 < len(tiles) else None
        if kb == qi:
            state = (_init_state(tq), _init_state(tq))
        new_state = []
        for hh in range(2):
            vt = vt_scr[hh * VT_ROWS:(hh + 1) * VT_ROWS, kb * tq:(kb + 1) * tq]
            new_state.append(_online_update_t(cur[hh], vt, *state[hh]))
        state = tuple(new_state)
        if kb == max(qi - 1, 0):
            _store_heads(o_ref, qi * tq, state)
        cur = nxt


def _fox_kernel(q_ref, k_ref, v_ref, c_ref, o_ref, vt_scr):
    tq = ATT_TILE
    S = q_ref.shape[0]
    causal = _causal_mask_t(tq)
    lane = lax.broadcasted_iota(jnp.int32, (tq, LANES), 1)
    ones = [jnp.where((lane >= 3 * hh) & (lane < 3 * hh + 3), 1.0, 0.0).astype(BF16)
            for hh in range(2)]
    _transpose_values(v_ref, vt_scr)
    q_weights = {}

    def score_tile(qi, kb):
        if qi not in q_weights:
            qh = _split_heads(q_ref[qi * tq:(qi + 1) * tq, :])
            q_weights.clear()
            q_weights[qi] = [_to_weights(jnp.concatenate([qh[hh], ones[hh]], axis=1))
                             for hh in range(2)]
        rows = slice(kb * tq, (kb + 1) * tq)
        ka = jnp.concatenate([k_ref[rows, :], c_ref[rows, :]], axis=1)
        out = []
        for hh in range(2):
            s = jnp.dot(ka, q_weights[qi][hh], preferred_element_type=F32)
            out.append(jnp.where(causal, s, MASKED) if kb == qi else s)
        return tuple(out)

    _run_causal_tiles(S // tq, score_tile, vt_scr, o_ref)


def _fox_attention(qkv, c_aug, B, S, n_fox):
    T = B * S
    pairs = n_fox * HEAD_DIM // LANES
    kb = pairs
    vb = 2 * pairs
    return pl.pallas_call(
        _fox_kernel,
        grid=(B, pairs),
        in_specs=[pl.BlockSpec((S, LANES), lambda b, p: (b, p)),
                  pl.BlockSpec((S, LANES), lambda b, p: (b, kb + p)),
                  pl.BlockSpec((S, LANES), lambda b, p: (b, vb + p)),
                  pl.BlockSpec((S, LANES), lambda b, p: (b, p))],
        out_specs=pl.BlockSpec((S, LANES), lambda b, p: (b, p)),
        out_shape=jax.ShapeDtypeStruct((T, pairs * LANES), BF16),
        scratch_shapes=[pltpu.VMEM((2 * VT_ROWS, S), BF16)],
        compiler_params=_params("parallel", "parallel"),
        name="fox_attention",
    )(qkv, qkv, qkv, c_aug)


def _moba_kernel(q_ref, k_ref, v_ref, a_ref, bias_ref, o_ref, vt_scr):
    tq = ATT_TILE
    S = q_ref.shape[0]
    rows = -(-(S // tq) // 8) * 8
    causal = _causal_mask_t(tq)
    blk_row = lax.broadcasted_iota(jnp.int32, (rows, tq), 0)
    _transpose_values(v_ref, vt_scr)

    kmean = jnp.dot(a_ref[...], k_ref[...], preferred_element_type=F32)
    kmean_hi = kmean.astype(BF16)
    kmean_lo = (kmean - kmean_hi.astype(F32)).astype(BF16)
    per_block = {}

    def prepare(qi):
        qh = _split_heads(q_ref[qi * tq:(qi + 1) * tq, :])
        qw = [_to_weights(qh[hh]) for hh in range(2)]
        offsets = []
        for hh in range(2):
            gate = (jnp.dot(kmean_hi, qw[hh], preferred_element_type=F32)
                    + jnp.dot(kmean_lo, qw[hh], preferred_element_type=F32))[:rows, :]
            rank = jnp.zeros((rows, tq), jnp.int32)
            for mb in range(qi):
                g_m = gate[mb:mb + 1, :]
                tie = jnp.where(blk_row > mb, 1, 0)
                rank = rank + jnp.where(g_m > gate, 1, jnp.where(g_m == gate, tie, 0))
            offsets.append(jnp.where(rank < MOBA_TOPK, 0.0, MASKED))
        return qw, offsets

    def score_tile(qi, kb):
        if qi not in per_block:
            per_block.clear()
            per_block[qi] = prepare(qi)
        qw, offsets = per_block[qi]
        k = k_ref[kb * tq:(kb + 1) * tq, :]
        out = []
        for hh in range(2):
            s = jnp.dot(k, qw[hh], preferred_element_type=F32)
            if kb == qi:
                s = jnp.where(causal, s + bias_ref[hh, 0], MASKED)
            elif kb == qi - 1:
                s = s + bias_ref[hh, 1] + offsets[hh][kb:kb + 1, :]
            else:
                s = s + (bias_ref[hh, 2, 0:1, :] + offsets[hh][kb:kb + 1, :])
            out.append(s)
        return tuple(out)

    _run_causal_tiles(S // tq, score_tile, vt_scr, o_ref)


def _moba_attention(qkv, blk_avg, bias_tiles, B, S, n_fox, n_moba):
    T = B * S
    pairs = n_moba * HEAD_DIM // LANES
    qb = 3 * n_fox * HEAD_DIM // LANES
    kb = qb + pairs
    vb = qb + 2 * pairs
    tq = ATT_TILE
    return pl.pallas_call(
        _moba_kernel,
        grid=(B, pairs),
        in_specs=[pl.BlockSpec((S, LANES), lambda b, p: (b, qb + p)),
                  pl.BlockSpec((S, LANES), lambda b, p: (b, kb + p)),
                  pl.BlockSpec((S, LANES), lambda b, p: (b, vb + p)),
                  pl.BlockSpec((LANES, S), lambda b, p: (0, 0)),
                  pl.BlockSpec((2, 3, tq, tq), lambda b, p: (p, 0, 0, 0))],
        out_specs=pl.BlockSpec((S, LANES), lambda b, p: (b, p)),
        out_shape=jax.ShapeDtypeStruct((T, pairs * LANES), BF16),
        scratch_shapes=[pltpu.VMEM((2 * VT_ROWS, S), BF16)],
        compiler_params=_params("parallel", "parallel"),
        name="moba_attention",
    )(qkv, qkv, qkv, blk_avg, bias_tiles)


def _mix_out_kernel(of_ref, om_ref, x_ref, w_ref, g_ref, b_ref, y_ref, *, alpha):
    half = of_ref.shape[1]
    h = (jnp.dot(of_ref[...], w_ref[:half, :], preferred_element_type=F32)
         + jnp.dot(om_ref[...], w_ref[half:, :], preferred_element_type=F32))
    y_ref[...] = _layer_norm(alpha * x_ref[...] + h, g_ref[...], b_ref[...])


def _mix_out(o_f, o_m, x, w, g, b, alpha):
    T, D = x.shape
    half = o_f.shape[1]
    row = lambda i: (i, 0)
    fixed = lambda i: (0, 0)
    return pl.pallas_call(
        functools.partial(_mix_out_kernel, alpha=alpha),
        grid=(T // ROW_TILE,),
        in_specs=[pl.BlockSpec((ROW_TILE, half), row),
                  pl.BlockSpec((ROW_TILE, half), row),
                  pl.BlockSpec((ROW_TILE, D), row),
                  pl.BlockSpec((D, D), fixed),
                  pl.BlockSpec((1, D), fixed),
                  pl.BlockSpec((1, D), fixed)],
        out_specs=pl.BlockSpec((ROW_TILE, D), row),
        out_shape=jax.ShapeDtypeStruct((T, D), F32),
        compiler_params=_params("parallel"),
        name="mix_out_ln",
    )(o_f, o_m, x, w, g.reshape(1, D), b.reshape(1, D))


def _mem_proj_kernel(m_ref, w_ref, kv_ref):
    mb = m_ref[...].astype(BF16)
    n = kv_ref.shape[1]
    for n0 in range(0, n, 512):
        kv_ref[:, n0:n0 + 512] = jnp.dot(
            mb, w_ref[:, n0:n0 + 512], preferred_element_type=F32).astype(BF16)


def _mem_proj(mem2d, w_kv):
    M, D = mem2d.shape
    n = w_kv.shape[1]
    tm = min(ROW_TILE, M)
    return pl.pallas_call(
        _mem_proj_kernel,
        grid=(M // tm,),
        in_specs=[pl.BlockSpec((tm, D), lambda i: (i, 0)),
                  pl.BlockSpec((D, n), lambda i: (0, 0))],
        out_specs=pl.BlockSpec((tm, n), lambda i: (i, 0)),
        out_shape=jax.ShapeDtypeStruct((M, n), BF16),
        compiler_params=_params("parallel"),
        name="mem_proj",
    )(mem2d, w_kv)


def _cross_kernel(x_ref, wq_ref, kv_ref, wo_ref, g_ref, b_ref, wr_hi_ref, wr_lo_ref, br_ref, tri_ref,
                  y_ref, yb_ref, idx_ref, gate_ref, rank_ref, count_ref, *, alpha, n_experts):
    D = x_ref.shape[1]

    @pl.when(pl.program_id(0) == 0)
    def _():
        count_ref[...] = jnp.zeros(count_ref.shape, F32)

    dh = D // N_CROSS_HEADS
    x = x_ref[...]
    q = jnp.dot(x.astype(BF16), wq_ref[...], preferred_element_type=F32).astype(BF16)
    heads = []
    for h in range(N_CROSS_HEADS):
        k_h = kv_ref[:, h * dh:(h + 1) * dh]
        v_h = kv_ref[:, D + h * dh:D + (h + 1) * dh]
        s = _nt_dot(q[:, h * dh:(h + 1) * dh], k_h)
        p = jnp.exp(s - jnp.max(s, axis=1, keepdims=True))
        p = p * (1.0 / jnp.sum(p, axis=1, keepdims=True))
        heads.append(jnp.dot(p.astype(BF16), v_h, preferred_element_type=F32).astype(BF16))
    o = jnp.concatenate(heads, axis=1)
    hproj = jnp.dot(o, wo_ref[...], preferred_element_type=F32)
    y = _layer_norm(alpha * x + hproj, g_ref[...], b_ref[...])
    y_ref[...] = y
    y_hi = y.astype(BF16)
    yb_ref[...] = y_hi

    y_lo = (y - y_hi.astype(F32)).astype(BF16)
    logits = (jnp.dot(y_hi, wr_hi_ref[...], preferred_element_type=F32)
              + jnp.dot(y_lo, wr_hi_ref[...], preferred_element_type=F32)
              + jnp.dot(y_hi, wr_lo_ref[...], preferred_element_type=F32)
              + br_ref[...])
    lane = lax.broadcasted_iota(jnp.int32, logits.shape, 1)
    lane_f = lane.astype(F32)
    work = jnp.where(lane < n_experts, logits, -jnp.inf)
    idx_out = jnp.zeros(logits.shape, F32)
    val_out = jnp.zeros(logits.shape, F32)
    top = None
    args = []
    for kk in range(TOP_K):
        best = jnp.max(work, axis=1, keepdims=True)
        arg = jnp.min(jnp.where(work == best, lane_f, float(LANES)), axis=1, keepdims=True)
        work = jnp.where(lane_f == arg, -jnp.inf, work)
        if top is None:
            top = best
        args.append(arg)
        idx_out = jnp.where(lane == kk, arg, idx_out)
        val_out = jnp.where(lane == kk, jnp.exp(best - top), val_out)
    idx_ref[...] = idx_out.astype(jnp.int32)
    gate_ref[...] = val_out * (1.0 / jnp.sum(val_out, axis=1, keepdims=True))

    chosen = jnp.where(work == -jnp.inf, jnp.where(lane < n_experts, 1.0, 0.0), 0.0)
    before = (jnp.dot(tri_ref[...], chosen.astype(BF16), preferred_element_type=F32)
              + count_ref[0:1, :])
    rank_out = jnp.zeros(logits.shape, F32)
    for kk in range(TOP_K):
        r = jnp.sum(jnp.where(lane_f == args[kk], before, 0.0), axis=1, keepdims=True)
        rank_out = jnp.where(lane == kk, r, rank_out)
    rank_ref[...] = rank_out.astype(jnp.int32)
    count_ref[...] = count_ref[...] + jnp.sum(chosen, axis=0, keepdims=True)


def _cross_and_route(x, wq, kv, wo, g, b, wr_hi, wr_lo, br, alpha, S, n_mem, n_experts):
    T, D = x.shape
    tiles_per_batch = S // ROW_TILE
    row = lambda i: (i, 0)
    fixed = lambda i: (0, 0)
    strict_lower = jnp.asarray(np.tril(np.ones((ROW_TILE, ROW_TILE), np.float32), -1), BF16)
    return pl.pallas_call(
        functools.partial(_cross_kernel, alpha=alpha, n_experts=n_experts),
        grid=(T // ROW_TILE,),
        in_specs=[pl.BlockSpec((ROW_TILE, D), row),
                  pl.BlockSpec((D, D), fixed),
                  pl.BlockSpec((n_mem, 2 * D), lambda i: (i // tiles_per_batch, 0)),
                  pl.BlockSpec((D, D), fixed),
                  pl.BlockSpec((1, D), fixed),
                  pl.BlockSpec((1, D), fixed),
                  pl.BlockSpec((D, LANES), fixed),
                  pl.BlockSpec((D, LANES), fixed),
                  pl.BlockSpec((1, LANES), fixed),
                  pl.BlockSpec((ROW_TILE, ROW_TILE), fixed)],
        out_specs=[pl.BlockSpec((ROW_TILE, D), row),
                   pl.BlockSpec((ROW_TILE, D), row),
                   pl.BlockSpec((ROW_TILE, LANES), row),
                   pl.BlockSpec((ROW_TILE, LANES), row),
                   pl.BlockSpec((ROW_TILE, LANES), row),
                   pl.BlockSpec((8, LANES), fixed)],
        out_shape=[jax.ShapeDtypeStruct((T, D), F32),
                   jax.ShapeDtypeStruct((T, D), BF16),
                   jax.ShapeDtypeStruct((T, LANES), jnp.int32),
                   jax.ShapeDtypeStruct((T, LANES), F32),
                   jax.ShapeDtypeStruct((T, LANES), jnp.int32),
                   jax.ShapeDtypeStruct((8, LANES), F32)],
        compiler_params=_params("arbitrary"),
        name="cross_attn_router",
    )(x, wq, kv, wo, g.reshape(1, D), b.reshape(1, D), wr_hi, wr_lo, br, strict_lower)


def _moe_kernel(be_ref, nv_ref, x_ref, wgu_ref, bgu_ref, wd_ref, bd_ref, y_ref, wgu_bf, wd_bf):
    i = pl.program_id(0)
    D, F = wgu_ref.shape[1], wd_ref.shape[1]
    valid = i < nv_ref[0]

    @pl.when(valid & ((i == 0) | (be_ref[i] != be_ref[jnp.maximum(i - 1, 0)])))
    def _():
        def cast_rows(ref, out, r, carry):
            r0 = pl.multiple_of(r * LANES, LANES)
            out[pl.ds(r0, LANES), :] = ref[0, pl.ds(r0, LANES), :].astype(BF16)
            return carry
        lax.fori_loop(0, D // LANES, functools.partial(cast_rows, wgu_ref, wgu_bf), 0)
        lax.fori_loop(0, F // LANES, functools.partial(cast_rows, wd_ref, wd_bf), 0)

    @pl.when(valid)
    def _():
        x = x_ref[...]
        acc = None
        for f0 in range(0, F, MOE_FCHUNK):
            f1 = f0 + MOE_FCHUNK
            g = jnp.dot(x, wgu_bf[:, f0:f1], preferred_element_type=F32) + bgu_ref[0, :, f0:f1]
            u = (jnp.dot(x, wgu_bf[:, F + f0:F + f1], preferred_element_type=F32)
                 + bgu_ref[0, :, F + f0:F + f1])
            g = jnp.minimum(g, SWIGLU_LIMIT)
            u = jnp.clip(u, -SWIGLU_LIMIT, SWIGLU_LIMIT)
            glu = g * jax.nn.sigmoid(g * SWIGLU_ALPHA)
            act = ((u + 1.0) * glu).astype(BF16)
            part = jnp.dot(act, wd_bf[f0:f1, :], preferred_element_type=F32)
            acc = part if acc is None else acc + part
        y_ref[...] = (acc + bd_ref[0]).astype(y_ref.dtype)

    @pl.when(jnp.logical_not(valid))
    def _():
        y_ref[...] = jnp.zeros(y_ref.shape, y_ref.dtype)


def _moe_experts(block_expert, n_valid, xs, wgu, bgu, wd, bd):
    n_rows, D = xs.shape
    E, _, F2 = wgu.shape
    F = F2 // 2
    n_blocks = n_rows // MOE_TILE

    def row_in(i, be, nv):
        return (jnp.minimum(i, nv[0] - 1), 0)

    def expert3(i, be, nv):
        return (be[i], 0, 0)

    grid_spec = pltpu.PrefetchScalarGridSpec(
        num_scalar_prefetch=2,
        grid=(n_blocks,),
        in_specs=[pl.BlockSpec((MOE_TILE, D), row_in),
                  pl.BlockSpec((1, D, F2), expert3),
                  pl.BlockSpec((1, 1, F2), expert3),
                  pl.BlockSpec((1, F, D), expert3),
                  pl.BlockSpec((1, 1, D), expert3)],
        out_specs=pl.BlockSpec((MOE_TILE, D), lambda i, be, nv: (i, 0)),
        scratch_shapes=[pltpu.VMEM((D, F2), BF16), pltpu.VMEM((F, D), BF16)],
    )
    return pl.pallas_call(
        _moe_kernel,
        grid_spec=grid_spec,
        out_shape=jax.ShapeDtypeStruct((n_rows, D), BF16),
        compiler_params=_params("arbitrary"),
        name="moe_experts",
    )(block_expert, n_valid, xs, wgu, bgu.reshape(E, 1, F2), wd, bd.reshape(E, 1, D))


def _combine_kernel(x_ref, ys_ref, gate_ref, g_ref, b_ref, y_ref, *, alpha):
    gates = gate_ref[...]
    y = alpha * x_ref[...]
    for kk in range(TOP_K):
        y = y + ys_ref[kk].astype(F32) * gates[:, kk:kk + 1]
    y_ref[...] = _layer_norm(y, g_ref[...], b_ref[...])


def _combine_ln(x, ys_by_k, gates, g, b, alpha):
    T, D = x.shape
    tm = ROW_TILE
    row = lambda i: (i, 0)
    fixed = lambda i: (0, 0)
    return pl.pallas_call(
        functools.partial(_combine_kernel, alpha=alpha),
        grid=(T // tm,),
        in_specs=[pl.BlockSpec((tm, D), row),
                  pl.BlockSpec((TOP_K, tm, D), lambda i: (0, i, 0)),
                  pl.BlockSpec((tm, LANES), row),
                  pl.BlockSpec((1, D), fixed),
                  pl.BlockSpec((1, D), fixed)],
        out_specs=pl.BlockSpec((tm, D), row),
        out_shape=jax.ShapeDtypeStruct((T, D), F32),
        compiler_params=_params("parallel"),
        name="moe_combine_ln",
    )(x, ys_by_k, gates, g.reshape(1, D), b.reshape(1, D))


def _t5_bucket(dist):
    max_exact = N_BUCKETS // 2
    n = jnp.maximum(dist, 0)
    nf = jnp.maximum(n, 1).astype(F32)
    large = max_exact + (jnp.log(nf / max_exact) / math.log(MAX_DISTANCE / max_exact)
                         * (N_BUCKETS - max_exact)).astype(jnp.int32)
    large = jnp.minimum(large, N_BUCKETS - 1)
    return jnp.where(n < max_exact, n, large)


def _moba_bias_tables(rel_bias, S):
    n = ATT_TILE
    n_heads = rel_bias.shape[1]
    by_dist = rel_bias.T.astype(F32)[:, _t5_bucket(jnp.arange(2 * n + 1))] * LOG2E

    def toeplitz(v):
        return jnp.tile(v, (1, n))[:, :n * (2 * n - 1)].reshape(n_heads, n, 2 * n - 1)[:, :, :n]

    own = toeplitz(by_dist[:, :2 * n])
    prev = toeplitz(jnp.roll(by_dist[:, :2 * n], -n, axis=1))
    far = jnp.broadcast_to(by_dist[:, 2 * n][:, None, None], (n_heads, n, n))
    return jnp.stack([own, prev, far], axis=1)


def _block_average_matrix(S):
    a = np.zeros((LANES, S), np.float32)
    for n in range(S // MOBA_BLOCK):
        a[n, n * MOBA_BLOCK:(n + 1) * MOBA_BLOCK] = 1.0 / MOBA_BLOCK
    return jnp.asarray(a, BF16)


def _dispatch(top_idx, rank, counts):
    T = top_idx.shape[0]
    TK = T * TOP_K
    n_experts = counts.shape[0]
    padded = ((counts + MOE_TILE - 1) // MOE_TILE) * MOE_TILE
    start = jnp.cumsum(counts) - counts
    pend = jnp.cumsum(padded)
    pstart = pend - padded
    n_blocks = TK // MOE_TILE + n_experts
    n_valid = (pend[-1] // MOE_TILE).astype(jnp.int32)
    hot = top_idx.T[:, :, None] == jnp.arange(n_experts, dtype=jnp.int32)
    dest = jnp.sum(jnp.where(hot, pstart, 0), axis=-1) + rank.T
    blk = jnp.minimum(jnp.arange(n_blocks, dtype=jnp.int32), n_valid - 1)
    block_expert = jnp.minimum(
        jnp.sum((pend[None, :] <= (blk * MOE_TILE)[:, None]).astype(jnp.int32), axis=1),
        n_experts - 1)
    order = jnp.argsort(top_idx.reshape(-1)).astype(jnp.int32)
    shift = (start - pstart)[block_expert]
    slot = jnp.arange(n_blocks * MOE_TILE, dtype=jnp.int32).reshape(n_blocks, MOE_TILE)
    src = jnp.clip(slot + shift[:, None], 0, TK - 1).reshape(-1)
    slot_token = order[src] // TOP_K
    return slot_token, dest.reshape(-1), block_expert, n_valid.reshape(1)


def kernel(x, mem, w_in, b_forget, w_mix_out, rel_bias, ln1_g, ln1_b, w_cq, w_ck, w_cv, w_co,
           ln2_g, ln2_b, w_router, b_router, w_gate_up, b_gate_up, w_down, b_down, ln3_g, ln3_b):
    B, S, D = x.shape
    depth = w_in.shape[0]
    n_mem = mem.shape[1]
    n_experts = w_router.shape[2]
    n_heads = D // HEAD_DIM
    n_fox = n_heads // 2
    n_moba = n_heads - n_fox
    fox_w, moba_w = n_fox * HEAD_DIM, n_moba * HEAD_DIM
    T = B * S
    alpha = (2 * depth) ** 0.25
    assert S % ROW_TILE == 0 and S % ATT_TILE == 0 and D % LANES == 0
    assert ATT_TILE == MOBA_BLOCK and MOBA_BLOCK >= MAX_DISTANCE
    assert n_experts <= LANES and (T * TOP_K) % MOE_TILE == 0

    scale = HEAD_DIM ** -0.5 * LOG2E
    c0 = 3 * fox_w
    c1 = c0 + n_fox
    w_fq = w_in[:, :, :fox_w] * scale
    w_fkv = w_in[:, :, fox_w:c0]
    w_fg = jnp.pad(w_in[:, :, c0:c1], ((0, 0), (0, 0), (0, LANES - n_fox)))
    w_mq = w_in[:, :, c1:c1 + moba_w] * scale
    w_mkv = w_in[:, :, c1 + moba_w:]
    w_in_b = jnp.concatenate([w_fq, w_fkv, w_mq, w_mkv, w_fg], axis=2).astype(BF16)

    w_out_b = w_mix_out.astype(BF16)
    w_cq_b = (w_cq * (D // N_CROSS_HEADS) ** -0.5).astype(BF16)
    w_ckv_b = jnp.concatenate([w_ck, w_cv], axis=2).astype(BF16)
    w_co_b = w_co.astype(BF16)
    w_r = jnp.pad(w_router, ((0, 0), (0, 0), (0, LANES - n_experts)))
    w_r_hi = w_r.astype(BF16)
    w_r_lo = (w_r - w_r_hi.astype(F32)).astype(BF16)
    b_r = jnp.pad(b_router, ((0, 0), (0, LANES - n_experts))).reshape(depth, 1, LANES)

    bias_tiles = _moba_bias_tables(rel_bias, S)
    blk_avg = _block_average_matrix(S)
    mem2d = mem.reshape(B * n_mem, D)
    xt = x.reshape(T, D)

    for l in range(depth):
        qkv, f_logit = _in_proj(xt, w_in_b[l])
        c_aug = _fox_decay(f_logit, b_forget[l], S, n_fox)
        o_f = _fox_attention(qkv, c_aug, B, S, n_fox)
        o_m = _moba_attention(qkv, blk_avg, bias_tiles, B, S, n_fox, n_moba)
        xt = _mix_out(o_f, o_m, xt, w_out_b[l], ln1_g[l], ln1_b[l], alpha)

        kv = _mem_proj(mem2d, w_ckv_b[l])
        xt, xt_b, top_idx, gates, rank, counts = _cross_and_route(
            xt, w_cq_b[l], kv, w_co_b[l], ln2_g[l], ln2_b[l], w_r_hi[l], w_r_lo[l], b_r[l],
            alpha, S, n_mem, n_experts)

        slot_token, dest, block_expert, n_valid = _dispatch(
            top_idx[:, :TOP_K], rank[:, :TOP_K], counts[0, :n_experts].astype(jnp.int32))
        xs = xt_b[slot_token]
        ys = _moe_experts(block_expert, n_valid, xs, w_gate_up[l], b_gate_up[l], w_down[l],
                          b_down[l])
        ys_by_k = ys[dest].reshape(TOP_K, T, D)
        xt = _combine_ln(xt, ys_by_k, gates, ln3_g[l], ln3_b[l], alpha)

    return xt.reshape(B, S, D)
```

```python
import functools
import math

import jax
import jax.numpy as jnp
import numpy as np
from jax import lax
from jax.experimental import pallas as pl
from jax.experimental.pallas import tpu as pltpu

F32 = jnp.float32
BF16 = jnp.bfloat16

HEAD_DIM = 64
N_BUCKETS = 32
MAX_DISTANCE = 128
MOBA_BLOCK = 256
MOBA_TOPK = 3
N_CROSS_HEADS = 4
TOP_K = 4
SWIGLU_LIMIT = 7.0
SWIGLU_ALPHA = 1.702
LN_EPS = 1e-5

LANES = 128
VMEM_LIMIT = 56 * 1024 * 1024

ROW_TILE = 512
ATT_TILE = MOBA_BLOCK
MOE_TILE = 512
MOE_FCHUNK = 512
MASKED = -1e30
LOG2E = math.log2(math.e)


def _params(*sem):
    return pltpu.CompilerParams(dimension_semantics=sem, vmem_limit_bytes=VMEM_LIMIT)


def _nt_dot(a, b):
    return lax.dot_general(a, b, (((1,), (1,)), ((), ())), preferred_element_type=F32)


def _layer_norm(y, g, b):
    mu = jnp.mean(y, axis=-1, keepdims=True)
    d = y - mu
    var = jnp.mean(d * d, axis=-1, keepdims=True)
    return d * lax.rsqrt(var + LN_EPS) * g + b


def _in_proj_kernel(x_ref, w_ref, qkv_ref, f_ref):
    xb = x_ref[...].astype(BF16)
    n_qkv = qkv_ref.shape[1]
    for n0 in range(0, n_qkv, 512):
        qkv_ref[:, n0:n0 + 512] = jnp.dot(
            xb, w_ref[:, n0:n0 + 512], preferred_element_type=F32).astype(BF16)
    f_ref[...] = jnp.dot(xb, w_ref[:, n_qkv:], preferred_element_type=F32)


def _in_proj(x, w):
    T, D = x.shape
    n_all = w.shape[1]
    n_qkv = n_all - LANES
    return pl.pallas_call(
        _in_proj_kernel,
        grid=(T // ROW_TILE,),
        in_specs=[pl.BlockSpec((ROW_TILE, D), lambda i: (i, 0)),
                  pl.BlockSpec((D, n_all), lambda i: (0, 0))],
        out_specs=[pl.BlockSpec((ROW_TILE, n_qkv), lambda i: (i, 0)),
                   pl.BlockSpec((ROW_TILE, LANES), lambda i: (i, 0))],
        out_shape=[jax.ShapeDtypeStruct((T, n_qkv), BF16),
                   jax.ShapeDtypeStruct((T, LANES), F32)],
        compiler_params=_params("parallel"),
        name="in_proj",
    )(x, w)


def _split3(x):
    p1 = x.astype(BF16)
    r1 = x - p1.astype(F32)
    p2 = r1.astype(BF16)
    p3 = (r1 - p2.astype(F32)).astype(BF16)
    return p1, p2, p3


def _decay_kernel(f_ref, b_ref, tri_ref, place_ref, c_ref):
    S = f_ref.shape[0]
    blk = tri_ref.shape[0]
    carry = jnp.zeros((1, LANES), F32)
    for j in range(S // blk):
        z = f_ref[j * blk:(j + 1) * blk, :] + b_ref[...]
        ls = jnp.minimum(z, 0.0) - jnp.log1p(jnp.exp(-jnp.abs(z)))
        c = carry
        for piece in _split3(ls):
            c = c + jnp.dot(tri_ref[...], piece, preferred_element_type=F32)
        carry = c[blk - 1:blk, :]
        aug = None
        for i, piece in enumerate(_split3(c * LOG2E)):
            t = jnp.dot(piece, place_ref[i], preferred_element_type=F32)
            aug = t if aug is None else aug + t
        c_ref[j * blk:(j + 1) * blk, :] = aug.astype(BF16)


def _fox_decay(f_logit, b_forget, S, n_fox):
    T = f_logit.shape[0]
    pairs = n_fox * HEAD_DIM // LANES
    blk = ATT_TILE
    tri = jnp.asarray(np.tril(np.ones((blk, blk), np.float32)), BF16)
    place = np.zeros((3, LANES, pairs * LANES), np.float32)
    for h in range(n_fox):
        for i in range(3):
            place[i, h, (h // 2) * LANES + 3 * (h % 2) + i] = -1.0
    b_pad = jnp.pad(b_forget, (0, LANES - n_fox)).reshape(1, LANES)
    return pl.pallas_call(
        _decay_kernel,
        grid=(T // S,),
        in_specs=[pl.BlockSpec((S, LANES), lambda b: (b, 0)),
                  pl.BlockSpec((1, LANES), lambda b: (0, 0)),
                  pl.BlockSpec((blk, blk), lambda b: (0, 0)),
                  pl.BlockSpec((3, LANES, pairs * LANES), lambda b: (0, 0, 0))],
        out_specs=pl.BlockSpec((S, pairs * LANES), lambda b: (b, 0)),
        out_shape=jax.ShapeDtypeStruct((T, pairs * LANES), BF16),
        compiler_params=_params("parallel"),
        name="fox_decay",
    )(f_logit, b_pad, tri, jnp.asarray(place, BF16))


VT_ROWS = HEAD_DIM + 16


def _online_update_t(s, vt, m, acc):
    m_new = jnp.maximum(m, jnp.max(s, axis=0, keepdims=True))
    p = jnp.exp2(s - m_new)
    acc_new = jnp.exp2(m - m_new) * acc + jnp.dot(vt, p.astype(BF16), preferred_element_type=F32)
    return m_new, acc_new


def _init_state(tq):
    return jnp.full((1, tq), MASKED, F32), jnp.zeros((VT_ROWS, tq), F32)


def _split_heads(q):
    lane = lax.broadcasted_iota(jnp.int32, q.shape, 1)
    zero = jnp.zeros((), q.dtype)
    return jnp.where(lane < HEAD_DIM, q, zero), jnp.where(lane < HEAD_DIM, zero, q)


def _causal_mask_t(tq):
    key = lax.broadcasted_iota(jnp.int32, (tq, tq), 0)
    qry = lax.broadcasted_iota(jnp.int32, (tq, tq), 1)
    return key <= qry


def _transpose_values(v_ref, vt_scr):
    tq = ATT_TILE
    S = v_ref.shape[0]
    for hh in range(2):
        vt_scr[hh * VT_ROWS + HEAD_DIM:(hh + 1) * VT_ROWS, :] = jnp.ones(
            (VT_ROWS - HEAD_DIM, S), BF16)
    for j in range(S // tq):
        vt = v_ref[j * tq:(j + 1) * tq, :].astype(F32).T.astype(BF16)
        for hh in range(2):
            vt_scr[hh * VT_ROWS:hh * VT_ROWS + HEAD_DIM, j * tq:(j + 1) * tq] = (
                vt[hh * HEAD_DIM:(hh + 1) * HEAD_DIM, :])


def _store_heads(o_ref, q0, state):
    tq = ATT_TILE
    o_t = jnp.concatenate(
        [acc[:HEAD_DIM] * (1.0 / acc[HEAD_DIM:HEAD_DIM + 1]) for _, acc in state], axis=0)
    o_ref[pl.ds(q0, tq), :] = o_t.T.astype(o_ref.dtype)


def _to_weights(x):
    return x.astype(F32).T.astype(BF16)


def _run_causal_tiles(n_blocks, score_tile, vt_scr, o_ref):
    tq = ATT_TILE
    tiles = [(qi, kb) for qi in range(n_blocks) for kb in [qi] + list(range(qi))]
    cur = score_tile(*tiles[0])
    state = None
    for i, (qi, kb) in enumerate(tiles):
        nxt = score_tile(*tiles[i + 1]) if i + 1 < len(tiles) else None
        if kb == qi:
            state = (_init_state(tq), _init_state(tq))
        new_state = []
        for hh in range(2):
            vt = vt_scr[hh * VT_ROWS:(hh + 1) * VT_ROWS, kb * tq:(kb + 1) * tq]
            new_state.append(_online_update_t(cur[hh], vt, *state[hh]))
        state = tuple(new_state)
        if kb == max(qi - 1, 0):
            _store_heads(o_ref, qi * tq, state)
        cur = nxt


def _fox_kernel(q_ref, k_ref, v_ref, c_ref, o_ref, vt_scr):
    tq = ATT_TILE
    S = q_ref.shape[0]
    causal = _causal_mask_t(tq)
    lane = lax.broadcasted_iota(jnp.int32, (tq, LANES), 1)
    ones = [jnp.where((lane >= 3 * hh) & (lane < 3 * hh + 3), 1.0, 0.0).astype(BF16)
            for hh in range(2)]
    _transpose_values(v_ref, vt_scr)
    q_weights = {}

    def score_tile(qi, kb):
        if qi not in q_weights:
            qh = _split_heads(q_ref[qi * tq:(qi + 1) * tq, :])
            q_weights.clear()
            q_weights[qi] = [_to_weights(jnp.concatenate([qh[hh], ones[hh]], axis=1))
                             for hh in range(2)]
        rows = slice(kb * tq, (kb + 1) * tq)
        ka = jnp.concatenate([k_ref[rows, :], c_ref[rows, :]], axis=1)
        out = []
        for hh in range(2):
            s = jnp.dot(ka, q_weights[qi][hh], preferred_element_type=F32)
            out.append(jnp.where(causal, s, MASKED) if kb == qi else s)
        return tuple(out)

    _run_causal_tiles(S // tq, score_tile, vt_scr, o_ref)


def _fox_attention(qkv, c_aug, B, S, n_fox):
    T = B * S
    pairs = n_fox * HEAD_DIM // LANES
    kb = pairs
    vb = 2 * pairs
    return pl.pallas_call(
        _fox_kernel,
        grid=(B, pairs),
        in_specs=[pl.BlockSpec((S, LANES), lambda b, p: (b, p)),
                  pl.BlockSpec((S, LANES), lambda b, p: (b, kb + p)),
                  pl.BlockSpec((S, LANES), lambda b, p: (b, vb + p)),
                  pl.BlockSpec((S, LANES), lambda b, p: (b, p))],
        out_specs=pl.BlockSpec((S, LANES), lambda b, p: (b, p)),
        out_shape=jax.ShapeDtypeStruct((T, pairs * LANES), BF16),
        scratch_shapes=[pltpu.VMEM((2 * VT_ROWS, S), BF16)],
        compiler_params=_params("parallel", "parallel"),
        name="fox_attention",
    )(qkv, qkv, qkv, c_aug)


def _moba_kernel(q_ref, k_ref, v_ref, a_ref, bias_ref, o_ref, vt_scr):
    tq = ATT_TILE
    S = q_ref.shape[0]
    rows = -(-(S // tq) // 8) * 8
    causal = _causal_mask_t(tq)
    blk_row = lax.broadcasted_iota(jnp.int32, (rows, tq), 0)
    _transpose_values(v_ref, vt_scr)

    kmean = jnp.dot(a_ref[...], k_ref[...], preferred_element_type=F32)
    kmean_hi = kmean.astype(BF16)
    kmean_lo = (kmean - kmean_hi.astype(F32)).astype(BF16)
    per_block = {}

    def prepare(qi):
        qh = _split_heads(q_ref[qi * tq:(qi + 1) * tq, :])
        qw = [_to_weights(qh[hh]) for hh in range(2)]
        offsets = []
        for hh in range(2):
            gate = (jnp.dot(kmean_hi, qw[hh], preferred_element_type=F32)
                    + jnp.dot(kmean_lo, qw[hh], preferred_element_type=F32))[:rows, :]
            rank = jnp.zeros((rows, tq), jnp.int32)
            for mb in range(qi):
                g_m = gate[mb:mb + 1, :]
                tie = jnp.where(blk_row > mb, 1, 0)
                rank = rank + jnp.where(g_m > gate, 1, jnp.where(g_m == gate, tie, 0))
            offsets.append(jnp.where(rank < MOBA_TOPK, 0.0, MASKED))
        return qw, offsets

    def score_tile(qi, kb):
        if qi not in per_block:
            per_block.clear()
            per_block[qi] = prepare(qi)
        qw, offsets = per_block[qi]
        k = k_ref[kb * tq:(kb + 1) * tq, :]
        out = []
        for hh in range(2):
            s = jnp.dot(k, qw[hh], preferred_element_type=F32)
            if kb == qi:
                s = jnp.where(causal, s + bias_ref[hh, 0], MASKED)
            elif kb == qi - 1:
                s = s + bias_ref[hh, 1] + offsets[hh][kb:kb + 1, :]
            else:
                s = s + (bias_ref[hh, 2, 0:1, :] + offsets[hh][kb:kb + 1, :])
            out.append(s)
        return tuple(out)

    _run_causal_tiles(S // tq, score_tile, vt_scr, o_ref)


def _moba_attention(qkv, blk_avg, bias_tiles, B, S, n_fox, n_moba):
    T = B * S
    pairs = n_moba * HEAD_DIM // LANES
    qb = 3 * n_fox * HEAD_DIM // LANES
    kb = qb + pairs
    vb = qb + 2 * pairs
    tq = ATT_TILE
    return pl.pallas_call(
        _moba_kernel,
        grid=(B, pairs),
        in_specs=[pl.BlockSpec((S, LANES), lambda b, p: (b, qb + p)),
                  pl.BlockSpec((S, LANES), lambda b, p: (b, kb + p)),
                  pl.BlockSpec((S, LANES), lambda b, p: (b, vb + p)),
                  pl.BlockSpec((LANES, S), lambda b, p: (0, 0)),
                  pl.BlockSpec((2, 3, tq, tq), lambda b, p: (p, 0, 0, 0))],
        out_specs=pl.BlockSpec((S, LANES), lambda b, p: (b, p)),
        out_shape=jax.ShapeDtypeStruct((T, pairs * LANES), BF16),
        scratch_shapes=[pltpu.VMEM((2 * VT_ROWS, S), BF16)],
        compiler_params=_params("parallel", "parallel"),
        name="moba_attention",
    )(qkv, qkv, qkv, blk_avg, bias_tiles)


def _mix_out_kernel(of_ref, om_ref, x_ref, w_ref, g_ref, b_ref, y_ref, *, alpha):
    half = of_ref.shape[1]
    h = (jnp.dot(of_ref[...], w_ref[:half, :], preferred_element_type=F32)
         + jnp.dot(om_ref[...], w_ref[half:, :], preferred_element_type=F32))
    y_ref[...] = _layer_norm(alpha * x_ref[...] + h, g_ref[...], b_ref[...])


def _mix_out(o_f, o_m, x, w, g, b, alpha):
    T, D = x.shape
    half = o_f.shape[1]
    row = lambda i: (i, 0)
    fixed = lambda i: (0, 0)
    return pl.pallas_call(
        functools.partial(_mix_out_kernel, alpha=alpha),
        grid=(T // ROW_TILE,),
        in_specs=[pl.BlockSpec((ROW_TILE, half), row),
                  pl.BlockSpec((ROW_TILE, half), row),
                  pl.BlockSpec((ROW_TILE, D), row),
                  pl.BlockSpec((D, D), fixed),
                  pl.BlockSpec((1, D), fixed),
                  pl.BlockSpec((1, D), fixed)],
        out_specs=pl.BlockSpec((ROW_TILE, D), row),
        out_shape=jax.ShapeDtypeStruct((T, D), F32),
        compiler_params=_params("parallel"),
        name="mix_out_ln",
    )(o_f, o_m, x, w, g.reshape(1, D), b.reshape(1, D))


def _mem_proj_kernel(m_ref, w_ref, kv_ref):
    mb = m_ref[...].astype(BF16)
    n = kv_ref.shape[1]
    for n0 in range(0, n, 512):
        kv_ref[:, n0:n0 + 512] = jnp.dot(
            mb, w_ref[:, n0:n0 + 512], preferred_element_type=F32).astype(BF16)


def _mem_proj(mem2d, w_kv):
    M, D = mem2d.shape
    n = w_kv.shape[1]
    tm = min(ROW_TILE, M)
    return pl.pallas_call(
        _mem_proj_kernel,
        grid=(M // tm,),
        in_specs=[pl.BlockSpec((tm, D), lambda i: (i, 0)),
                  pl.BlockSpec((D, n), lambda i: (0, 0))],
        out_specs=pl.BlockSpec((tm, n), lambda i: (i, 0)),
        out_shape=jax.ShapeDtypeStruct((M, n), BF16),
        compiler_params=_params("parallel"),
        name="mem_proj",
    )(mem2d, w_kv)


def _cross_kernel(x_ref, wq_ref, kv_ref, wo_ref, g_ref, b_ref, wr_hi_ref, wr_lo_ref, br_ref, tri_ref,
                  y_ref, yb_ref, idx_ref, gate_ref, rank_ref, count_ref, *, alpha, n_experts):
    D = x_ref.shape[1]

    @pl.when(pl.program_id(0) == 0)
    def _():
        count_ref[...] = jnp.zeros(count_ref.shape, F32)

    dh = D // N_CROSS_HEADS
    x = x_ref[...]
    q = jnp.dot(x.astype(BF16), wq_ref[...], preferred_element_type=F32).astype(BF16)
    heads = []
    for h in range(N_CROSS_HEADS):
        k_h = kv_ref[:, h * dh:(h + 1) * dh]
        v_h = kv_ref[:, D + h * dh:D + (h + 1) * dh]
        s = _nt_dot(q[:, h * dh:(h + 1) * dh], k_h)
        p = jnp.exp(s - jnp.max(s, axis=1, keepdims=True))
        p = p * (1.0 / jnp.sum(p, axis=1, keepdims=True))
        heads.append(jnp.dot(p.astype(BF16), v_h, preferred_element_type=F32).astype(BF16))
    o = jnp.concatenate(heads, axis=1)
    hproj = jnp.dot(o, wo_ref[...], preferred_element_type=F32)
    y = _layer_norm(alpha * x + hproj, g_ref[...], b_ref[...])
    y_ref[...] = y
    y_hi = y.astype(BF16)
    yb_ref[...] = y_hi

    y_lo = (y - y_hi.astype(F32)).astype(BF16)
    logits = (jnp.dot(y_hi, wr_hi_ref[...], preferred_element_type=F32)
              + jnp.dot(y_lo, wr_hi_ref[...], preferred_element_type=F32)
              + jnp.dot(y_hi, wr_lo_ref[...], preferred_element_type=F32)
              + br_ref[...])
    lane = lax.broadcasted_iota(jnp.int32, logits.shape, 1)
    lane_f = lane.astype(F32)
    work = jnp.where(lane < n_experts, logits, -jnp.inf)
    idx_out = jnp.zeros(logits.shape, F32)
    val_out = jnp.zeros(logits.shape, F32)
    top = None
    args = []
    for kk in range(TOP_K):
        best = jnp.max(work, axis=1, keepdims=True)
        arg = jnp.min(jnp.where(work == best, lane_f, float(LANES)), axis=1, keepdims=True)
        work = jnp.where(lane_f == arg, -jnp.inf, work)
        if top is None:
            top = best
        args.append(arg)
        idx_out = jnp.where(lane == kk, arg, idx_out)
        val_out = jnp.where(lane == kk, jnp.exp(best - top), val_out)
    idx_ref[...] = idx_out.astype(jnp.int32)
    gate_ref[...] = val_out * (1.0 / jnp.sum(val_out, axis=1, keepdims=True))

    chosen = jnp.where(work == -jnp.inf, jnp.where(lane < n_experts, 1.0, 0.0), 0.0)
    before = (jnp.dot(tri_ref[...], chosen.astype(BF16), preferred_element_type=F32)
              + count_ref[0:1, :])
    rank_out = jnp.zeros(logits.shape, F32)
    for kk in range(TOP_K):
        r = jnp.sum(jnp.where(lane_f == args[kk], before, 0.0), axis=1, keepdims=True)
        rank_out = jnp.where(lane == kk, r, rank_out)
    rank_ref[...] = rank_out.astype(jnp.int32)
    count_ref[...] = count_ref[...] + jnp.sum(chosen, axis=0, keepdims=True)


def _cross_and_route(x, wq, kv, wo, g, b, wr_hi, wr_lo, br, alpha, S, n_mem, n_experts):
    T, D = x.shape
    tiles_per_batch = S // ROW_TILE
    row = lambda i: (i, 0)
    fixed = lambda i: (0, 0)
    strict_lower = jnp.asarray(np.tril(np.ones((ROW_TILE, ROW_TILE), np.float32), -1), BF16)
    return pl.pallas_call(
        functools.partial(_cross_kernel, alpha=alpha, n_experts=n_experts),
        grid=(T // ROW_TILE,),
        in_specs=[pl.BlockSpec((ROW_TILE, D), row),
                  pl.BlockSpec((D, D), fixed),
                  pl.BlockSpec((n_mem, 2 * D), lambda i: (i // tiles_per_batch, 0)),
                  pl.BlockSpec((D, D), fixed),
                  pl.BlockSpec((1, D), fixed),
                  pl.BlockSpec((1, D), fixed),
                  pl.BlockSpec((D, LANES), fixed),
                  pl.BlockSpec((D, LANES), fixed),
                  pl.BlockSpec((1, LANES), fixed),
                  pl.BlockSpec((ROW_TILE, ROW_TILE), fixed)],
        out_specs=[pl.BlockSpec((ROW_TILE, D), row),
                   pl.BlockSpec((ROW_TILE, D), row),
                   pl.BlockSpec((ROW_TILE, LANES), row),
                   pl.BlockSpec((ROW_TILE, LANES), row),
                   pl.BlockSpec((ROW_TILE, LANES), row),
                   pl.BlockSpec((8, LANES), fixed)],
        out_shape=[jax.ShapeDtypeStruct((T, D), F32),
                   jax.ShapeDtypeStruct((T, D), BF16),
                   jax.ShapeDtypeStruct((T, LANES), jnp.int32),
                   jax.ShapeDtypeStruct((T, LANES), F32),
                   jax.ShapeDtypeStruct((T, LANES), jnp.int32),
                   jax.ShapeDtypeStruct((8, LANES), F32)],
        compiler_params=_params("arbitrary"),
        name="cross_attn_router",
    )(x, wq, kv, wo, g.reshape(1, D), b.reshape(1, D), wr_hi, wr_lo, br, strict_lower)


def _moe_kernel(be_ref, nv_ref, x_ref, wgu_ref, bgu_ref, wd_ref, bd_ref, y_ref, wgu_bf, wd_bf):
    i = pl.program_id(0)
    D, F = wgu_ref.shape[0], wd_ref.shape[0]
    valid = i < nv_ref[0]

    @pl.when(valid & ((i == 0) | (be_ref[i] != be_ref[jnp.maximum(i - 1, 0)])))
    def _():
        def cast_rows(ref, out, r, carry):
            r0 = pl.multiple_of(r * LANES, LANES)
            out[pl.ds(r0, LANES), :] = ref[pl.ds(r0, LANES), :].astype(BF16)
            return carry
        lax.fori_loop(0, D // LANES, functools.partial(cast_rows, wgu_ref, wgu_bf), 0)
        lax.fori_loop(0, F // LANES, functools.partial(cast_rows, wd_ref, wd_bf), 0)

    @pl.when(valid)
    def _():
        x = x_ref[...]
        acc = None
        for f0 in range(0, F, MOE_FCHUNK):
            f1 = f0 + MOE_FCHUNK
            g = jnp.dot(x, wgu_bf[:, f0:f1], preferred_element_type=F32) + bgu_ref[0, :, f0:f1]
            u = (jnp.dot(x, wgu_bf[:, F + f0:F + f1], preferred_element_type=F32)
                 + bgu_ref[0, :, F + f0:F + f1])
            g = jnp.minimum(g, SWIGLU_LIMIT)
            u = jnp.clip(u, -SWIGLU_LIMIT, SWIGLU_LIMIT)
            glu = g * jax.nn.sigmoid(g * SWIGLU_ALPHA)
            act = ((u + 1.0) * glu).astype(BF16)
            part = jnp.dot(act, wd_bf[f0:f1, :], preferred_element_type=F32)
            acc = part if acc is None else acc + part
        y_ref[...] = (acc + bd_ref[0]).astype(y_ref.dtype)

    @pl.when(jnp.logical_not(valid))
    def _():
        y_ref[...] = jnp.zeros(y_ref.shape, y_ref.dtype)


def _moe_experts(layer, block_expert, n_valid, xs, wgu, bgu, wd, bd):
    n_rows, D = xs.shape
    L, E, _, F2 = wgu.shape
    F = F2 // 2
    n_blocks = n_rows // MOE_TILE

    def row_in(i, be, nv):
        return (jnp.minimum(i, nv[0] - 1), 0)

    def weight(i, be, nv):
        return (layer, be[i], 0, 0)

    def bias(i, be, nv):
        return (layer * E + be[i], 0, 0)

    grid_spec = pltpu.PrefetchScalarGridSpec(
        num_scalar_prefetch=2,
        grid=(n_blocks,),
        in_specs=[pl.BlockSpec((MOE_TILE, D), row_in),
                  pl.BlockSpec((None, None, D, F2), weight),
                  pl.BlockSpec((1, 1, F2), bias),
                  pl.BlockSpec((None, None, F, D), weight),
                  pl.BlockSpec((1, 1, D), bias)],
        out_specs=pl.BlockSpec((MOE_TILE, D), lambda i, be, nv: (i, 0)),
        scratch_shapes=[pltpu.VMEM((D, F2), BF16), pltpu.VMEM((F, D), BF16)],
    )
    return pl.pallas_call(
        _moe_kernel,
        grid_spec=grid_spec,
        out_shape=jax.ShapeDtypeStruct((n_rows, D), F32),
        compiler_params=_params("arbitrary"),
        name="moe_experts",
    )(block_expert, n_valid, xs, wgu, bgu.reshape(L * E, 1, F2), wd, bd.reshape(L * E, 1, D))


def _combine_kernel(x_ref, ys_ref, gate_ref, g_ref, b_ref, y_ref, *, alpha):
    gates = gate_ref[...]
    y = alpha * x_ref[...]
    for kk in range(TOP_K):
        y = y + ys_ref[kk].astype(F32) * gates[:, kk:kk + 1]
    y_ref[...] = _layer_norm(y, g_ref[...], b_ref[...])


def _combine_ln(x, ys_by_k, gates, g, b, alpha):
    T, D = x.shape
    tm = ROW_TILE
    row = lambda i: (i, 0)
    fixed = lambda i: (0, 0)
    return pl.pallas_call(
        functools.partial(_combine_kernel, alpha=alpha),
        grid=(T // tm,),
        in_specs=[pl.BlockSpec((tm, D), row),
                  pl.BlockSpec((TOP_K, tm, D), lambda i: (0, i, 0)),
                  pl.BlockSpec((tm, LANES), row),
                  pl.BlockSpec((1, D), fixed),
                  pl.BlockSpec((1, D), fixed)],
        out_specs=pl.BlockSpec((tm, D), row),
        out_shape=jax.ShapeDtypeStruct((T, D), F32),
        compiler_params=_params("parallel"),
        name="moe_combine_ln",
    )(x, ys_by_k, gates, g.reshape(1, D), b.reshape(1, D))


def _t5_bucket(dist):
    max_exact = N_BUCKETS // 2
    n = jnp.maximum(dist, 0)
    nf = jnp.maximum(n, 1).astype(F32)
    large = max_exact + (jnp.log(nf / max_exact) / math.log(MAX_DISTANCE / max_exact)
                         * (N_BUCKETS - max_exact)).astype(jnp.int32)
    large = jnp.minimum(large, N_BUCKETS - 1)
    return jnp.where(n < max_exact, n, large)


def _moba_bias_tables(rel_bias, S):
    n = ATT_TILE
    n_heads = rel_bias.shape[1]
    by_dist = rel_bias.T.astype(F32)[:, _t5_bucket(jnp.arange(2 * n + 1))] * LOG2E

    def toeplitz(v):
        return jnp.tile(v, (1, n))[:, :n * (2 * n - 1)].reshape(n_heads, n, 2 * n - 1)[:, :, :n]

    own = toeplitz(by_dist[:, :2 * n])
    prev = toeplitz(jnp.roll(by_dist[:, :2 * n], -n, axis=1))
    far = jnp.broadcast_to(by_dist[:, 2 * n][:, None, None], (n_heads, n, n))
    return jnp.stack([own, prev, far], axis=1)


def _block_average_matrix(S):
    a = np.zeros((LANES, S), np.float32)
    for n in range(S // MOBA_BLOCK):
        a[n, n * MOBA_BLOCK:(n + 1) * MOBA_BLOCK] = 1.0 / MOBA_BLOCK
    return jnp.asarray(a, BF16)


def _dispatch(top_idx, rank, counts):
    T = top_idx.shape[0]
    TK = T * TOP_K
    n_experts = counts.shape[0]
    padded = ((counts + MOE_TILE - 1) // MOE_TILE) * MOE_TILE
    start = jnp.cumsum(counts) - counts
    pend = jnp.cumsum(padded)
    pstart = pend - padded
    n_blocks = TK // MOE_TILE + n_experts
    n_valid = (pend[-1] // MOE_TILE).astype(jnp.int32)
    hot = top_idx.T[:, :, None] == jnp.arange(n_experts, dtype=jnp.int32)
    dest = jnp.sum(jnp.where(hot, pstart, 0), axis=-1) + rank.T
    blk = jnp.minimum(jnp.arange(n_blocks, dtype=jnp.int32), n_valid - 1)
    block_expert = jnp.minimum(
        jnp.sum((pend[None, :] <= (blk * MOE_TILE)[:, None]).astype(jnp.int32), axis=1),
        n_experts - 1)
    order = jnp.argsort(top_idx.reshape(-1)).astype(jnp.int32)
    shift = (start - pstart)[block_expert]
    slot = jnp.arange(n_blocks * MOE_TILE, dtype=jnp.int32).reshape(n_blocks, MOE_TILE)
    src = jnp.clip(slot + shift[:, None], 0, TK - 1).reshape(-1)
    slot_token = order[src] // TOP_K
    return slot_token, dest.reshape(-1), block_expert, n_valid.reshape(1)


def kernel(x, mem, w_in, b_forget, w_mix_out, rel_bias, ln1_g, ln1_b, w_cq, w_ck, w_cv, w_co,
           ln2_g, ln2_b, w_router, b_router, w_gate_up, b_gate_up, w_down, b_down, ln3_g, ln3_b):
    B, S, D = x.shape
    depth = w_in.shape[0]
    n_mem = mem.shape[1]
    n_experts = w_router.shape[2]
    n_heads = D // HEAD_DIM
    n_fox = n_heads // 2
    n_moba = n_heads - n_fox
    fox_w, moba_w = n_fox * HEAD_DIM, n_moba * HEAD_DIM
    T = B * S
    alpha = (2 * depth) ** 0.25
    assert S % ROW_TILE == 0 and S % ATT_TILE == 0 and D % LANES == 0
    assert ATT_TILE == MOBA_BLOCK and MOBA_BLOCK >= MAX_DISTANCE
    assert n_experts <= LANES and (T * TOP_K) % MOE_TILE == 0

    scale = HEAD_DIM ** -0.5 * LOG2E
    c0 = 3 * fox_w
    c1 = c0 + n_fox
    w_fq = w_in[:, :, :fox_w] * scale
    w_fkv = w_in[:, :, fox_w:c0]
    w_fg = jnp.pad(w_in[:, :, c0:c1], ((0, 0), (0, 0), (0, LANES - n_fox)))
    w_mq = w_in[:, :, c1:c1 + moba_w] * scale
    w_mkv = w_in[:, :, c1 + moba_w:]
    w_in_b = jnp.concatenate([w_fq, w_fkv, w_mq, w_mkv, w_fg], axis=2).astype(BF16)

    w_out_b = w_mix_out.astype(BF16)
    w_cq_b = (w_cq * (D // N_CROSS_HEADS) ** -0.5).astype(BF16)
    w_ckv_b = jnp.concatenate([w_ck, w_cv], axis=2).astype(BF16)
    w_co_b = w_co.astype(BF16)
    w_r = jnp.pad(w_router, ((0, 0), (0, 0), (0, LANES - n_experts)))
    w_r_hi = w_r.astype(BF16)
    w_r_lo = (w_r - w_r_hi.astype(F32)).astype(BF16)
    b_r = jnp.pad(b_router, ((0, 0), (0, LANES - n_experts))).reshape(depth, 1, LANES)

    bias_tiles = _moba_bias_tables(rel_bias, S)
    blk_avg = _block_average_matrix(S)
    mem2d = mem.reshape(B * n_mem, D)
    xt = x.reshape(T, D)

    for l in range(depth):
        qkv, f_logit = _in_proj(xt, w_in_b[l])
        c_aug = _fox_decay(f_logit, b_forget[l], S, n_fox)
        o_f = _fox_attention(qkv, c_aug, B, S, n_fox)
        o_m = _moba_attention(qkv, blk_avg, bias_tiles, B, S, n_fox, n_moba)
        xt = _mix_out(o_f, o_m, xt, w_out_b[l], ln1_g[l], ln1_b[l], alpha)

        kv = _mem_proj(mem2d, w_ckv_b[l])
        xt, xt_b, top_idx, gates, rank, counts = _cross_and_route(
            xt, w_cq_b[l], kv, w_co_b[l], ln2_g[l], ln2_b[l], w_r_hi[l], w_r_lo[l], b_r[l],
            alpha, S, n_mem, n_experts)

        slot_token, dest, block_expert, n_valid = _dispatch(
            top_idx[:, :TOP_K], rank[:, :TOP_K], counts[0, :n_experts].astype(jnp.int32))
        xs = xt_b[slot_token]
        ys = _moe_experts(l, block_expert, n_valid, xs, w_gate_up, b_gate_up, w_down, b_down)
        ys_by_k = ys[dest].reshape(TOP_K, T, D)
        xt = _combine_ln(xt, ys_by_k, gates, ln3_g[l], ln3_b[l], alpha)

    return xt.reshape(B, S, D)
```

```python
import functools
import math

import jax
import jax.numpy as jnp
import numpy as np
from jax import lax
from jax.experimental import pallas as pl
from jax.experimental.pallas import tpu as pltpu

F32 = jnp.float32
BF16 = jnp.bfloat16

HEAD_DIM = 64
N_BUCKETS = 32
MAX_DISTANCE = 128
MOBA_BLOCK = 256
MOBA_TOPK = 3
N_CROSS_HEADS = 4
TOP_K = 4
SWIGLU_LIMIT = 7.0
SWIGLU_ALPHA = 1.702
LN_EPS = 1e-5

LANES = 128
VMEM_LIMIT = 56 * 1024 * 1024

ROW_TILE = 512
ATT_TILE = MOBA_BLOCK
MOE_TILE = 512
MOE_FCHUNK = 512
MASKED = -1e30
LOG2E = math.log2(math.e)


def _params(*sem):
    return pltpu.CompilerParams(dimension_semantics=sem, vmem_limit_bytes=VMEM_LIMIT)


def _nt_dot(a, b):
    return lax.dot_general(a, b, (((1,), (1,)), ((), ())), preferred_element_type=F32)


def _layer_norm(y, g, b):
    mu = jnp.mean(y, axis=-1, keepdims=True)
    d = y - mu
    var = jnp.mean(d * d, axis=-1, keepdims=True)
    return d * lax.rsqrt(var + LN_EPS) * g + b


def _in_proj_kernel(x_ref, w_ref, qkv_ref, f_ref):
    xb = x_ref[...].astype(BF16)
    n_qkv = qkv_ref.shape[1]
    for n0 in range(0, n_qkv, 512):
        qkv_ref[:, n0:n0 + 512] = jnp.dot(
            xb, w_ref[:, n0:n0 + 512], preferred_element_type=F32).astype(BF16)
    f_ref[...] = jnp.dot(xb, w_ref[:, n_qkv:], preferred_element_type=F32)


def _in_proj(x, w):
    T, D = x.shape
    n_all = w.shape[1]
    n_qkv = n_all - LANES
    return pl.pallas_call(
        _in_proj_kernel,
        grid=(T // ROW_TILE,),
        in_specs=[pl.BlockSpec((ROW_TILE, D), lambda i: (i, 0)),
                  pl.BlockSpec((D, n_all), lambda i: (0, 0))],
        out_specs=[pl.BlockSpec((ROW_TILE, n_qkv), lambda i: (i, 0)),
                   pl.BlockSpec((ROW_TILE, LANES), lambda i: (i, 0))],
        out_shape=[jax.ShapeDtypeStruct((T, n_qkv), BF16),
                   jax.ShapeDtypeStruct((T, LANES), F32)],
        compiler_params=_params("parallel"),
        name="in_proj",
    )(x, w)


def _split3(x):
    p1 = x.astype(BF16)
    r1 = x - p1.astype(F32)
    p2 = r1.astype(BF16)
    p3 = (r1 - p2.astype(F32)).astype(BF16)
    return p1, p2, p3


def _decay_kernel(f_ref, b_ref, tri_ref, place_ref, c_ref):
    S = f_ref.shape[0]
    blk = tri_ref.shape[0]
    carry = jnp.zeros((1, LANES), F32)
    for j in range(S // blk):
        z = f_ref[j * blk:(j + 1) * blk, :] + b_ref[...]
        ls = jnp.minimum(z, 0.0) - jnp.log1p(jnp.exp(-jnp.abs(z)))
        c = carry
        for piece in _split3(ls):
            c = c + jnp.dot(tri_ref[...], piece, preferred_element_type=F32)
        carry = c[blk - 1:blk, :]
        aug = None
        for i, piece in enumerate(_split3(c * LOG2E)):
            t = jnp.dot(piece, place_ref[i], preferred_element_type=F32)
            aug = t if aug is None else aug + t
        c_ref[j * blk:(j + 1) * blk, :] = aug.astype(BF16)


def _fox_decay(f_logit, b_forget, S, n_fox):
    T = f_logit.shape[0]
    pairs = n_fox * HEAD_DIM // LANES
    blk = ATT_TILE
    tri = jnp.asarray(np.tril(np.ones((blk, blk), np.float32)), BF16)
    place = np.zeros((3, LANES, pairs * LANES), np.float32)
    for h in range(n_fox):
        for i in range(3):
            place[i, h, (h // 2) * LANES + 3 * (h % 2) + i] = -1.0
    b_pad = jnp.pad(b_forget, (0, LANES - n_fox)).reshape(1, LANES)
    return pl.pallas_call(
        _decay_kernel,
        grid=(T // S,),
        in_specs=[pl.BlockSpec((S, LANES), lambda b: (b, 0)),
                  pl.BlockSpec((1, LANES), lambda b: (0, 0)),
                  pl.BlockSpec((blk, blk), lambda b: (0, 0)),
                  pl.BlockSpec((3, LANES, pairs * LANES), lambda b: (0, 0, 0))],
        out_specs=pl.BlockSpec((S, pairs * LANES), lambda b: (b, 0)),
        out_shape=jax.ShapeDtypeStruct((T, pairs * LANES), BF16),
        compiler_params=_params("parallel"),
        name="fox_decay",
    )(f_logit, b_pad, tri, jnp.asarray(place, BF16))


VT_ROWS = HEAD_DIM + 16


def _online_update_t(s, vt, m, acc):
    m_new = jnp.maximum(m, jnp.max(s, axis=0, keepdims=True))
    p = jnp.exp2(s - m_new)
    acc_new = jnp.exp2(m - m_new) * acc + jnp.dot(vt, p.astype(BF16), preferred_element_type=F32)
    return m_new, acc_new


def _init_state(tq):
    return jnp.full((1, tq), MASKED, F32), jnp.zeros((VT_ROWS, tq), F32)


def _split_heads(q):
    lane = lax.broadcasted_iota(jnp.int32, q.shape, 1)
    zero = jnp.zeros((), q.dtype)
    return jnp.where(lane < HEAD_DIM, q, zero), jnp.where(lane < HEAD_DIM, zero, q)


def _causal_mask_t(tq):
    key = lax.broadcasted_iota(jnp.int32, (tq, tq), 0)
    qry = lax.broadcasted_iota(jnp.int32, (tq, tq), 1)
    return key <= qry


def _transpose_values(v_ref, vt_scr):
    tq = ATT_TILE
    S = v_ref.shape[0]
    for hh in range(2):
        vt_scr[hh * VT_ROWS + HEAD_DIM:(hh + 1) * VT_ROWS, :] = jnp.ones(
            (VT_ROWS - HEAD_DIM, S), BF16)
    for j in range(S // tq):
        vt = v_ref[j * tq:(j + 1) * tq, :].astype(F32).T.astype(BF16)
        for hh in range(2):
            vt_scr[hh * VT_ROWS:hh * VT_ROWS + HEAD_DIM, j * tq:(j + 1) * tq] = (
                vt[hh * HEAD_DIM:(hh + 1) * HEAD_DIM, :])


def _store_heads(o_ref, q0, state):
    tq = ATT_TILE
    o_t = jnp.concatenate(
        [acc[:HEAD_DIM] * (1.0 / acc[HEAD_DIM:HEAD_DIM + 1]) for _, acc in state], axis=0)
    o_ref[pl.ds(q0, tq), :] = o_t.T.astype(o_ref.dtype)


def _to_weights(x):
    return x.astype(F32).T.astype(BF16)


def _run_causal_tiles(n_blocks, score_tile, vt_scr, o_ref):
    tq = ATT_TILE
    tiles = [(qi, kb) for qi in range(n_blocks) for kb in [qi] + list(range(qi))]
    cur = score_tile(*tiles[0])
    state = None
    for i, (qi, kb) in enumerate(tiles):
        nxt = score_tile(*tiles[i + 1]) if i + 1 < len(tiles) else None
        if kb == qi:
            state = (_init_state(tq), _init_state(tq))
        new_state = []
        for hh in range(2):
            vt = vt_scr[hh * VT_ROWS:(hh + 1) * VT_ROWS, kb * tq:(kb + 1) * tq]
            new_state.append(_online_update_t(cur[hh], vt, *state[hh]))
        state = tuple(new_state)
        if kb == max(qi - 1, 0):
            _store_heads(o_ref, qi * tq, state)
        cur = nxt


def _fox_kernel(q_ref, k_ref, v_ref, c_ref, o_ref, vt_scr):
    tq = ATT_TILE
    S = q_ref.shape[0]
    causal = _causal_mask_t(tq)
    lane = lax.broadcasted_iota(jnp.int32, (tq, LANES), 1)
    ones = [jnp.where((lane >= 3 * hh) & (lane < 3 * hh + 3), 1.0, 0.0).astype(BF16)
            for hh in range(2)]
    _transpose_values(v_ref, vt_scr)
    q_weights = {}

    def score_tile(qi, kb):
        if qi not in q_weights:
            qh = _split_heads(q_ref[qi * tq:(qi + 1) * tq, :])
            q_weights.clear()
            q_weights[qi] = [_to_weights(jnp.concatenate([qh[hh], ones[hh]], axis=1))
                             for hh in range(2)]
        rows = slice(kb * tq, (kb + 1) * tq)
        ka = jnp.concatenate([k_ref[rows, :], c_ref[rows, :]], axis=1)
        out = []
        for hh in range(2):
            s = jnp.dot(ka, q_weights[qi][hh], preferred_element_type=F32)
            out.append(jnp.where(causal, s, MASKED) if kb == qi else s)
        return tuple(out)

    _run_causal_tiles(S // tq, score_tile, vt_scr, o_ref)


def _fox_attention(qkv, c_aug, B, S, n_fox):
    T = B * S
    pairs = n_fox * HEAD_DIM // LANES
    kb = pairs
    vb = 2 * pairs
    return pl.pallas_call(
        _fox_kernel,
        grid=(B, pairs),
        in_specs=[pl.BlockSpec((S, LANES), lambda b, p: (b, p)),
                  pl.BlockSpec((S, LANES), lambda b, p: (b, kb + p)),
                  pl.BlockSpec((S, LANES), lambda b, p: (b, vb + p)),
                  pl.BlockSpec((S, LANES), lambda b, p: (b, p))],
        out_specs=pl.BlockSpec((S, LANES), lambda b, p: (b, p)),
        out_shape=jax.ShapeDtypeStruct((T, pairs * LANES), BF16),
        scratch_shapes=[pltpu.VMEM((2 * VT_ROWS, S), BF16)],
        compiler_params=_params("parallel", "parallel"),
        name="fox_attention",
    )(qkv, qkv, qkv, c_aug)


def _moba_kernel(q_ref, k_ref, v_ref, a_ref, bias_ref, o_ref, vt_scr):
    tq = ATT_TILE
    S = q_ref.shape[0]
    rows = -(-(S // tq) // 8) * 8
    causal = _causal_mask_t(tq)
    blk_row = lax.broadcasted_iota(jnp.int32, (rows, tq), 0)
    _transpose_values(v_ref, vt_scr)

    kmean = jnp.dot(a_ref[...], k_ref[...], preferred_element_type=F32)
    kmean_hi = kmean.astype(BF16)
    kmean_lo = (kmean - kmean_hi.astype(F32)).astype(BF16)
    per_block = {}

    def prepare(qi):
        qh = _split_heads(q_ref[qi * tq:(qi + 1) * tq, :])
        qw = [_to_weights(qh[hh]) for hh in range(2)]
        offsets = []
        for hh in range(2):
            gate = (jnp.dot(kmean_hi, qw[hh], preferred_element_type=F32)
                    + jnp.dot(kmean_lo, qw[hh], preferred_element_type=F32))[:rows, :]
            rank = jnp.zeros((rows, tq), jnp.int32)
            for mb in range(qi):
                g_m = gate[mb:mb + 1, :]
                tie = jnp.where(blk_row > mb, 1, 0)
                rank = rank + jnp.where(g_m > gate, 1, jnp.where(g_m == gate, tie, 0))
            offsets.append(jnp.where(rank < MOBA_TOPK, 0.0, MASKED))
        return qw, offsets

    def score_tile(qi, kb):
        if qi not in per_block:
            per_block.clear()
            per_block[qi] = prepare(qi)
        qw, offsets = per_block[qi]
        k = k_ref[kb * tq:(kb + 1) * tq, :]
        out = []
        for hh in range(2):
            s = jnp.dot(k, qw[hh], preferred_element_type=F32)
            if kb == qi:
                s = jnp.where(causal, s + bias_ref[hh, 0], MASKED)
            elif kb == qi - 1:
                s = s + bias_ref[hh, 1] + offsets[hh][kb:kb + 1, :]
            else:
                s = s + (bias_ref[hh, 2, 0:1, :] + offsets[hh][kb:kb + 1, :])
            out.append(s)
        return tuple(out)

    _run_causal_tiles(S // tq, score_tile, vt_scr, o_ref)


def _moba_attention(qkv, blk_avg, bias_tiles, B, S, n_fox, n_moba):
    T = B * S
    pairs = n_moba * HEAD_DIM // LANES
    qb = 3 * n_fox * HEAD_DIM // LANES
    kb = qb + pairs
    vb = qb + 2 * pairs
    tq = ATT_TILE
    return pl.pallas_call(
        _moba_kernel,
        grid=(B, pairs),
        in_specs=[pl.BlockSpec((S, LANES), lambda b, p: (b, qb + p)),
                  pl.BlockSpec((S, LANES), lambda b, p: (b, kb + p)),
                  pl.BlockSpec((S, LANES), lambda b, p: (b, vb + p)),
                  pl.BlockSpec((LANES, S), lambda b, p: (0, 0)),
                  pl.BlockSpec((2, 3, tq, tq), lambda b, p: (p, 0, 0, 0))],
        out_specs=pl.BlockSpec((S, LANES), lambda b, p: (b, p)),
        out_shape=jax.ShapeDtypeStruct((T, pairs * LANES), BF16),
        scratch_shapes=[pltpu.VMEM((2 * VT_ROWS, S), BF16)],
        compiler_params=_params("parallel", "parallel"),
        name="moba_attention",
    )(qkv, qkv, qkv, blk_avg, bias_tiles)


def _mix_out_kernel(of_ref, om_ref, x_ref, w_ref, g_ref, b_ref, y_ref, *, alpha):
    half = of_ref.shape[1]
    h = (jnp.dot(of_ref[...], w_ref[:half, :], preferred_element_type=F32)
         + jnp.dot(om_ref[...], w_ref[half:, :], preferred_element_type=F32))
    y_ref[...] = _layer_norm(alpha * x_ref[...] + h, g_ref[...], b_ref[...])


def _mix_out(o_f, o_m, x, w, g, b, alpha):
    T, D = x.shape
    half = o_f.shape[1]
    row = lambda i: (i, 0)
    fixed = lambda i: (0, 0)
    return pl.pallas_call(
        functools.partial(_mix_out_kernel, alpha=alpha),
        grid=(T // ROW_TILE,),
        in_specs=[pl.BlockSpec((ROW_TILE, half), row),
                  pl.BlockSpec((ROW_TILE, half), row),
                  pl.BlockSpec((ROW_TILE, D), row),
                  pl.BlockSpec((D, D), fixed),
                  pl.BlockSpec((1, D), fixed),
                  pl.BlockSpec((1, D), fixed)],
        out_specs=pl.BlockSpec((ROW_TILE, D), row),
        out_shape=jax.ShapeDtypeStruct((T, D), F32),
        compiler_params=_params("parallel"),
        name="mix_out_ln",
    )(o_f, o_m, x, w, g.reshape(1, D), b.reshape(1, D))


def _mem_proj_kernel(m_ref, w_ref, kv_ref):
    mb = m_ref[...].astype(BF16)
    n = kv_ref.shape[1]
    for n0 in range(0, n, 512):
        kv_ref[:, n0:n0 + 512] = jnp.dot(
            mb, w_ref[:, n0:n0 + 512], preferred_element_type=F32).astype(BF16)


def _mem_proj(mem2d, w_kv):
    M, D = mem2d.shape
    n = w_kv.shape[1]
    tm = min(ROW_TILE, M)
    return pl.pallas_call(
        _mem_proj_kernel,
        grid=(M // tm,),
        in_specs=[pl.BlockSpec((tm, D), lambda i: (i, 0)),
                  pl.BlockSpec((D, n), lambda i: (0, 0))],
        out_specs=pl.BlockSpec((tm, n), lambda i: (i, 0)),
        out_shape=jax.ShapeDtypeStruct((M, n), BF16),
        compiler_params=_params("parallel"),
        name="mem_proj",
    )(mem2d, w_kv)


def _cross_kernel(x_ref, wq_ref, kv_ref, wo_ref, g_ref, b_ref, wr_hi_ref, wr_lo_ref, br_ref,
                  y_ref, idx_ref, gate_ref, count_ref, *, alpha, n_experts):
    D = x_ref.shape[1]

    @pl.when(pl.program_id(0) == 0)
    def _():
        count_ref[...] = jnp.zeros(count_ref.shape, F32)

    dh = D // N_CROSS_HEADS
    x = x_ref[...]
    q = jnp.dot(x.astype(BF16), wq_ref[...], preferred_element_type=F32).astype(BF16)
    heads = []
    for h in range(N_CROSS_HEADS):
        k_h = kv_ref[:, h * dh:(h + 1) * dh]
        v_h = kv_ref[:, D + h * dh:D + (h + 1) * dh]
        s = _nt_dot(q[:, h * dh:(h + 1) * dh], k_h)
        p = jnp.exp(s - jnp.max(s, axis=1, keepdims=True))
        p = p * (1.0 / jnp.sum(p, axis=1, keepdims=True))
        heads.append(jnp.dot(p.astype(BF16), v_h, preferred_element_type=F32).astype(BF16))
    o = jnp.concatenate(heads, axis=1)
    hproj = jnp.dot(o, wo_ref[...], preferred_element_type=F32)
    y = _layer_norm(alpha * x + hproj, g_ref[...], b_ref[...])
    y_ref[...] = y
    y_hi = y.astype(BF16)

    y_lo = (y - y_hi.astype(F32)).astype(BF16)
    logits = (jnp.dot(y_hi, wr_hi_ref[...], preferred_element_type=F32)
              + jnp.dot(y_lo, wr_hi_ref[...], preferred_element_type=F32)
              + jnp.dot(y_hi, wr_lo_ref[...], preferred_element_type=F32)
              + br_ref[...])
    lane = lax.broadcasted_iota(jnp.int32, logits.shape, 1)
    lane_f = lane.astype(F32)
    work = jnp.where(lane < n_experts, logits, -jnp.inf)
    idx_out = jnp.zeros(logits.shape, F32)
    val_out = jnp.zeros(logits.shape, F32)
    top = None
    for kk in range(TOP_K):
        best = jnp.max(work, axis=1, keepdims=True)
        arg = jnp.min(jnp.where(work == best, lane_f, float(LANES)), axis=1, keepdims=True)
        work = jnp.where(lane_f == arg, -jnp.inf, work)
        if top is None:
            top = best
        idx_out = jnp.where(lane == kk, arg, idx_out)
        val_out = jnp.where(lane == kk, jnp.exp(best - top), val_out)
    idx_ref[...] = idx_out.astype(jnp.int32)
    gate_ref[...] = val_out * (1.0 / jnp.sum(val_out, axis=1, keepdims=True))

    chosen = jnp.where(work == -jnp.inf, jnp.where(lane < n_experts, 1.0, 0.0), 0.0)
    count_ref[...] = count_ref[...] + jnp.sum(chosen, axis=0, keepdims=True)


def _cross_and_route(x, wq, kv, wo, g, b, wr_hi, wr_lo, br, alpha, S, n_mem, n_experts):
    T, D = x.shape
    tiles_per_batch = S // ROW_TILE
    row = lambda i: (i, 0)
    fixed = lambda i: (0, 0)
    return pl.pallas_call(
        functools.partial(_cross_kernel, alpha=alpha, n_experts=n_experts),
        grid=(T // ROW_TILE,),
        in_specs=[pl.BlockSpec((ROW_TILE, D), row),
                  pl.BlockSpec((D, D), fixed),
                  pl.BlockSpec((n_mem, 2 * D), lambda i: (i // tiles_per_batch, 0)),
                  pl.BlockSpec((D, D), fixed),
                  pl.BlockSpec((1, D), fixed),
                  pl.BlockSpec((1, D), fixed),
                  pl.BlockSpec((D, LANES), fixed),
                  pl.BlockSpec((D, LANES), fixed),
                  pl.BlockSpec((1, LANES), fixed)],
        out_specs=[pl.BlockSpec((ROW_TILE, D), row),
                   pl.BlockSpec((ROW_TILE, LANES), row),
                   pl.BlockSpec((ROW_TILE, LANES), row),
                   pl.BlockSpec((8, LANES), fixed)],
        out_shape=[jax.ShapeDtypeStruct((T, D), F32),
                   jax.ShapeDtypeStruct((T, LANES), jnp.int32),
                   jax.ShapeDtypeStruct((T, LANES), F32),
                   jax.ShapeDtypeStruct((8, LANES), F32)],
        compiler_params=_params("arbitrary"),
        name="cross_attn_router",
    )(x, wq, kv, wo, g.reshape(1, D), b.reshape(1, D), wr_hi, wr_lo, br)


def _moe_kernel(be_ref, nv_ref, idx_hbm, x_hbm, wgu_ref, bgu_ref, wd_ref, bd_ref, y_hbm,
                wgu_bf, wd_bf, xbuf, ybuf, idx_smem, isem, gsem, ssem, zsem):
    i = pl.program_id(0)
    nb = pl.num_programs(0)
    nv = nv_ref[0]
    D, F = wgu_ref.shape[0], wd_ref.shape[0]
    sub = D // LANES
    tile = xbuf.shape[1] // sub
    valid = i < nv
    cur = i & 1
    nxt = 1 - cur

    def idx_copy(step, s):
        return pltpu.make_async_copy(idx_hbm.at[step], idx_smem.at[s], isem.at[s])

    def start_gathers(par, segment, buf):
        for r in range(tile):
            src = pl.multiple_of(idx_smem[par, segment * tile + r], sub)
            pltpu.make_async_copy(x_hbm.at[pl.ds(src, sub), :],
                                  xbuf.at[buf, pl.ds(r * sub, sub), :], gsem.at[buf]).start()

    def wait_gathers(buf):
        pltpu.make_async_copy(xbuf.at[buf], xbuf.at[buf], gsem.at[buf]).wait()

    def start_scatters(par, buf):
        for r in range(tile):
            dst = pl.multiple_of(idx_smem[par, tile + r], sub)
            pltpu.make_async_copy(ybuf.at[buf, pl.ds(r * sub, sub), :],
                                  y_hbm.at[pl.ds(dst, sub), :], ssem.at[buf]).start()

    def wait_scatters(buf):
        pltpu.make_async_copy(ybuf.at[buf], ybuf.at[buf], ssem.at[buf]).wait()

    @pl.when(i == 0)
    def _():
        idx_copy(0, 0).start()
    idx_copy(i, cur).wait()

    @pl.when(i + 1 < nb)
    def _():
        idx_copy(i + 1, nxt).start()

    @pl.when(i == 0)
    def _():
        ybuf[1] = jnp.zeros(ybuf.shape[1:], ybuf.dtype)
        start_gathers(0, 2, 0)

    @pl.when(i <= nv)
    def _():
        wait_gathers(cur)

    @pl.when((i >= 1) & (i - 2 < nv))
    def _():
        wait_scatters(cur)

    @pl.when(valid & ((i == 0) | (be_ref[i] != be_ref[jnp.maximum(i - 1, 0)])))
    def _():
        def cast_rows(ref, out, r, carry):
            r0 = pl.multiple_of(r * LANES, LANES)
            out[pl.ds(r0, LANES), :] = ref[pl.ds(r0, LANES), :].astype(BF16)
            return carry
        lax.fori_loop(0, D // LANES, functools.partial(cast_rows, wgu_ref, wgu_bf), 0)
        lax.fori_loop(0, F // LANES, functools.partial(cast_rows, wd_ref, wd_bf), 0)

    def compute(par):
        x = jnp.concatenate([xbuf[par, pl.ds(j, tile, stride=sub), :].astype(BF16)
                             for j in range(sub)], axis=1)
        start_gathers(par, 0, 1 - par)
        start_scatters(par, 1 - par)
        acc = None
        for f0 in range(0, F, MOE_FCHUNK):
            f1 = f0 + MOE_FCHUNK
            g = jnp.dot(x, wgu_bf[:, f0:f1], preferred_element_type=F32) + bgu_ref[0, :, f0:f1]
            u = (jnp.dot(x, wgu_bf[:, F + f0:F + f1], preferred_element_type=F32)
                 + bgu_ref[0, :, F + f0:F + f1])
            g = jnp.minimum(g, SWIGLU_LIMIT)
            u = jnp.clip(u, -SWIGLU_LIMIT, SWIGLU_LIMIT)
            glu = g * jax.nn.sigmoid(g * SWIGLU_ALPHA)
            act = ((u + 1.0) * glu).astype(BF16)
            part = jnp.dot(act, wd_bf[f0:f1, :], preferred_element_type=F32)
            acc = part if acc is None else acc + part
        y = acc + bd_ref[0]
        for j in range(sub):
            ybuf[par, pl.ds(j, tile, stride=sub), :] = y[:, j * LANES:(j + 1) * LANES]

    for par in range(2):
        pl.when(valid & (cur == par))(functools.partial(compute, par))
        pl.when((i == nv) & (cur == par))(functools.partial(start_scatters, par, 1 - par))

    @pl.when(jnp.logical_not(valid))
    def _():
        ybuf[cur] = jnp.zeros(ybuf.shape[1:], ybuf.dtype)
        row0 = pl.multiple_of(i * tile * sub, tile * sub)
        zero_fill = pltpu.make_async_copy(ybuf.at[cur], y_hbm.at[pl.ds(row0, tile * sub), :],
                                          zsem)
        zero_fill.start()
        zero_fill.wait()

    @pl.when((i == nb - 1) & (i - 1 < nv))
    def _():
        wait_scatters(nxt)


def _moe_experts(layer, block_expert, n_valid, step_idx, x, wgu, bgu, wd, bd):
    L, E, D, F2 = wgu.shape
    n_blocks = step_idx.shape[0]
    F = F2 // 2
    sub = D // LANES
    assert n_blocks >= 3 and x.shape[1] == LANES

    def weight(i, be, nv):
        return (layer, be[i], 0, 0)

    def bias(i, be, nv):
        return (layer * E + be[i], 0, 0)

    grid_spec = pltpu.PrefetchScalarGridSpec(
        num_scalar_prefetch=2,
        grid=(n_blocks,),
        in_specs=[pl.BlockSpec(memory_space=pl.ANY),
                  pl.BlockSpec(memory_space=pl.ANY),
                  pl.BlockSpec((None, None, D, F2), weight),
                  pl.BlockSpec((1, 1, F2), bias),
                  pl.BlockSpec((None, None, F, D), weight),
                  pl.BlockSpec((1, 1, D), bias)],
        out_specs=pl.BlockSpec(memory_space=pl.ANY),
        scratch_shapes=[pltpu.VMEM((D, F2), BF16), pltpu.VMEM((F, D), BF16),
                        pltpu.VMEM((2, MOE_TILE * sub, LANES), F32),
                        pltpu.VMEM((2, MOE_TILE * sub, LANES), F32),
                        pltpu.SMEM((2, 3 * MOE_TILE), jnp.int32),
                        pltpu.SemaphoreType.DMA((2,)), pltpu.SemaphoreType.DMA((2,)),
                        pltpu.SemaphoreType.DMA((2,)), pltpu.SemaphoreType.DMA(())],
    )
    return pl.pallas_call(
        _moe_kernel,
        grid_spec=grid_spec,
        out_shape=jax.ShapeDtypeStruct(((n_blocks + 1) * MOE_TILE * sub, LANES), F32),
        compiler_params=_params("arbitrary"),
        name="moe_experts",
    )(block_expert, n_valid, step_idx, x, wgu, bgu.reshape(L * E, 1, F2), wd,
      bd.reshape(L * E, 1, D))


def _combine_kernel(x_ref, *refs, alpha):
    ys_refs, (gate_ref, g_ref, b_ref, y_ref) = refs[:TOP_K], refs[TOP_K:]
    tm, D = x_ref.shape
    sub = D // LANES
    gates = gate_ref[...]
    cols = []
    for j in range(sub):
        y = alpha * x_ref[:, j * LANES:(j + 1) * LANES]
        for kk in range(TOP_K):
            y = y + ys_refs[kk][pl.ds(j, tm, stride=sub), :] * gates[:, kk:kk + 1]
        cols.append(y)
    y_ref[...] = _layer_norm(jnp.concatenate(cols, axis=1), g_ref[...], b_ref[...])


def _combine_ln(x, ys, gates, g, b, alpha):
    T, D = x.shape
    tm = ROW_TILE
    sub = D // LANES
    row = lambda i: (i, 0)
    fixed = lambda i: (0, 0)
    ys_specs = [pl.BlockSpec((tm * sub, LANES),
                             functools.partial(lambda k, i: (k * (T // tm) + i, 0), k))
                for k in range(TOP_K)]
    return pl.pallas_call(
        functools.partial(_combine_kernel, alpha=alpha),
        grid=(T // tm,),
        in_specs=[pl.BlockSpec((tm, D), row)] + ys_specs + [
            pl.BlockSpec((tm, LANES), row),
            pl.BlockSpec((1, D), fixed),
            pl.BlockSpec((1, D), fixed)],
        out_specs=pl.BlockSpec((tm, D), row),
        out_shape=jax.ShapeDtypeStruct((T, D), F32),
        compiler_params=_params("parallel"),
        name="moe_combine_ln",
    )(x, *([ys] * TOP_K), gates, g.reshape(1, D), b.reshape(1, D))


def _t5_bucket(dist):
    max_exact = N_BUCKETS // 2
    n = jnp.maximum(dist, 0)
    nf = jnp.maximum(n, 1).astype(F32)
    large = max_exact + (jnp.log(nf / max_exact) / math.log(MAX_DISTANCE / max_exact)
                         * (N_BUCKETS - max_exact)).astype(jnp.int32)
    large = jnp.minimum(large, N_BUCKETS - 1)
    return jnp.where(n < max_exact, n, large)


def _moba_bias_tables(rel_bias, S):
    n = ATT_TILE
    n_heads = rel_bias.shape[1]
    by_dist = rel_bias.T.astype(F32)[:, _t5_bucket(jnp.arange(2 * n + 1))] * LOG2E

    def toeplitz(v):
        return jnp.tile(v, (1, n))[:, :n * (2 * n - 1)].reshape(n_heads, n, 2 * n - 1)[:, :, :n]

    own = toeplitz(by_dist[:, :2 * n])
    prev = toeplitz(jnp.roll(by_dist[:, :2 * n], -n, axis=1))
    far = jnp.broadcast_to(by_dist[:, 2 * n][:, None, None], (n_heads, n, n))
    return jnp.stack([own, prev, far], axis=1)


def _block_average_matrix(S):
    a = np.zeros((LANES, S), np.float32)
    for n in range(S // MOBA_BLOCK):
        a[n, n * MOBA_BLOCK:(n + 1) * MOBA_BLOCK] = 1.0 / MOBA_BLOCK
    return jnp.asarray(a, BF16)


def _dispatch(top_idx, counts, rows_per_token):
    T = top_idx.shape[0]
    TK = T * TOP_K
    n_experts = counts.shape[0]
    padded = ((counts + MOE_TILE - 1) // MOE_TILE) * MOE_TILE
    start = jnp.cumsum(counts) - counts
    pend = jnp.cumsum(padded)
    pstart = pend - padded
    n_blocks = TK // MOE_TILE + n_experts
    n_valid = (pend[-1] // MOE_TILE).astype(jnp.int32)
    blk = jnp.minimum(jnp.arange(n_blocks, dtype=jnp.int32), n_valid - 1)
    block_expert = jnp.minimum(
        jnp.sum((pend[None, :] <= (blk * MOE_TILE)[:, None]).astype(jnp.int32), axis=1),
        n_experts - 1)
    order = jnp.argsort(top_idx.reshape(-1)).astype(jnp.int32)
    slot = jnp.arange(n_blocks * MOE_TILE, dtype=jnp.int32).reshape(n_blocks, MOE_TILE)
    within = slot - pstart[block_expert][:, None]
    real = within < counts[block_expert][:, None]
    src = jnp.clip(within + start[block_expert][:, None], 0, TK - 1)
    flat = order[src.reshape(-1)].reshape(n_blocks, MOE_TILE)
    token = flat // TOP_K
    out_row = jnp.where(real, (flat % TOP_K) * T + token,
                        slot + (TK - (start + counts)[block_expert])[:, None])
    spare = n_blocks * MOE_TILE + jnp.arange(MOE_TILE, dtype=jnp.int32)
    step_idx = jnp.concatenate(
        [jnp.concatenate([token[1:], token[-1:]], axis=0),
         jnp.concatenate([spare[None, :], out_row[:-1]], axis=0),
         token], axis=1)
    return step_idx * rows_per_token, block_expert, n_valid.reshape(1)


def kernel(x, mem, w_in, b_forget, w_mix_out, rel_bias, ln1_g, ln1_b, w_cq, w_ck, w_cv, w_co,
           ln2_g, ln2_b, w_router, b_router, w_gate_up, b_gate_up, w_down, b_down, ln3_g, ln3_b):
    B, S, D = x.shape
    depth = w_in.shape[0]
    n_mem = mem.shape[1]
    n_experts = w_router.shape[2]
    n_heads = D // HEAD_DIM
    n_fox = n_heads // 2
    n_moba = n_heads - n_fox
    fox_w, moba_w = n_fox * HEAD_DIM, n_moba * HEAD_DIM
    T = B * S
    alpha = (2 * depth) ** 0.25
    assert S % ROW_TILE == 0 and S % ATT_TILE == 0 and D % LANES == 0
    assert ATT_TILE == MOBA_BLOCK and MOBA_BLOCK >= MAX_DISTANCE
    assert n_experts <= LANES and (T * TOP_K) % MOE_TILE == 0

    scale = HEAD_DIM ** -0.5 * LOG2E
    c0 = 3 * fox_w
    c1 = c0 + n_fox
    w_fq = w_in[:, :, :fox_w] * scale
    w_fkv = w_in[:, :, fox_w:c0]
    w_fg = jnp.pad(w_in[:, :, c0:c1], ((0, 0), (0, 0), (0, LANES - n_fox)))
    w_mq = w_in[:, :, c1:c1 + moba_w] * scale
    w_mkv = w_in[:, :, c1 + moba_w:]
    w_in_b = jnp.concatenate([w_fq, w_fkv, w_mq, w_mkv, w_fg], axis=2).astype(BF16)

    w_out_b = w_mix_out.astype(BF16)
    w_cq_b = (w_cq * (D // N_CROSS_HEADS) ** -0.5).astype(BF16)
    w_ckv_b = jnp.concatenate([w_ck, w_cv], axis=2).astype(BF16)
    w_co_b = w_co.astype(BF16)
    w_r = jnp.pad(w_router, ((0, 0), (0, 0), (0, LANES - n_experts)))
    w_r_hi = w_r.astype(BF16)
    w_r_lo = (w_r - w_r_hi.astype(F32)).astype(BF16)
    b_r = jnp.pad(b_router, ((0, 0), (0, LANES - n_experts))).reshape(depth, 1, LANES)

    bias_tiles = _moba_bias_tables(rel_bias, S)
    blk_avg = _block_average_matrix(S)
    mem2d = mem.reshape(B * n_mem, D)
    xt = x.reshape(T, D)

    for l in range(depth):
        qkv, f_logit = _in_proj(xt, w_in_b[l])
        c_aug = _fox_decay(f_logit, b_forget[l], S, n_fox)
        o_f = _fox_attention(qkv, c_aug, B, S, n_fox)
        o_m = _moba_attention(qkv, blk_avg, bias_tiles, B, S, n_fox, n_moba)
        xt = _mix_out(o_f, o_m, xt, w_out_b[l], ln1_g[l], ln1_b[l], alpha)

        kv = _mem_proj(mem2d, w_ckv_b[l])
        xt, top_idx, gates, counts = _cross_and_route(
            xt, w_cq_b[l], kv, w_co_b[l], ln2_g[l], ln2_b[l], w_r_hi[l], w_r_lo[l], b_r[l],
            alpha, S, n_mem, n_experts)

        step_idx, block_expert, n_valid = _dispatch(
            top_idx[:, :TOP_K], counts[0, :n_experts].astype(jnp.int32), D // LANES)
        ys = _moe_experts(l, block_expert, n_valid, step_idx, xt.reshape(T * D // LANES, LANES),
                          w_gate_up, b_gate_up, w_down, b_down)
        xt = _combine_ln(xt, ys, gates, ln3_g[l], ln3_b[l], alpha)

    return xt.reshape(B, S, D)
```

```python
import functools
import math

import jax
import jax.numpy as jnp
import numpy as np
from jax import lax
from jax.experimental import pallas as pl
from jax.experimental.pallas import tpu as pltpu

F32 = jnp.float32
BF16 = jnp.bfloat16

HEAD_DIM = 64
N_BUCKETS = 32
MAX_DISTANCE = 128
MOBA_BLOCK = 256
MOBA_TOPK = 3
N_CROSS_HEADS = 4
TOP_K = 4
SWIGLU_LIMIT = 7.0
SWIGLU_ALPHA = 1.702
LN_EPS = 1e-5

LANES = 128
VMEM_LIMIT = 56 * 1024 * 1024

ROW_TILE = 512
ATT_TILE = MOBA_BLOCK
MOE_TILE = 512
MOE_FCHUNK = 512
MASKED = -1e30
LOG2E = math.log2(math.e)


def _params(*sem):
    return pltpu.CompilerParams(dimension_semantics=sem, vmem_limit_bytes=VMEM_LIMIT)


def _nt_dot(a, b):
    return lax.dot_general(a, b, (((1,), (1,)), ((), ())), preferred_element_type=F32)


def _layer_norm(y, g, b):
    mu = jnp.mean(y, axis=-1, keepdims=True)
    d = y - mu
    var = jnp.mean(d * d, axis=-1, keepdims=True)
    return d * lax.rsqrt(var + LN_EPS) * g + b


def _in_proj_kernel(x_ref, w_ref, qkv_ref, f_ref):
    xb = x_ref[...].astype(BF16)
    n_qkv = qkv_ref.shape[1]
    for n0 in range(0, n_qkv, 512):
        qkv_ref[:, n0:n0 + 512] = jnp.dot(
            xb, w_ref[:, n0:n0 + 512], preferred_element_type=F32).astype(BF16)
    f_ref[...] = jnp.dot(xb, w_ref[:, n_qkv:], preferred_element_type=F32)


def _in_proj(x, w):
    T, D = x.shape
    n_all = w.shape[1]
    n_qkv = n_all - LANES
    return pl.pallas_call(
        _in_proj_kernel,
        grid=(T // ROW_TILE,),
        in_specs=[pl.BlockSpec((ROW_TILE, D), lambda i: (i, 0)),
                  pl.BlockSpec((D, n_all), lambda i: (0, 0))],
        out_specs=[pl.BlockSpec((ROW_TILE, n_qkv), lambda i: (i, 0)),
                   pl.BlockSpec((ROW_TILE, LANES), lambda i: (i, 0))],
        out_shape=[jax.ShapeDtypeStruct((T, n_qkv), BF16),
                   jax.ShapeDtypeStruct((T, LANES), F32)],
        compiler_params=_params("parallel"),
        name="in_proj",
    )(x, w)


def _split3(x):
    p1 = x.astype(BF16)
    r1 = x - p1.astype(F32)
    p2 = r1.astype(BF16)
    p3 = (r1 - p2.astype(F32)).astype(BF16)
    return p1, p2, p3


def _decay_kernel(f_ref, b_ref, tri_ref, place_ref, c_ref):
    S = f_ref.shape[0]
    blk = tri_ref.shape[0]
    carry = jnp.zeros((1, LANES), F32)
    for j in range(S // blk):
        z = f_ref[j * blk:(j + 1) * blk, :] + b_ref[...]
        ls = jnp.minimum(z, 0.0) - jnp.log1p(jnp.exp(-jnp.abs(z)))
        c = carry
        for piece in _split3(ls):
            c = c + jnp.dot(tri_ref[...], piece, preferred_element_type=F32)
        carry = c[blk - 1:blk, :]
        aug = None
        for i, piece in enumerate(_split3(c * LOG2E)):
            t = jnp.dot(piece, place_ref[i], preferred_element_type=F32)
            aug = t if aug is None else aug + t
        c_ref[j * blk:(j + 1) * blk, :] = aug.astype(BF16)


def _fox_decay(f_logit, b_forget, S, n_fox):
    T = f_logit.shape[0]
    pairs = n_fox * HEAD_DIM // LANES
    blk = ATT_TILE
    tri = jnp.asarray(np.tril(np.ones((blk, blk), np.float32)), BF16)
    place = np.zeros((3, LANES, pairs * LANES), np.float32)
    for h in range(n_fox):
        for i in range(3):
            place[i, h, (h // 2) * LANES + 3 * (h % 2) + i] = -1.0
    b_pad = jnp.pad(b_forget, (0, LANES - n_fox)).reshape(1, LANES)
    return pl.pallas_call(
        _decay_kernel,
        grid=(T // S,),
        in_specs=[pl.BlockSpec((S, LANES), lambda b: (b, 0)),
                  pl.BlockSpec((1, LANES), lambda b: (0, 0)),
                  pl.BlockSpec((blk, blk), lambda b: (0, 0)),
                  pl.BlockSpec((3, LANES, pairs * LANES), lambda b: (0, 0, 0))],
        out_specs=pl.BlockSpec((S, pairs * LANES), lambda b: (b, 0)),
        out_shape=jax.ShapeDtypeStruct((T, pairs * LANES), BF16),
        compiler_params=_params("parallel"),
        name="fox_decay",
    )(f_logit, b_pad, tri, jnp.asarray(place, BF16))


VT_ROWS = HEAD_DIM + 16


def _online_update_t(s, vt, m, acc):
    m_new = jnp.maximum(m, jnp.max(s, axis=0, keepdims=True))
    p = jnp.exp2(s - m_new)
    acc_new = jnp.exp2(m - m_new) * acc + jnp.dot(vt, p.astype(BF16), preferred_element_type=F32)
    return m_new, acc_new


def _init_state(tq):
    return jnp.full((1, tq), MASKED, F32), jnp.zeros((VT_ROWS, tq), F32)


def _split_heads(q):
    lane = lax.broadcasted_iota(jnp.int32, q.shape, 1)
    zero = jnp.zeros((), q.dtype)
    return jnp.where(lane < HEAD_DIM, q, zero), jnp.where(lane < HEAD_DIM, zero, q)


def _causal_mask_t(tq):
    key = lax.broadcasted_iota(jnp.int32, (tq, tq), 0)
    qry = lax.broadcasted_iota(jnp.int32, (tq, tq), 1)
    return key <= qry


def _transpose_values(v_ref, vt_scr):
    tq = ATT_TILE
    S = v_ref.shape[0]
    for hh in range(2):
        vt_scr[hh * VT_ROWS + HEAD_DIM:(hh + 1) * VT_ROWS, :] = jnp.ones(
            (VT_ROWS - HEAD_DIM, S), BF16)
    for j in range(S // tq):
        vt = v_ref[j * tq:(j + 1) * tq, :].astype(F32).T.astype(BF16)
        for hh in range(2):
            vt_scr[hh * VT_ROWS:hh * VT_ROWS + HEAD_DIM, j * tq:(j + 1) * tq] = (
                vt[hh * HEAD_DIM:(hh + 1) * HEAD_DIM, :])


def _store_heads(o_ref, q0, state):
    tq = ATT_TILE
    o_t = jnp.concatenate(
        [acc[:HEAD_DIM] * (1.0 / acc[HEAD_DIM:HEAD_DIM + 1]) for _, acc in state], axis=0)
    o_ref[pl.ds(q0, tq), :] = o_t.T.astype(o_ref.dtype)


def _to_weights(x):
    return x.astype(F32).T.astype(BF16)


def _run_causal_tiles(n_blocks, score_tile, vt_scr, o_ref):
    tq = ATT_TILE
    tiles = [(qi, kb) for qi in range(n_blocks) for kb in [qi] + list(range(qi))]
    cur = score_tile(*tiles[0])
    state = None
    for i, (qi, kb) in enumerate(tiles):
        nxt = score_tile(*tiles[i + 1]) if i + 1 < len(tiles) else None
        if kb == qi:
            state = (_init_state(tq), _init_state(tq))
        new_state = []
        for hh in range(2):
            vt = vt_scr[hh * VT_ROWS:(hh + 1) * VT_ROWS, kb * tq:(kb + 1) * tq]
            new_state.append(_online_update_t(cur[hh], vt, *state[hh]))
        state = tuple(new_state)
        if kb == max(qi - 1, 0):
            _store_heads(o_ref, qi * tq, state)
        cur = nxt


def _fox_kernel(q_ref, k_ref, v_ref, c_ref, o_ref, vt_scr):
    tq = ATT_TILE
    S = q_ref.shape[0]
    causal = _causal_mask_t(tq)
    lane = lax.broadcasted_iota(jnp.int32, (tq, LANES), 1)
    ones = [jnp.where((lane >= 3 * hh) & (lane < 3 * hh + 3), 1.0, 0.0).astype(BF16)
            for hh in range(2)]
    _transpose_values(v_ref, vt_scr)
    q_weights = {}

    def score_tile(qi, kb):
        if qi not in q_weights:
            qh = _split_heads(q_ref[qi * tq:(qi + 1) * tq, :])
            q_weights.clear()
            q_weights[qi] = [_to_weights(jnp.concatenate([qh[hh], ones[hh]], axis=1))
                             for hh in range(2)]
        rows = slice(kb * tq, (kb + 1) * tq)
        ka = jnp.concatenate([k_ref[rows, :], c_ref[rows, :]], axis=1)
        out = []
        for hh in range(2):
            s = jnp.dot(ka, q_weights[qi][hh], preferred_element_type=F32)
            out.append(jnp.where(causal, s, MASKED) if kb == qi else s)
        return tuple(out)

    _run_causal_tiles(S // tq, score_tile, vt_scr, o_ref)


def _fox_attention(qkv, c_aug, B, S, n_fox):
    T = B * S
    pairs = n_fox * HEAD_DIM // LANES
    kb = pairs
    vb = 2 * pairs
    return pl.pallas_call(
        _fox_kernel,
        grid=(B, pairs),
        in_specs=[pl.BlockSpec((S, LANES), lambda b, p: (b, p)),
                  pl.BlockSpec((S, LANES), lambda b, p: (b, kb + p)),
                  pl.BlockSpec((S, LANES), lambda b, p: (b, vb + p)),
                  pl.BlockSpec((S, LANES), lambda b, p: (b, p))],
        out_specs=pl.BlockSpec((S, LANES), lambda b, p: (b, p)),
        out_shape=jax.ShapeDtypeStruct((T, pairs * LANES), BF16),
        scratch_shapes=[pltpu.VMEM((2 * VT_ROWS, S), BF16)],
        compiler_params=_params("parallel", "parallel"),
        name="fox_attention",
    )(qkv, qkv, qkv, c_aug)


def _moba_kernel(q_ref, k_ref, v_ref, a_ref, bias_ref, o_ref, vt_scr):
    tq = ATT_TILE
    S = q_ref.shape[0]
    rows = -(-(S // tq) // 8) * 8
    causal = _causal_mask_t(tq)
    blk_row = lax.broadcasted_iota(jnp.int32, (rows, tq), 0)
    _transpose_values(v_ref, vt_scr)

    kmean = jnp.dot(a_ref[...], k_ref[...], preferred_element_type=F32)
    kmean_hi = kmean.astype(BF16)
    kmean_lo = (kmean - kmean_hi.astype(F32)).astype(BF16)
    per_block = {}

    def prepare(qi):
        qh = _split_heads(q_ref[qi * tq:(qi + 1) * tq, :])
        qw = [_to_weights(qh[hh]) for hh in range(2)]
        offsets = []
        for hh in range(2):
            gate = (jnp.dot(kmean_hi, qw[hh], preferred_element_type=F32)
                    + jnp.dot(kmean_lo, qw[hh], preferred_element_type=F32))[:rows, :]
            rank = jnp.zeros((rows, tq), jnp.int32)
            for mb in range(qi):
                g_m = gate[mb:mb + 1, :]
                tie = jnp.where(blk_row > mb, 1, 0)
                rank = rank + jnp.where(g_m > gate, 1, jnp.where(g_m == gate, tie, 0))
            offsets.append(jnp.where(rank < MOBA_TOPK, 0.0, MASKED))
        return qw, offsets

    def score_tile(qi, kb):
        if qi not in per_block:
            per_block.clear()
            per_block[qi] = prepare(qi)
        qw, offsets = per_block[qi]
        k = k_ref[kb * tq:(kb + 1) * tq, :]
        out = []
        for hh in range(2):
            s = jnp.dot(k, qw[hh], preferred_element_type=F32)
            if kb == qi:
                s = jnp.where(causal, s + bias_ref[hh, 0], MASKED)
            elif kb == qi - 1:
                s = s + bias_ref[hh, 1] + offsets[hh][kb:kb + 1, :]
            else:
                s = s + (bias_ref[hh, 2, 0:1, :] + offsets[hh][kb:kb + 1, :])
            out.append(s)
        return tuple(out)

    _run_causal_tiles(S // tq, score_tile, vt_scr, o_ref)


def _moba_attention(qkv, blk_avg, bias_tiles, B, S, n_fox, n_moba):
    T = B * S
    pairs = n_moba * HEAD_DIM // LANES
    qb = 3 * n_fox * HEAD_DIM // LANES
    kb = qb + pairs
    vb = qb + 2 * pairs
    tq = ATT_TILE
    return pl.pallas_call(
        _moba_kernel,
        grid=(B, pairs),
        in_specs=[pl.BlockSpec((S, LANES), lambda b, p: (b, qb + p)),
                  pl.BlockSpec((S, LANES), lambda b, p: (b, kb + p)),
                  pl.BlockSpec((S, LANES), lambda b, p: (b, vb + p)),
                  pl.BlockSpec((LANES, S), lambda b, p: (0, 0)),
                  pl.BlockSpec((2, 3, tq, tq), lambda b, p: (p, 0, 0, 0))],
        out_specs=pl.BlockSpec((S, LANES), lambda b, p: (b, p)),
        out_shape=jax.ShapeDtypeStruct((T, pairs * LANES), BF16),
        scratch_shapes=[pltpu.VMEM((2 * VT_ROWS, S), BF16)],
        compiler_params=_params("parallel", "parallel"),
        name="moba_attention",
    )(qkv, qkv, qkv, blk_avg, bias_tiles)


def _mix_out_kernel(of_ref, om_ref, x_ref, w_ref, g_ref, b_ref, y_ref, *, alpha):
    half = of_ref.shape[1]
    h = (jnp.dot(of_ref[...], w_ref[:half, :], preferred_element_type=F32)
         + jnp.dot(om_ref[...], w_ref[half:, :], preferred_element_type=F32))
    y_ref[...] = _layer_norm(alpha * x_ref[...] + h, g_ref[...], b_ref[...])


def _mix_out(o_f, o_m, x, w, g, b, alpha):
    T, D = x.shape
    half = o_f.shape[1]
    row = lambda i: (i, 0)
    fixed = lambda i: (0, 0)
    return pl.pallas_call(
        functools.partial(_mix_out_kernel, alpha=alpha),
        grid=(T // ROW_TILE,),
        in_specs=[pl.BlockSpec((ROW_TILE, half), row),
                  pl.BlockSpec((ROW_TILE, half), row),
                  pl.BlockSpec((ROW_TILE, D), row),
                  pl.BlockSpec((D, D), fixed),
                  pl.BlockSpec((1, D), fixed),
                  pl.BlockSpec((1, D), fixed)],
        out_specs=pl.BlockSpec((ROW_TILE, D), row),
        out_shape=jax.ShapeDtypeStruct((T, D), F32),
        compiler_params=_params("parallel"),
        name="mix_out_ln",
    )(o_f, o_m, x, w, g.reshape(1, D), b.reshape(1, D))


def _mem_proj_kernel(m_ref, w_ref, kv_ref):
    mb = m_ref[...].astype(BF16)
    n = kv_ref.shape[1]
    for n0 in range(0, n, 512):
        kv_ref[:, n0:n0 + 512] = jnp.dot(
            mb, w_ref[:, n0:n0 + 512], preferred_element_type=F32).astype(BF16)


def _mem_proj(mem2d, w_kv):
    M, D = mem2d.shape
    n = w_kv.shape[1]
    tm = min(ROW_TILE, M)
    return pl.pallas_call(
        _mem_proj_kernel,
        grid=(M // tm,),
        in_specs=[pl.BlockSpec((tm, D), lambda i: (i, 0)),
                  pl.BlockSpec((D, n), lambda i: (0, 0))],
        out_specs=pl.BlockSpec((tm, n), lambda i: (i, 0)),
        out_shape=jax.ShapeDtypeStruct((M, n), BF16),
        compiler_params=_params("parallel"),
        name="mem_proj",
    )(mem2d, w_kv)


def _cross_kernel(x_ref, wq_ref, kv_ref, wo_ref, g_ref, b_ref, wr_hi_ref, wr_lo_ref, br_ref,
                  y_ref, yt_ref, idx_ref, gate_ref, count_ref, *, alpha, n_experts):
    D = x_ref.shape[1]

    @pl.when(pl.program_id(0) == 0)
    def _():
        count_ref[...] = jnp.zeros(count_ref.shape, F32)

    dh = D // N_CROSS_HEADS
    x = x_ref[...]
    q = jnp.dot(x.astype(BF16), wq_ref[...], preferred_element_type=F32).astype(BF16)
    heads = []
    for h in range(N_CROSS_HEADS):
        k_h = kv_ref[:, h * dh:(h + 1) * dh]
        v_h = kv_ref[:, D + h * dh:D + (h + 1) * dh]
        s = _nt_dot(q[:, h * dh:(h + 1) * dh], k_h)
        p = jnp.exp(s - jnp.max(s, axis=1, keepdims=True))
        p = p * (1.0 / jnp.sum(p, axis=1, keepdims=True))
        heads.append(jnp.dot(p.astype(BF16), v_h, preferred_element_type=F32).astype(BF16))
    o = jnp.concatenate(heads, axis=1)
    hproj = jnp.dot(o, wo_ref[...], preferred_element_type=F32)
    y = _layer_norm(alpha * x + hproj, g_ref[...], b_ref[...])
    y_ref[...] = y
    for j in range(D // LANES):
        yt_ref[pl.ds(j, y.shape[0], stride=D // LANES), :] = y[:, j * LANES:(j + 1) * LANES]
    y_hi = y.astype(BF16)

    y_lo = (y - y_hi.astype(F32)).astype(BF16)
    logits = (jnp.dot(y_hi, wr_hi_ref[...], preferred_element_type=F32)
              + jnp.dot(y_lo, wr_hi_ref[...], preferred_element_type=F32)
              + jnp.dot(y_hi, wr_lo_ref[...], preferred_element_type=F32)
              + br_ref[...])
    lane = lax.broadcasted_iota(jnp.int32, logits.shape, 1)
    lane_f = lane.astype(F32)
    work = jnp.where(lane < n_experts, logits, -jnp.inf)
    idx_out = jnp.zeros(logits.shape, F32)
    val_out = jnp.zeros(logits.shape, F32)
    top = None
    for kk in range(TOP_K):
        best = jnp.max(work, axis=1, keepdims=True)
        arg = jnp.min(jnp.where(work == best, lane_f, float(LANES)), axis=1, keepdims=True)
        work = jnp.where(lane_f == arg, -jnp.inf, work)
        if top is None:
            top = best
        idx_out = jnp.where(lane == kk, arg, idx_out)
        val_out = jnp.where(lane == kk, jnp.exp(best - top), val_out)
    idx_ref[...] = idx_out.astype(jnp.int32)
    gate_ref[...] = val_out * (1.0 / jnp.sum(val_out, axis=1, keepdims=True))

    chosen = jnp.where(work == -jnp.inf, jnp.where(lane < n_experts, 1.0, 0.0), 0.0)
    count_ref[...] = count_ref[...] + jnp.sum(chosen, axis=0, keepdims=True)


def _cross_and_route(x, wq, kv, wo, g, b, wr_hi, wr_lo, br, alpha, S, n_mem, n_experts):
    T, D = x.shape
    tiles_per_batch = S // ROW_TILE
    row = lambda i: (i, 0)
    fixed = lambda i: (0, 0)
    return pl.pallas_call(
        functools.partial(_cross_kernel, alpha=alpha, n_experts=n_experts),
        grid=(T // ROW_TILE,),
        in_specs=[pl.BlockSpec((ROW_TILE, D), row),
                  pl.BlockSpec((D, D), fixed),
                  pl.BlockSpec((n_mem, 2 * D), lambda i: (i // tiles_per_batch, 0)),
                  pl.BlockSpec((D, D), fixed),
                  pl.BlockSpec((1, D), fixed),
                  pl.BlockSpec((1, D), fixed),
                  pl.BlockSpec((D, LANES), fixed),
                  pl.BlockSpec((D, LANES), fixed),
                  pl.BlockSpec((1, LANES), fixed)],
        out_specs=[pl.BlockSpec((ROW_TILE, D), row),
                   pl.BlockSpec((ROW_TILE * D // LANES, LANES), row),
                   pl.BlockSpec((ROW_TILE, LANES), row),
                   pl.BlockSpec((ROW_TILE, LANES), row),
                   pl.BlockSpec((8, LANES), fixed)],
        out_shape=[jax.ShapeDtypeStruct((T, D), F32),
                   jax.ShapeDtypeStruct((T * D // LANES, LANES), F32),
                   jax.ShapeDtypeStruct((T, LANES), jnp.int32),
                   jax.ShapeDtypeStruct((T, LANES), F32),
                   jax.ShapeDtypeStruct((8, LANES), F32)],
        compiler_params=_params("arbitrary"),
        name="cross_attn_router",
    )(x, wq, kv, wo, g.reshape(1, D), b.reshape(1, D), wr_hi, wr_lo, br)


def _moe_kernel(be_ref, nv_ref, idx_hbm, x_hbm, wgu_ref, bgu_ref, wd_ref, bd_ref, y_hbm,
                wgu_bf, wd_bf, xbuf, ybuf, idx_smem, isem, gsem, ssem, zsem):
    i = pl.program_id(0)
    nb = pl.num_programs(0)
    nv = nv_ref[0]
    D, F = wgu_ref.shape[0], wd_ref.shape[0]
    sub = D // LANES
    tile = xbuf.shape[1] // sub
    valid = i < nv
    cur = i & 1
    nxt = 1 - cur

    def idx_copy(step, s):
        return pltpu.make_async_copy(idx_hbm.at[step], idx_smem.at[s], isem.at[s])

    def start_gathers(par, segment, buf):
        for r in range(tile):
            src = pl.multiple_of(idx_smem[par, segment * tile + r], sub)
            pltpu.make_async_copy(x_hbm.at[pl.ds(src, sub), :],
                                  xbuf.at[buf, pl.ds(r * sub, sub), :],
                                  gsem.at[buf]).start(priority=r % 2)

    def wait_gathers(buf):
        pltpu.make_async_copy(xbuf.at[buf], xbuf.at[buf], gsem.at[buf]).wait()

    def start_scatters(par, buf):
        for r in range(tile):
            dst = pl.multiple_of(idx_smem[par, tile + r], sub)
            pltpu.make_async_copy(ybuf.at[buf, pl.ds(r * sub, sub), :],
                                  y_hbm.at[pl.ds(dst, sub), :],
                                  ssem.at[buf]).start(priority=r % 2)

    def wait_scatters(buf):
        pltpu.make_async_copy(ybuf.at[buf], ybuf.at[buf], ssem.at[buf]).wait()

    @pl.when(i == 0)
    def _():
        idx_copy(0, 0).start()
    idx_copy(i, cur).wait()

    @pl.when(i + 1 < nb)
    def _():
        idx_copy(i + 1, nxt).start()

    @pl.when(i == 0)
    def _():
        ybuf[1] = jnp.zeros(ybuf.shape[1:], ybuf.dtype)
        start_gathers(0, 2, 0)

    @pl.when(i <= nv)
    def _():
        wait_gathers(cur)

    @pl.when((i >= 1) & (i - 2 < nv))
    def _():
        wait_scatters(cur)

    @pl.when(valid & ((i == 0) | (be_ref[i] != be_ref[jnp.maximum(i - 1, 0)])))
    def _():
        def cast_rows(ref, out, r, carry):
            r0 = pl.multiple_of(r * LANES, LANES)
            out[pl.ds(r0, LANES), :] = ref[pl.ds(r0, LANES), :].astype(BF16)
            return carry
        lax.fori_loop(0, D // LANES, functools.partial(cast_rows, wgu_ref, wgu_bf), 0)
        lax.fori_loop(0, F // LANES, functools.partial(cast_rows, wd_ref, wd_bf), 0)

    def compute(par):
        x = jnp.concatenate([xbuf[par, pl.ds(j, tile, stride=sub), :].astype(BF16)
                             for j in range(sub)], axis=1)
        start_gathers(par, 0, 1 - par)
        start_scatters(par, 1 - par)
        acc = None
        for f0 in range(0, F, MOE_FCHUNK):
            f1 = f0 + MOE_FCHUNK
            g = jnp.dot(x, wgu_bf[:, f0:f1], preferred_element_type=F32) + bgu_ref[0, :, f0:f1]
            u = (jnp.dot(x, wgu_bf[:, F + f0:F + f1], preferred_element_type=F32)
                 + bgu_ref[0, :, F + f0:F + f1])
            g = jnp.minimum(g, SWIGLU_LIMIT)
            u = jnp.clip(u, -SWIGLU_LIMIT, SWIGLU_LIMIT)
            glu = g * jax.nn.sigmoid(g * SWIGLU_ALPHA)
            act = ((u + 1.0) * glu).astype(BF16)
            part = jnp.dot(act, wd_bf[f0:f1, :], preferred_element_type=F32)
            acc = part if acc is None else acc + part
        y = acc + bd_ref[0]
        for j in range(sub):
            ybuf[par, pl.ds(j, tile, stride=sub), :] = y[:, j * LANES:(j + 1) * LANES]

    for par in range(2):
        pl.when(valid & (cur == par))(functools.partial(compute, par))
        pl.when((i == nv) & (cur == par))(functools.partial(start_scatters, par, 1 - par))

    @pl.when(jnp.logical_not(valid))
    def _():
        ybuf[cur] = jnp.zeros(ybuf.shape[1:], ybuf.dtype)
        row0 = pl.multiple_of(i * tile * sub, tile * sub)
        zero_fill = pltpu.make_async_copy(ybuf.at[cur], y_hbm.at[pl.ds(row0, tile * sub), :],
                                          zsem)
        zero_fill.start()
        zero_fill.wait()

    @pl.when((i == nb - 1) & (i - 1 < nv))
    def _():
        wait_scatters(nxt)


def _moe_experts(layer, block_expert, n_valid, step_idx, x, wgu, bgu, wd, bd):
    L, E, D, F2 = wgu.shape
    n_blocks = step_idx.shape[0]
    F = F2 // 2
    sub = D // LANES
    assert n_blocks >= 3 and x.shape[1] == LANES

    def weight(i, be, nv):
        return (layer, be[i], 0, 0)

    def bias(i, be, nv):
        return (layer * E + be[i], 0, 0)

    grid_spec = pltpu.PrefetchScalarGridSpec(
        num_scalar_prefetch=2,
        grid=(n_blocks,),
        in_specs=[pl.BlockSpec(memory_space=pl.ANY),
                  pl.BlockSpec(memory_space=pl.ANY),
                  pl.BlockSpec((None, None, D, F2), weight),
                  pl.BlockSpec((1, 1, F2), bias),
                  pl.BlockSpec((None, None, F, D), weight),
                  pl.BlockSpec((1, 1, D), bias)],
        out_specs=pl.BlockSpec(memory_space=pl.ANY),
        scratch_shapes=[pltpu.VMEM((D, F2), BF16), pltpu.VMEM((F, D), BF16),
                        pltpu.VMEM((2, MOE_TILE * sub, LANES), F32),
                        pltpu.VMEM((2, MOE_TILE * sub, LANES), F32),
                        pltpu.SMEM((2, 3 * MOE_TILE), jnp.int32),
                        pltpu.SemaphoreType.DMA((2,)), pltpu.SemaphoreType.DMA((2,)),
                        pltpu.SemaphoreType.DMA((2,)), pltpu.SemaphoreType.DMA(())],
    )
    return pl.pallas_call(
        _moe_kernel,
        grid_spec=grid_spec,
        out_shape=jax.ShapeDtypeStruct(((n_blocks + 1) * MOE_TILE * sub, LANES), F32),
        compiler_params=_params("arbitrary"),
        name="moe_experts",
    )(block_expert, n_valid, step_idx, x, wgu, bgu.reshape(L * E, 1, F2), wd,
      bd.reshape(L * E, 1, D))


def _combine_kernel(x_ref, *refs, alpha):
    ys_refs, (gate_ref, g_ref, b_ref, y_ref) = refs[:TOP_K], refs[TOP_K:]
    tm, D = x_ref.shape
    sub = D // LANES
    gates = gate_ref[...]
    cols = []
    for j in range(sub):
        y = alpha * x_ref[:, j * LANES:(j + 1) * LANES]
        for kk in range(TOP_K):
            y = y + ys_refs[kk][pl.ds(j, tm, stride=sub), :] * gates[:, kk:kk + 1]
        cols.append(y)
    y_ref[...] = _layer_norm(jnp.concatenate(cols, axis=1), g_ref[...], b_ref[...])


def _combine_ln(x, ys, gates, g, b, alpha):
    T, D = x.shape
    tm = ROW_TILE
    sub = D // LANES
    row = lambda i: (i, 0)
    fixed = lambda i: (0, 0)
    ys_specs = [pl.BlockSpec((tm * sub, LANES),
                             functools.partial(lambda k, i: (k * (T // tm) + i, 0), k))
                for k in range(TOP_K)]
    return pl.pallas_call(
        functools.partial(_combine_kernel, alpha=alpha),
        grid=(T // tm,),
        in_specs=[pl.BlockSpec((tm, D), row)] + ys_specs + [
            pl.BlockSpec((tm, LANES), row),
            pl.BlockSpec((1, D), fixed),
            pl.BlockSpec((1, D), fixed)],
        out_specs=pl.BlockSpec((tm, D), row),
        out_shape=jax.ShapeDtypeStruct((T, D), F32),
        compiler_params=_params("parallel"),
        name="moe_combine_ln",
    )(x, *([ys] * TOP_K), gates, g.reshape(1, D), b.reshape(1, D))


def _t5_bucket(dist):
    max_exact = N_BUCKETS // 2
    n = jnp.maximum(dist, 0)
    nf = jnp.maximum(n, 1).astype(F32)
    large = max_exact + (jnp.log(nf / max_exact) / math.log(MAX_DISTANCE / max_exact)
                         * (N_BUCKETS - max_exact)).astype(jnp.int32)
    large = jnp.minimum(large, N_BUCKETS - 1)
    return jnp.where(n < max_exact, n, large)


def _moba_bias_tables(rel_bias, S):
    n = ATT_TILE
    n_heads = rel_bias.shape[1]
    by_dist = rel_bias.T.astype(F32)[:, _t5_bucket(jnp.arange(2 * n + 1))] * LOG2E

    def toeplitz(v):
        return jnp.tile(v, (1, n))[:, :n * (2 * n - 1)].reshape(n_heads, n, 2 * n - 1)[:, :, :n]

    own = toeplitz(by_dist[:, :2 * n])
    prev = toeplitz(jnp.roll(by_dist[:, :2 * n], -n, axis=1))
    far = jnp.broadcast_to(by_dist[:, 2 * n][:, None, None], (n_heads, n, n))
    return jnp.stack([own, prev, far], axis=1)


def _block_average_matrix(S):
    a = np.zeros((LANES, S), np.float32)
    for n in range(S // MOBA_BLOCK):
        a[n, n * MOBA_BLOCK:(n + 1) * MOBA_BLOCK] = 1.0 / MOBA_BLOCK
    return jnp.asarray(a, BF16)


def _dispatch(top_idx, counts, rows_per_token):
    T = top_idx.shape[0]
    TK = T * TOP_K
    n_experts = counts.shape[0]
    padded = ((counts + MOE_TILE - 1) // MOE_TILE) * MOE_TILE
    start = jnp.cumsum(counts) - counts
    pend = jnp.cumsum(padded)
    pstart = pend - padded
    n_blocks = TK // MOE_TILE + n_experts
    n_valid = (pend[-1] // MOE_TILE).astype(jnp.int32)
    blk = jnp.minimum(jnp.arange(n_blocks, dtype=jnp.int32), n_valid - 1)
    block_expert = jnp.minimum(
        jnp.sum((pend[None, :] <= (blk * MOE_TILE)[:, None]).astype(jnp.int32), axis=1),
        n_experts - 1)
    order = jnp.argsort(top_idx.reshape(-1)).astype(jnp.int32)
    slot = jnp.arange(n_blocks * MOE_TILE, dtype=jnp.int32).reshape(n_blocks, MOE_TILE)
    within = slot - pstart[block_expert][:, None]
    real = within < counts[block_expert][:, None]
    src = jnp.clip(within + start[block_expert][:, None], 0, TK - 1)
    flat = order[src.reshape(-1)].reshape(n_blocks, MOE_TILE)
    token = flat // TOP_K
    out_row = jnp.where(real, (flat % TOP_K) * T + token,
                        slot + (TK - (start + counts)[block_expert])[:, None])
    spare = n_blocks * MOE_TILE + jnp.arange(MOE_TILE, dtype=jnp.int32)
    step_idx = jnp.concatenate(
        [jnp.concatenate([token[1:], token[-1:]], axis=0),
         jnp.concatenate([spare[None, :], out_row[:-1]], axis=0),
         token], axis=1)
    return step_idx * rows_per_token, block_expert, n_valid.reshape(1)


def kernel(x, mem, w_in, b_forget, w_mix_out, rel_bias, ln1_g, ln1_b, w_cq, w_ck, w_cv, w_co,
           ln2_g, ln2_b, w_router, b_router, w_gate_up, b_gate_up, w_down, b_down, ln3_g, ln3_b):
    B, S, D = x.shape
    depth = w_in.shape[0]
    n_mem = mem.shape[1]
    n_experts = w_router.shape[2]
    n_heads = D // HEAD_DIM
    n_fox = n_heads // 2
    n_moba = n_heads - n_fox
    fox_w, moba_w = n_fox * HEAD_DIM, n_moba * HEAD_DIM
    T = B * S
    alpha = (2 * depth) ** 0.25
    assert S % ROW_TILE == 0 and S % ATT_TILE == 0 and D % LANES == 0
    assert ATT_TILE == MOBA_BLOCK and MOBA_BLOCK >= MAX_DISTANCE
    assert n_experts <= LANES and (T * TOP_K) % MOE_TILE == 0

    scale = HEAD_DIM ** -0.5 * LOG2E
    c0 = 3 * fox_w
    c1 = c0 + n_fox
    w_fq = w_in[:, :, :fox_w] * scale
    w_fkv = w_in[:, :, fox_w:c0]
    w_fg = jnp.pad(w_in[:, :, c0:c1], ((0, 0), (0, 0), (0, LANES - n_fox)))
    w_mq = w_in[:, :, c1:c1 + moba_w] * scale
    w_mkv = w_in[:, :, c1 + moba_w:]
    w_in_b = jnp.concatenate([w_fq, w_fkv, w_mq, w_mkv, w_fg], axis=2).astype(BF16)

    w_out_b = w_mix_out.astype(BF16)
    w_cq_b = (w_cq * (D // N_CROSS_HEADS) ** -0.5).astype(BF16)
    w_ckv_b = jnp.concatenate([w_ck, w_cv], axis=2).astype(BF16)
    w_co_b = w_co.astype(BF16)
    w_r = jnp.pad(w_router, ((0, 0), (0, 0), (0, LANES - n_experts)))
    w_r_hi = w_r.astype(BF16)
    w_r_lo = (w_r - w_r_hi.astype(F32)).astype(BF16)
    b_r = jnp.pad(b_router, ((0, 0), (0, LANES - n_experts))).reshape(depth, 1, LANES)

    bias_tiles = _moba_bias_tables(rel_bias, S)
    blk_avg = _block_average_matrix(S)
    mem2d = mem.reshape(B * n_mem, D)
    xt = x.reshape(T, D)

    for l in range(depth):
        qkv, f_logit = _in_proj(xt, w_in_b[l])
        c_aug = _fox_decay(f_logit, b_forget[l], S, n_fox)
        o_f = _fox_attention(qkv, c_aug, B, S, n_fox)
        o_m = _moba_attention(qkv, blk_avg, bias_tiles, B, S, n_fox, n_moba)
        xt = _mix_out(o_f, o_m, xt, w_out_b[l], ln1_g[l], ln1_b[l], alpha)

        kv = _mem_proj(mem2d, w_ckv_b[l])
        xt, xt_tiles, top_idx, gates, counts = _cross_and_route(
            xt, w_cq_b[l], kv, w_co_b[l], ln2_g[l], ln2_b[l], w_r_hi[l], w_r_lo[l], b_r[l],
            alpha, S, n_mem, n_experts)

        step_idx, block_expert, n_valid = _dispatch(
            top_idx[:, :TOP_K], counts[0, :n_experts].astype(jnp.int32), D // LANES)
        ys = _moe_experts(l, block_expert, n_valid, step_idx, xt_tiles, w_gate_up, b_gate_up,
                          w_down, b_down)
        xt = _combine_ln(xt, ys, gates, ln3_g[l], ln3_b[l], alpha)

    return xt.reshape(B, S, D)
```

```python
import functools
import math

import jax
import jax.numpy as jnp
import numpy as np
from jax import lax
from jax.experimental import pallas as pl
from jax.experimental.pallas import tpu as pltpu

F32 = jnp.float32
BF16 = jnp.bfloat16

HEAD_DIM = 64
N_BUCKETS = 32
MAX_DISTANCE = 128
MOBA_BLOCK = 256
MOBA_TOPK = 3
N_CROSS_HEADS = 4
TOP_K = 4
SWIGLU_LIMIT = 7.0
SWIGLU_ALPHA = 1.702
LN_EPS = 1e-5

LANES = 128
VMEM_LIMIT = 56 * 1024 * 1024

ROW_TILE = 512
ATT_TILE = MOBA_BLOCK
MOE_TILE = 512
MOE_FCHUNK = 512
MASKED = -1e30
LOG2E = math.log2(math.e)


def _params(*sem):
    return pltpu.CompilerParams(dimension_semantics=sem, vmem_limit_bytes=VMEM_LIMIT)


def _nt_dot(a, b):
    return lax.dot_general(a, b, (((1,), (1,)), ((), ())), preferred_element_type=F32)


def _layer_norm(y, g, b):
    mu = jnp.mean(y, axis=-1, keepdims=True)
    d = y - mu
    var = jnp.mean(d * d, axis=-1, keepdims=True)
    return d * lax.rsqrt(var + LN_EPS) * g + b


def _in_proj_kernel(x_ref, w_ref, qkv_ref, f_ref):
    xb = x_ref[...].astype(BF16)
    n_qkv = qkv_ref.shape[1]
    for n0 in range(0, n_qkv, 512):
        qkv_ref[:, n0:n0 + 512] = jnp.dot(
            xb, w_ref[:, n0:n0 + 512], preferred_element_type=F32).astype(BF16)
    f_ref[...] = jnp.dot(xb, w_ref[:, n_qkv:], preferred_element_type=F32)


def _in_proj(x, w):
    T, D = x.shape
    n_all = w.shape[1]
    n_qkv = n_all - LANES
    return pl.pallas_call(
        _in_proj_kernel,
        grid=(T // ROW_TILE,),
        in_specs=[pl.BlockSpec((ROW_TILE, D), lambda i: (i, 0)),
                  pl.BlockSpec((D, n_all), lambda i: (0, 0))],
        out_specs=[pl.BlockSpec((ROW_TILE, n_qkv), lambda i: (i, 0)),
                   pl.BlockSpec((ROW_TILE, LANES), lambda i: (i, 0))],
        out_shape=[jax.ShapeDtypeStruct((T, n_qkv), BF16),
                   jax.ShapeDtypeStruct((T, LANES), F32)],
        compiler_params=_params("parallel"),
        name="in_proj",
    )(x, w)


def _split3(x):
    p1 = x.astype(BF16)
    r1 = x - p1.astype(F32)
    p2 = r1.astype(BF16)
    p3 = (r1 - p2.astype(F32)).astype(BF16)
    return p1, p2, p3


def _decay_kernel(f_ref, b_ref, tri_ref, place_ref, c_ref):
    S = f_ref.shape[0]
    blk = tri_ref.shape[0]
    carry = jnp.zeros((1, LANES), F32)
    for j in range(S // blk):
        z = f_ref[j * blk:(j + 1) * blk, :] + b_ref[...]
        ls = jnp.minimum(z, 0.0) - jnp.log1p(jnp.exp(-jnp.abs(z)))
        c = carry
        for piece in _split3(ls):
            c = c + jnp.dot(tri_ref[...], piece, preferred_element_type=F32)
        carry = c[blk - 1:blk, :]
        aug = None
        for i, piece in enumerate(_split3(c * LOG2E)):
            t = jnp.dot(piece, place_ref[i], preferred_element_type=F32)
            aug = t if aug is None else aug + t
        c_ref[j * blk:(j + 1) * blk, :] = aug.astype(BF16)


def _fox_decay(f_logit, b_forget, S, n_fox):
    T = f_logit.shape[0]
    pairs = n_fox * HEAD_DIM // LANES
    blk = ATT_TILE
    tri = jnp.asarray(np.tril(np.ones((blk, blk), np.float32)), BF16)
    place = np.zeros((3, LANES, pairs * LANES), np.float32)
    for h in range(n_fox):
        for i in range(3):
            place[i, h, (h // 2) * LANES + 3 * (h % 2) + i] = -1.0
    b_pad = jnp.pad(b_forget, (0, LANES - n_fox)).reshape(1, LANES)
    return pl.pallas_call(
        _decay_kernel,
        grid=(T // S,),
        in_specs=[pl.BlockSpec((S, LANES), lambda b: (b, 0)),
                  pl.BlockSpec((1, LANES), lambda b: (0, 0)),
                  pl.BlockSpec((blk, blk), lambda b: (0, 0)),
                  pl.BlockSpec((3, LANES, pairs * LANES), lambda b: (0, 0, 0))],
        out_specs=pl.BlockSpec((S, pairs * LANES), lambda b: (b, 0)),
        out_shape=jax.ShapeDtypeStruct((T, pairs * LANES), BF16),
        compiler_params=_params("parallel"),
        name="fox_decay",
    )(f_logit, b_pad, tri, jnp.asarray(place, BF16))


VT_ROWS = HEAD_DIM + 16


def _online_update_t(s, vt, m, acc):
    m_new = jnp.maximum(m, jnp.max(s, axis=0, keepdims=True))
    p = jnp.exp2(s - m_new)
    acc_new = jnp.exp2(m - m_new) * acc + jnp.dot(vt, p.astype(BF16), preferred_element_type=F32)
    return m_new, acc_new


def _init_state(tq):
    return jnp.full((1, tq), MASKED, F32), jnp.zeros((VT_ROWS, tq), F32)


def _split_heads(q):
    lane = lax.broadcasted_iota(jnp.int32, q.shape, 1)
    zero = jnp.zeros((), q.dtype)
    return jnp.where(lane < HEAD_DIM, q, zero), jnp.where(lane < HEAD_DIM, zero, q)


def _causal_mask_t(tq):
    key = lax.broadcasted_iota(jnp.int32, (tq, tq), 0)
    qry = lax.broadcasted_iota(jnp.int32, (tq, tq), 1)
    return key <= qry


def _transpose_values(v_ref, vt_scr):
    tq = ATT_TILE
    S = v_ref.shape[0]
    for hh in range(2):
        vt_scr[hh * VT_ROWS + HEAD_DIM:(hh + 1) * VT_ROWS, :] = jnp.ones(
            (VT_ROWS - HEAD_DIM, S), BF16)
    for j in range(S // tq):
        vt = v_ref[j * tq:(j + 1) * tq, :].astype(F32).T.astype(BF16)
        for hh in range(2):
            vt_scr[hh * VT_ROWS:hh * VT_ROWS + HEAD_DIM, j * tq:(j + 1) * tq] = (
                vt[hh * HEAD_DIM:(hh + 1) * HEAD_DIM, :])


def _store_heads(o_ref, q0, state):
    tq = ATT_TILE
    o_t = jnp.concatenate(
        [acc[:HEAD_DIM] * (1.0 / acc[HEAD_DIM:HEAD_DIM + 1]) for _, acc in state], axis=0)
    o_ref[pl.ds(q0, tq), :] = o_t.T.astype(o_ref.dtype)


def _to_weights(x):
    return x.astype(F32).T.astype(BF16)


def _run_causal_tiles(n_blocks, score_tile, vt_scr, o_ref):
    tq = ATT_TILE
    tiles = [(qi, kb) for qi in range(n_blocks) for kb in [qi] + list(range(qi))]
    cur = score_tile(*tiles[0])
    state = None
    for i, (qi, kb) in enumerate(tiles):
        nxt = score_tile(*tiles[i + 1]) if i + 1 < len(tiles) else None
        if kb == qi:
            state = (_init_state(tq), _init_state(tq))
        new_state = []
        for hh in range(2):
            vt = vt_scr[hh * VT_ROWS:(hh + 1) * VT_ROWS, kb * tq:(kb + 1) * tq]
            new_state.append(_online_update_t(cur[hh], vt, *state[hh]))
        state = tuple(new_state)
        if kb == max(qi - 1, 0):
            _store_heads(o_ref, qi * tq, state)
        cur = nxt


def _fox_kernel(q_ref, k_ref, v_ref, c_ref, o_ref, vt_scr):
    tq = ATT_TILE
    S = q_ref.shape[0]
    causal = _causal_mask_t(tq)
    lane = lax.broadcasted_iota(jnp.int32, (tq, LANES), 1)
    ones = [jnp.where((lane >= 3 * hh) & (lane < 3 * hh + 3), 1.0, 0.0).astype(BF16)
            for hh in range(2)]
    _transpose_values(v_ref, vt_scr)
    q_weights = {}

    def score_tile(qi, kb):
        if qi not in q_weights:
            qh = _split_heads(q_ref[qi * tq:(qi + 1) * tq, :])
            q_weights.clear()
            q_weights[qi] = [_to_weights(jnp.concatenate([qh[hh], ones[hh]], axis=1))
                             for hh in range(2)]
        rows = slice(kb * tq, (kb + 1) * tq)
        ka = jnp.concatenate([k_ref[rows, :], c_ref[rows, :]], axis=1)
        out = []
        for hh in range(2):
            s = jnp.dot(ka, q_weights[qi][hh], preferred_element_type=F32)
            out.append(jnp.where(causal, s, MASKED) if kb == qi else s)
        return tuple(out)

    _run_causal_tiles(S // tq, score_tile, vt_scr, o_ref)


def _fox_attention(qkv, c_aug, B, S, n_fox):
    T = B * S
    pairs = n_fox * HEAD_DIM // LANES
    kb = pairs
    vb = 2 * pairs
    return pl.pallas_call(
        _fox_kernel,
        grid=(B, pairs),
        in_specs=[pl.BlockSpec((S, LANES), lambda b, p: (b, p)),
                  pl.BlockSpec((S, LANES), lambda b, p: (b, kb + p)),
                  pl.BlockSpec((S, LANES), lambda b, p: (b, vb + p)),
                  pl.BlockSpec((S, LANES), lambda b, p: (b, p))],
        out_specs=pl.BlockSpec((S, LANES), lambda b, p: (b, p)),
        out_shape=jax.ShapeDtypeStruct((T, pairs * LANES), BF16),
        scratch_shapes=[pltpu.VMEM((2 * VT_ROWS, S), BF16)],
        compiler_params=_params("parallel", "parallel"),
        name="fox_attention",
    )(qkv, qkv, qkv, c_aug)


def _moba_kernel(q_ref, k_ref, v_ref, a_ref, bias_ref, o_ref, vt_scr):
    tq = ATT_TILE
    S = q_ref.shape[0]
    rows = -(-(S // tq) // 8) * 8
    causal = _causal_mask_t(tq)
    blk_row = lax.broadcasted_iota(jnp.int32, (rows, tq), 0)
    _transpose_values(v_ref, vt_scr)

    kmean = jnp.dot(a_ref[...], k_ref[...], preferred_element_type=F32)
    kmean_hi = kmean.astype(BF16)
    kmean_lo = (kmean - kmean_hi.astype(F32)).astype(BF16)
    per_block = {}

    def prepare(qi):
        qh = _split_heads(q_ref[qi * tq:(qi + 1) * tq, :])
        qw = [_to_weights(qh[hh]) for hh in range(2)]
        offsets = []
        for hh in range(2):
            gate = (jnp.dot(kmean_hi, qw[hh], preferred_element_type=F32)
                    + jnp.dot(kmean_lo, qw[hh], preferred_element_type=F32))[:rows, :]
            rank = jnp.zeros((rows, tq), jnp.int32)
            for mb in range(qi):
                g_m = gate[mb:mb + 1, :]
                tie = jnp.where(blk_row > mb, 1, 0)
                rank = rank + jnp.where(g_m > gate, 1, jnp.where(g_m == gate, tie, 0))
            offsets.append(jnp.where(rank < MOBA_TOPK, 0.0, MASKED))
        return qw, offsets

    def score_tile(qi, kb):
        if qi not in per_block:
            per_block.clear()
            per_block[qi] = prepare(qi)
        qw, offsets = per_block[qi]
        k = k_ref[kb * tq:(kb + 1) * tq, :]
        out = []
        for hh in range(2):
            s = jnp.dot(k, qw[hh], preferred_element_type=F32)
            if kb == qi:
                s = jnp.where(causal, s + bias_ref[hh, 0], MASKED)
            elif kb == qi - 1:
                s = s + bias_ref[hh, 1] + offsets[hh][kb:kb + 1, :]
            else:
                s = s + (bias_ref[hh, 2, 0:1, :] + offsets[hh][kb:kb + 1, :])
            out.append(s)
        return tuple(out)

    _run_causal_tiles(S // tq, score_tile, vt_scr, o_ref)


def _moba_attention(qkv, blk_avg, bias_tiles, B, S, n_fox, n_moba):
    T = B * S
    pairs = n_moba * HEAD_DIM // LANES
    qb = 3 * n_fox * HEAD_DIM // LANES
    kb = qb + pairs
    vb = qb + 2 * pairs
    tq = ATT_TILE
    return pl.pallas_call(
        _moba_kernel,
        grid=(B, pairs),
        in_specs=[pl.BlockSpec((S, LANES), lambda b, p: (b, qb + p)),
                  pl.BlockSpec((S, LANES), lambda b, p: (b, kb + p)),
                  pl.BlockSpec((S, LANES), lambda b, p: (b, vb + p)),
                  pl.BlockSpec((LANES, S), lambda b, p: (0, 0)),
                  pl.BlockSpec((2, 3, tq, tq), lambda b, p: (p, 0, 0, 0))],
        out_specs=pl.BlockSpec((S, LANES), lambda b, p: (b, p)),
        out_shape=jax.ShapeDtypeStruct((T, pairs * LANES), BF16),
        scratch_shapes=[pltpu.VMEM((2 * VT_ROWS, S), BF16)],
        compiler_params=_params("parallel", "parallel"),
        name="moba_attention",
    )(qkv, qkv, qkv, blk_avg, bias_tiles)


def _mix_out_kernel(of_ref, om_ref, x_ref, w_ref, g_ref, b_ref, y_ref, *, alpha):
    half = of_ref.shape[1]
    h = (jnp.dot(of_ref[...], w_ref[:half, :], preferred_element_type=F32)
         + jnp.dot(om_ref[...], w_ref[half:, :], preferred_element_type=F32))
    y_ref[...] = _layer_norm(alpha * x_ref[...] + h, g_ref[...], b_ref[...])


def _mix_out(o_f, o_m, x, w, g, b, alpha):
    T, D = x.shape
    half = o_f.shape[1]
    row = lambda i: (i, 0)
    fixed = lambda i: (0, 0)
    return pl.pallas_call(
        functools.partial(_mix_out_kernel, alpha=alpha),
        grid=(T // ROW_TILE,),
        in_specs=[pl.BlockSpec((ROW_TILE, half), row),
                  pl.BlockSpec((ROW_TILE, half), row),
                  pl.BlockSpec((ROW_TILE, D), row),
                  pl.BlockSpec((D, D), fixed),
                  pl.BlockSpec((1, D), fixed),
                  pl.BlockSpec((1, D), fixed)],
        out_specs=pl.BlockSpec((ROW_TILE, D), row),
        out_shape=jax.ShapeDtypeStruct((T, D), F32),
        compiler_params=_params("parallel"),
        name="mix_out_ln",
    )(o_f, o_m, x, w, g.reshape(1, D), b.reshape(1, D))


def _mem_proj_kernel(m_ref, w_ref, kv_ref):
    mb = m_ref[...].astype(BF16)
    n = kv_ref.shape[1]
    for n0 in range(0, n, 512):
        kv_ref[:, n0:n0 + 512] = jnp.dot(
            mb, w_ref[:, n0:n0 + 512], preferred_element_type=F32).astype(BF16)


def _mem_proj(mem2d, w_kv):
    M, D = mem2d.shape
    n = w_kv.shape[1]
    tm = min(ROW_TILE, M)
    return pl.pallas_call(
        _mem_proj_kernel,
        grid=(M // tm,),
        in_specs=[pl.BlockSpec((tm, D), lambda i: (i, 0)),
                  pl.BlockSpec((D, n), lambda i: (0, 0))],
        out_specs=pl.BlockSpec((tm, n), lambda i: (i, 0)),
        out_shape=jax.ShapeDtypeStruct((M, n), BF16),
        compiler_params=_params("parallel"),
        name="mem_proj",
    )(mem2d, w_kv)


def _cross_kernel(x_ref, wq_ref, kv_ref, wo_ref, g_ref, b_ref, wr_hi_ref, wr_lo_ref, br_ref,
                  y_ref, yt_ref, idx_ref, gate_ref, count_ref, *, alpha, n_experts):
    D = x_ref.shape[1]

    @pl.when(pl.program_id(0) == 0)
    def _():
        count_ref[...] = jnp.zeros(count_ref.shape, F32)

    dh = D // N_CROSS_HEADS
    x = x_ref[...]
    q = jnp.dot(x.astype(BF16), wq_ref[...], preferred_element_type=F32).astype(BF16)
    heads = []
    for h in range(N_CROSS_HEADS):
        k_h = kv_ref[:, h * dh:(h + 1) * dh]
        v_h = kv_ref[:, D + h * dh:D + (h + 1) * dh]
        s = _nt_dot(q[:, h * dh:(h + 1) * dh], k_h)
        p = jnp.exp(s - jnp.max(s, axis=1, keepdims=True))
        p = p * (1.0 / jnp.sum(p, axis=1, keepdims=True))
        heads.append(jnp.dot(p.astype(BF16), v_h, preferred_element_type=F32).astype(BF16))
    o = jnp.concatenate(heads, axis=1)
    hproj = jnp.dot(o, wo_ref[...], preferred_element_type=F32)
    y = _layer_norm(alpha * x + hproj, g_ref[...], b_ref[...])
    y_ref[...] = y
    for j in range(D // LANES):
        yt_ref[pl.ds(j, y.shape[0], stride=D // LANES), :] = y[:, j * LANES:(j + 1) * LANES]
    y_hi = y.astype(BF16)

    y_lo = (y - y_hi.astype(F32)).astype(BF16)
    logits = (jnp.dot(y_hi, wr_hi_ref[...], preferred_element_type=F32)
              + jnp.dot(y_lo, wr_hi_ref[...], preferred_element_type=F32)
              + jnp.dot(y_hi, wr_lo_ref[...], preferred_element_type=F32)
              + br_ref[...])
    lane = lax.broadcasted_iota(jnp.int32, logits.shape, 1)
    lane_f = lane.astype(F32)
    work = jnp.where(lane < n_experts, logits, -jnp.inf)
    idx_out = jnp.zeros(logits.shape, F32)
    val_out = jnp.zeros(logits.shape, F32)
    top = None
    for kk in range(TOP_K):
        best = jnp.max(work, axis=1, keepdims=True)
        arg = jnp.min(jnp.where(work == best, lane_f, float(LANES)), axis=1, keepdims=True)
        work = jnp.where(lane_f == arg, -jnp.inf, work)
        if top is None:
            top = best
        idx_out = jnp.where(lane == kk, arg, idx_out)
        val_out = jnp.where(lane == kk, jnp.exp(best - top), val_out)
    idx_ref[...] = idx_out.astype(jnp.int32)
    gate_ref[...] = val_out * (1.0 / jnp.sum(val_out, axis=1, keepdims=True))

    chosen = jnp.where(work == -jnp.inf, jnp.where(lane < n_experts, 1.0, 0.0), 0.0)
    count_ref[...] = count_ref[...] + jnp.sum(chosen, axis=0, keepdims=True)


def _cross_and_route(x, wq, kv, wo, g, b, wr_hi, wr_lo, br, alpha, S, n_mem, n_experts):
    T, D = x.shape
    tiles_per_batch = S // ROW_TILE
    row = lambda i: (i, 0)
    fixed = lambda i: (0, 0)
    return pl.pallas_call(
        functools.partial(_cross_kernel, alpha=alpha, n_experts=n_experts),
        grid=(T // ROW_TILE,),
        in_specs=[pl.BlockSpec((ROW_TILE, D), row),
                  pl.BlockSpec((D, D), fixed),
                  pl.BlockSpec((n_mem, 2 * D), lambda i: (i // tiles_per_batch, 0)),
                  pl.BlockSpec((D, D), fixed),
                  pl.BlockSpec((1, D), fixed),
                  pl.BlockSpec((1, D), fixed),
                  pl.BlockSpec((D, LANES), fixed),
                  pl.BlockSpec((D, LANES), fixed),
                  pl.BlockSpec((1, LANES), fixed)],
        out_specs=[pl.BlockSpec((ROW_TILE, D), row),
                   pl.BlockSpec((ROW_TILE * D // LANES, LANES), row),
                   pl.BlockSpec((ROW_TILE, LANES), row),
                   pl.BlockSpec((ROW_TILE, LANES), row),
                   pl.BlockSpec((8, LANES), fixed)],
        out_shape=[jax.ShapeDtypeStruct((T, D), F32),
                   jax.ShapeDtypeStruct((T * D // LANES, LANES), F32),
                   jax.ShapeDtypeStruct((T, LANES), jnp.int32),
                   jax.ShapeDtypeStruct((T, LANES), F32),
                   jax.ShapeDtypeStruct((8, LANES), F32)],
        compiler_params=_params("arbitrary"),
        name="cross_attn_router",
    )(x, wq, kv, wo, g.reshape(1, D), b.reshape(1, D), wr_hi, wr_lo, br)


def _moe_kernel(be_ref, nv_ref, idx_hbm, x_hbm, wgu_ref, bgu_ref, wd_ref, bd_ref, y_hbm,
                wgu_bf, wd_bf, xbuf, ybuf, idx_smem, isem, gsem, ssem, zsem, fence_sem):
    i = pl.program_id(0)
    nb = pl.num_programs(0)
    nv = nv_ref[0]
    D, F = wgu_ref.shape[0], wd_ref.shape[0]
    sub = D // LANES
    tile = xbuf.shape[1] // sub
    valid = i < nv
    cur = i & 1
    nxt = 1 - cur

    def idx_copy(step, s):
        return pltpu.make_async_copy(idx_hbm.at[step], idx_smem.at[s], isem.at[s])

    def start_gathers(par, segment, buf):
        for r in range(tile):
            src = pl.multiple_of(idx_smem[par, segment * tile + r], sub)
            pltpu.make_async_copy(x_hbm.at[pl.ds(src, sub), :],
                                  xbuf.at[buf, pl.ds(r * sub, sub), :],
                                  gsem.at[buf]).start(priority=r % 2)

    def wait_gathers(buf):
        pltpu.make_async_copy(xbuf.at[buf], xbuf.at[buf], gsem.at[buf]).wait()

    def start_scatters(par, buf):
        for r in range(tile):
            dst = pl.multiple_of(idx_smem[par, tile + r], sub)
            pltpu.make_async_copy(ybuf.at[buf, pl.ds(r * sub, sub), :],
                                  y_hbm.at[pl.ds(dst, sub), :],
                                  ssem.at[buf]).start(priority=r % 2)

    def wait_scatters(buf):
        pltpu.make_async_copy(ybuf.at[buf], ybuf.at[buf], ssem.at[buf]).wait()

    @pl.when(i == 0)
    def _():
        idx_copy(0, 0).start()
    idx_copy(i, cur).wait()

    @pl.when(i + 1 < nb)
    def _():
        idx_copy(i + 1, nxt).start()

    @pl.when(i == 0)
    def _():
        ybuf[1] = jnp.zeros(ybuf.shape[1:], ybuf.dtype)
        start_gathers(0, 2, 0)

    @pl.when(i <= nv)
    def _():
        wait_gathers(cur)

    @pl.when((i >= 1) & (i - 2 < nv))
    def _():
        wait_scatters(cur)

    @pl.when(valid & ((i == 0) | (be_ref[i] != be_ref[jnp.maximum(i - 1, 0)])))
    def _():
        def cast_rows(ref, out, r, carry):
            r0 = pl.multiple_of(r * LANES, LANES)
            out[pl.ds(r0, LANES), :] = ref[pl.ds(r0, LANES), :].astype(BF16)
            return carry
        lax.fori_loop(0, D // LANES, functools.partial(cast_rows, wgu_ref, wgu_bf), 0)
        lax.fori_loop(0, F // LANES, functools.partial(cast_rows, wd_ref, wd_bf), 0)

    def compute(par):
        x = jnp.concatenate([xbuf[par, pl.ds(j, tile, stride=sub), :].astype(BF16)
                             for j in range(sub)], axis=1)
        start_gathers(par, 0, 1 - par)
        acc = None
        for f0 in range(0, F, MOE_FCHUNK):
            f1 = f0 + MOE_FCHUNK
            if f0 == MOE_FCHUNK:
                pl.semaphore_signal(fence_sem, 1)
                pl.semaphore_wait(fence_sem, 1)
                start_scatters(par, 1 - par)
            g = jnp.dot(x, wgu_bf[:, f0:f1], preferred_element_type=F32) + bgu_ref[0, :, f0:f1]
            u = (jnp.dot(x, wgu_bf[:, F + f0:F + f1], preferred_element_type=F32)
                 + bgu_ref[0, :, F + f0:F + f1])
            g = jnp.minimum(g, SWIGLU_LIMIT)
            u = jnp.clip(u, -SWIGLU_LIMIT, SWIGLU_LIMIT)
            glu = g * jax.nn.sigmoid(g * SWIGLU_ALPHA)
            act = ((u + 1.0) * glu).astype(BF16)
            part = jnp.dot(act, wd_bf[f0:f1, :], preferred_element_type=F32)
            acc = part if acc is None else acc + part
        y = acc + bd_ref[0]
        for j in range(sub):
            ybuf[par, pl.ds(j, tile, stride=sub), :] = y[:, j * LANES:(j + 1) * LANES]

    for par in range(2):
        pl.when(valid & (cur == par))(functools.partial(compute, par))
        pl.when((i == nv) & (cur == par))(functools.partial(start_scatters, par, 1 - par))

    @pl.when(jnp.logical_not(valid))
    def _():
        ybuf[cur] = jnp.zeros(ybuf.shape[1:], ybuf.dtype)
        row0 = pl.multiple_of(i * tile * sub, tile * sub)
        zero_fill = pltpu.make_async_copy(ybuf.at[cur], y_hbm.at[pl.ds(row0, tile * sub), :],
                                          zsem)
        zero_fill.start()
        zero_fill.wait()

    @pl.when((i == nb - 1) & (i - 1 < nv))
    def _():
        wait_scatters(nxt)


def _moe_experts(layer, block_expert, n_valid, step_idx, x, wgu, bgu, wd, bd):
    L, E, D, F2 = wgu.shape
    n_blocks = step_idx.shape[0]
    F = F2 // 2
    sub = D // LANES
    assert n_blocks >= 3 and x.shape[1] == LANES and F >= 2 * MOE_FCHUNK

    def weight(i, be, nv):
        return (layer, be[i], 0, 0)

    def bias(i, be, nv):
        return (layer * E + be[i], 0, 0)

    grid_spec = pltpu.PrefetchScalarGridSpec(
        num_scalar_prefetch=2,
        grid=(n_blocks,),
        in_specs=[pl.BlockSpec(memory_space=pl.ANY),
                  pl.BlockSpec(memory_space=pl.ANY),
                  pl.BlockSpec((None, None, D, F2), weight),
                  pl.BlockSpec((1, 1, F2), bias),
                  pl.BlockSpec((None, None, F, D), weight),
                  pl.BlockSpec((1, 1, D), bias)],
        out_specs=pl.BlockSpec(memory_space=pl.ANY),
        scratch_shapes=[pltpu.VMEM((D, F2), BF16), pltpu.VMEM((F, D), BF16),
                        pltpu.VMEM((2, MOE_TILE * sub, LANES), F32),
                        pltpu.VMEM((2, MOE_TILE * sub, LANES), F32),
                        pltpu.SMEM((2, 3 * MOE_TILE), jnp.int32),
                        pltpu.SemaphoreType.DMA((2,)), pltpu.SemaphoreType.DMA((2,)),
                        pltpu.SemaphoreType.DMA((2,)), pltpu.SemaphoreType.DMA(()),
                        pltpu.SemaphoreType.REGULAR],
    )
    return pl.pallas_call(
        _moe_kernel,
        grid_spec=grid_spec,
        out_shape=jax.ShapeDtypeStruct(((n_blocks + 1) * MOE_TILE * sub, LANES), F32),
        compiler_params=_params("arbitrary"),
        name="moe_experts",
    )(block_expert, n_valid, step_idx, x, wgu, bgu.reshape(L * E, 1, F2), wd,
      bd.reshape(L * E, 1, D))


def _combine_kernel(x_ref, *refs, alpha):
    ys_refs, (gate_ref, g_ref, b_ref, y_ref) = refs[:TOP_K], refs[TOP_K:]
    tm, D = x_ref.shape
    sub = D // LANES
    gates = gate_ref[...]
    cols = []
    for j in range(sub):
        y = alpha * x_ref[:, j * LANES:(j + 1) * LANES]
        for kk in range(TOP_K):
            y = y + ys_refs[kk][pl.ds(j, tm, stride=sub), :] * gates[:, kk:kk + 1]
        cols.append(y)
    y_ref[...] = _layer_norm(jnp.concatenate(cols, axis=1), g_ref[...], b_ref[...])


def _combine_ln(x, ys, gates, g, b, alpha):
    T, D = x.shape
    tm = ROW_TILE
    sub = D // LANES
    row = lambda i: (i, 0)
    fixed = lambda i: (0, 0)
    ys_specs = [pl.BlockSpec((tm * sub, LANES),
                             functools.partial(lambda k, i: (k * (T // tm) + i, 0), k))
                for k in range(TOP_K)]
    return pl.pallas_call(
        functools.partial(_combine_kernel, alpha=alpha),
        grid=(T // tm,),
        in_specs=[pl.BlockSpec((tm, D), row)] + ys_specs + [
            pl.BlockSpec((tm, LANES), row),
            pl.BlockSpec((1, D), fixed),
            pl.BlockSpec((1, D), fixed)],
        out_specs=pl.BlockSpec((tm, D), row),
        out_shape=jax.ShapeDtypeStruct((T, D), F32),
        compiler_params=_params("parallel"),
        name="moe_combine_ln",
    )(x, *([ys] * TOP_K), gates, g.reshape(1, D), b.reshape(1, D))


def _t5_bucket(dist):
    max_exact = N_BUCKETS // 2
    n = jnp.maximum(dist, 0)
    nf = jnp.maximum(n, 1).astype(F32)
    large = max_exact + (jnp.log(nf / max_exact) / math.log(MAX_DISTANCE / max_exact)
                         * (N_BUCKETS - max_exact)).astype(jnp.int32)
    large = jnp.minimum(large, N_BUCKETS - 1)
    return jnp.where(n < max_exact, n, large)


def _moba_bias_tables(rel_bias, S):
    n = ATT_TILE
    n_heads = rel_bias.shape[1]
    by_dist = rel_bias.T.astype(F32)[:, _t5_bucket(jnp.arange(2 * n + 1))] * LOG2E

    def toeplitz(v):
        return jnp.tile(v, (1, n))[:, :n * (2 * n - 1)].reshape(n_heads, n, 2 * n - 1)[:, :, :n]

    own = toeplitz(by_dist[:, :2 * n])
    prev = toeplitz(jnp.roll(by_dist[:, :2 * n], -n, axis=1))
    far = jnp.broadcast_to(by_dist[:, 2 * n][:, None, None], (n_heads, n, n))
    return jnp.stack([own, prev, far], axis=1)


def _block_average_matrix(S):
    a = np.zeros((LANES, S), np.float32)
    for n in range(S // MOBA_BLOCK):
        a[n, n * MOBA_BLOCK:(n + 1) * MOBA_BLOCK] = 1.0 / MOBA_BLOCK
    return jnp.asarray(a, BF16)


def _dispatch(top_idx, counts, rows_per_token):
    T = top_idx.shape[0]
    TK = T * TOP_K
    n_experts = counts.shape[0]
    padded = ((counts + MOE_TILE - 1) // MOE_TILE) * MOE_TILE
    start = jnp.cumsum(counts) - counts
    pend = jnp.cumsum(padded)
    pstart = pend - padded
    n_blocks = TK // MOE_TILE + n_experts
    n_valid = (pend[-1] // MOE_TILE).astype(jnp.int32)
    blk = jnp.minimum(jnp.arange(n_blocks, dtype=jnp.int32), n_valid - 1)
    block_expert = jnp.minimum(
        jnp.sum((pend[None, :] <= (blk * MOE_TILE)[:, None]).astype(jnp.int32), axis=1),
        n_experts - 1)
    order = jnp.argsort(top_idx.reshape(-1)).astype(jnp.int32)
    slot = jnp.arange(n_blocks * MOE_TILE, dtype=jnp.int32).reshape(n_blocks, MOE_TILE)
    within = slot - pstart[block_expert][:, None]
    real = within < counts[block_expert][:, None]
    src = jnp.clip(within + start[block_expert][:, None], 0, TK - 1)
    flat = order[src.reshape(-1)].reshape(n_blocks, MOE_TILE)
    token = flat // TOP_K
    out_row = jnp.where(real, (flat % TOP_K) * T + token,
                        slot + (TK - (start + counts)[block_expert])[:, None])
    spare = n_blocks * MOE_TILE + jnp.arange(MOE_TILE, dtype=jnp.int32)
    step_idx = jnp.concatenate(
        [jnp.concatenate([token[1:], token[-1:]], axis=0),
         jnp.concatenate([spare[None, :], out_row[:-1]], axis=0),
         token], axis=1)
    return step_idx * rows_per_token, block_expert, n_valid.reshape(1)


def kernel(x, mem, w_in, b_forget, w_mix_out, rel_bias, ln1_g, ln1_b, w_cq, w_ck, w_cv, w_co,
           ln2_g, ln2_b, w_router, b_router, w_gate_up, b_gate_up, w_down, b_down, ln3_g, ln3_b):
    B, S, D = x.shape
    depth = w_in.shape[0]
    n_mem = mem.shape[1]
    n_experts = w_router.shape[2]
    n_heads = D // HEAD_DIM
    n_fox = n_heads // 2
    n_moba = n_heads - n_fox
    fox_w, moba_w = n_fox * HEAD_DIM, n_moba * HEAD_DIM
    T = B * S
    alpha = (2 * depth) ** 0.25
    assert S % ROW_TILE == 0 and S % ATT_TILE == 0 and D % LANES == 0
    assert ATT_TILE == MOBA_BLOCK and MOBA_BLOCK >= MAX_DISTANCE
    assert n_experts <= LANES and (T * TOP_K) % MOE_TILE == 0

    scale = HEAD_DIM ** -0.5 * LOG2E
    c0 = 3 * fox_w
    c1 = c0 + n_fox
    w_fq = w_in[:, :, :fox_w] * scale
    w_fkv = w_in[:, :, fox_w:c0]
    w_fg = jnp.pad(w_in[:, :, c0:c1], ((0, 0), (0, 0), (0, LANES - n_fox)))
    w_mq = w_in[:, :, c1:c1 + moba_w] * scale
    w_mkv = w_in[:, :, c1 + moba_w:]
    w_in_b = jnp.concatenate([w_fq, w_fkv, w_mq, w_mkv, w_fg], axis=2).astype(BF16)

    w_out_b = w_mix_out.astype(BF16)
    w_cq_b = (w_cq * (D // N_CROSS_HEADS) ** -0.5).astype(BF16)
    w_ckv_b = jnp.concatenate([w_ck, w_cv], axis=2).astype(BF16)
    w_co_b = w_co.astype(BF16)
    w_r = jnp.pad(w_router, ((0, 0), (0, 0), (0, LANES - n_experts)))
    w_r_hi = w_r.astype(BF16)
    w_r_lo = (w_r - w_r_hi.astype(F32)).astype(BF16)
    b_r = jnp.pad(b_router, ((0, 0), (0, LANES - n_experts))).reshape(depth, 1, LANES)

    bias_tiles = _moba_bias_tables(rel_bias, S)
    blk_avg = _block_average_matrix(S)
    mem2d = mem.reshape(B * n_mem, D)
    xt = x.reshape(T, D)

    for l in range(depth):
        qkv, f_logit = _in_proj(xt, w_in_b[l])
        c_aug = _fox_decay(f_logit, b_forget[l], S, n_fox)
        o_f = _fox_attention(qkv, c_aug, B, S, n_fox)
        o_m = _moba_attention(qkv, blk_avg, bias_tiles, B, S, n_fox, n_moba)
        xt = _mix_out(o_f, o_m, xt, w_out_b[l], ln1_g[l], ln1_b[l], alpha)

        kv = _mem_proj(mem2d, w_ckv_b[l])
        xt, xt_tiles, top_idx, gates, counts = _cross_and_route(
            xt, w_cq_b[l], kv, w_co_b[l], ln2_g[l], ln2_b[l], w_r_hi[l], w_r_lo[l], b_r[l],
            alpha, S, n_mem, n_experts)

        step_idx, block_expert, n_valid = _dispatch(
            top_idx[:, :TOP_K], counts[0, :n_experts].astype(jnp.int32), D // LANES)
        ys = _moe_experts(l, block_expert, n_valid, step_idx, xt_tiles, w_gate_up, b_gate_up,
                          w_down, b_down)
        xt = _combine_ln(xt, ys, gates, ln3_g[l], ln3_b[l], alpha)

    return xt.reshape(B, S, D)
```

```python
import functools
import math

import jax
import jax.numpy as jnp
import numpy as np
from jax import lax
from jax.experimental import pallas as pl
from jax.experimental.pallas import tpu as pltpu

F32 = jnp.float32
BF16 = jnp.bfloat16

HEAD_DIM = 64
N_BUCKETS = 32
MAX_DISTANCE = 128
MOBA_BLOCK = 256
MOBA_TOPK = 3
N_CROSS_HEADS = 4
TOP_K = 4
SWIGLU_LIMIT = 7.0
SWIGLU_ALPHA = 1.702
LN_EPS = 1e-5

LANES = 128
VMEM_LIMIT = 56 * 1024 * 1024

ROW_TILE = 512
ATT_TILE = MOBA_BLOCK
MOE_TILE = 512
MOE_FCHUNK = 512
MASKED = -1e30
LOG2E = math.log2(math.e)


def _params(*sem):
    return pltpu.CompilerParams(dimension_semantics=sem, vmem_limit_bytes=VMEM_LIMIT)


def _nt_dot(a, b):
    return lax.dot_general(a, b, (((1,), (1,)), ((), ())), preferred_element_type=F32)


def _layer_norm(y, g, b):
    mu = jnp.mean(y, axis=-1, keepdims=True)
    d = y - mu
    var = jnp.mean(d * d, axis=-1, keepdims=True)
    return d * lax.rsqrt(var + LN_EPS) * g + b


def _in_proj_kernel(x_ref, w_ref, qkv_ref, f_ref):
    xb = x_ref[...].astype(BF16)
    n_qkv = qkv_ref.shape[1]
    for n0 in range(0, n_qkv, 512):
        qkv_ref[:, n0:n0 + 512] = jnp.dot(
            xb, w_ref[:, n0:n0 + 512], preferred_element_type=F32).astype(BF16)
    f_ref[...] = jnp.dot(xb, w_ref[:, n_qkv:], preferred_element_type=F32)


def _in_proj(x, w):
    T, D = x.shape
    n_all = w.shape[1]
    n_qkv = n_all - LANES
    return pl.pallas_call(
        _in_proj_kernel,
        grid=(T // ROW_TILE,),
        in_specs=[pl.BlockSpec((ROW_TILE, D), lambda i: (i, 0)),
                  pl.BlockSpec((D, n_all), lambda i: (0, 0))],
        out_specs=[pl.BlockSpec((ROW_TILE, n_qkv), lambda i: (i, 0)),
                   pl.BlockSpec((ROW_TILE, LANES), lambda i: (i, 0))],
        out_shape=[jax.ShapeDtypeStruct((T, n_qkv), BF16),
                   jax.ShapeDtypeStruct((T, LANES), F32)],
        compiler_params=_params("parallel"),
        name="in_proj",
    )(x, w)


def _split3(x):
    p1 = x.astype(BF16)
    r1 = x - p1.astype(F32)
    p2 = r1.astype(BF16)
    p3 = (r1 - p2.astype(F32)).astype(BF16)
    return p1, p2, p3


def _decay_kernel(f_ref, b_ref, tri_ref, place_ref, c_ref):
    S = f_ref.shape[0]
    blk = tri_ref.shape[0]
    carry = jnp.zeros((1, LANES), F32)
    for j in range(S // blk):
        z = f_ref[j * blk:(j + 1) * blk, :] + b_ref[...]
        ls = jnp.minimum(z, 0.0) - jnp.log1p(jnp.exp(-jnp.abs(z)))
        c = carry
        for piece in _split3(ls):
            c = c + jnp.dot(tri_ref[...], piece, preferred_element_type=F32)
        carry = c[blk - 1:blk, :]
        aug = None
        for i, piece in enumerate(_split3(c * LOG2E)):
            t = jnp.dot(piece, place_ref[i], preferred_element_type=F32)
            aug = t if aug is None else aug + t
        c_ref[j * blk:(j + 1) * blk, :] = aug.astype(BF16)


def _fox_decay(f_logit, b_forget, S, n_fox):
    T = f_logit.shape[0]
    pairs = n_fox * HEAD_DIM // LANES
    blk = ATT_TILE
    tri = jnp.asarray(np.tril(np.ones((blk, blk), np.float32)), BF16)
    place = np.zeros((3, LANES, pairs * LANES), np.float32)
    for h in range(n_fox):
        for i in range(3):
            place[i, h, (h // 2) * LANES + 3 * (h % 2) + i] = -1.0
    b_pad = jnp.pad(b_forget, (0, LANES - n_fox)).reshape(1, LANES)
    return pl.pallas_call(
        _decay_kernel,
        grid=(T // S,),
        in_specs=[pl.BlockSpec((S, LANES), lambda b: (b, 0)),
                  pl.BlockSpec((1, LANES), lambda b: (0, 0)),
                  pl.BlockSpec((blk, blk), lambda b: (0, 0)),
                  pl.BlockSpec((3, LANES, pairs * LANES), lambda b: (0, 0, 0))],
        out_specs=pl.BlockSpec((S, pairs * LANES), lambda b: (b, 0)),
        out_shape=jax.ShapeDtypeStruct((T, pairs * LANES), BF16),
        compiler_params=_params("parallel"),
        name="fox_decay",
    )(f_logit, b_pad, tri, jnp.asarray(place, BF16))


VT_ROWS = HEAD_DIM + 16


def _online_update_t(s, vt, m, acc):
    m_new = jnp.maximum(m, jnp.max(s, axis=0, keepdims=True))
    p = jnp.exp2(s - m_new)
    acc_new = jnp.exp2(m - m_new) * acc + jnp.dot(vt, p.astype(BF16), preferred_element_type=F32)
    return m_new, acc_new


def _init_state(tq):
    return jnp.full((1, tq), MASKED, F32), jnp.zeros((VT_ROWS, tq), F32)


def _split_heads(q):
    lane = lax.broadcasted_iota(jnp.int32, q.shape, 1)
    zero = jnp.zeros((), q.dtype)
    return jnp.where(lane < HEAD_DIM, q, zero), jnp.where(lane < HEAD_DIM, zero, q)


def _causal_mask_t(tq):
    key = lax.broadcasted_iota(jnp.int32, (tq, tq), 0)
    qry = lax.broadcasted_iota(jnp.int32, (tq, tq), 1)
    return key <= qry


def _transpose_values(v_ref, vt_scr):
    tq = ATT_TILE
    S = v_ref.shape[0]
    for hh in range(2):
        vt_scr[hh * VT_ROWS + HEAD_DIM:(hh + 1) * VT_ROWS, :] = jnp.ones(
            (VT_ROWS - HEAD_DIM, S), BF16)
    for j in range(S // tq):
        vt = v_ref[j * tq:(j + 1) * tq, :].astype(F32).T.astype(BF16)
        for hh in range(2):
            vt_scr[hh * VT_ROWS:hh * VT_ROWS + HEAD_DIM, j * tq:(j + 1) * tq] = (
                vt[hh * HEAD_DIM:(hh + 1) * HEAD_DIM, :])


def _store_heads(o_ref, q0, state):
    tq = ATT_TILE
    o_t = jnp.concatenate(
        [acc[:HEAD_DIM] * (1.0 / acc[HEAD_DIM:HEAD_DIM + 1]) for _, acc in state], axis=0)
    o_ref[pl.ds(q0, tq), :] = o_t.T.astype(o_ref.dtype)


def _to_weights(x):
    return x.astype(F32).T.astype(BF16)


def _run_causal_tiles(n_blocks, score_tile, vt_scr, o_ref):
    tq = ATT_TILE
    tiles = [(qi, kb) for qi in range(n_blocks) for kb in [qi] + list(range(qi))]
    cur = score_tile(*tiles[0])
    state = None
    for i, (qi, kb) in enumerate(tiles):
        nxt = score_tile(*tiles[i + 1]) if i + 1 < len(tiles) else None
        if kb == qi:
            state = (_init_state(tq), _init_state(tq))
        new_state = []
        for hh in range(2):
            vt = vt_scr[hh * VT_ROWS:(hh + 1) * VT_ROWS, kb * tq:(kb + 1) * tq]
            new_state.append(_online_update_t(cur[hh], vt, *state[hh]))
        state = tuple(new_state)
        if kb == max(qi - 1, 0):
            _store_heads(o_ref, qi * tq, state)
        cur = nxt


def _fox_kernel(q_ref, k_ref, v_ref, c_ref, o_ref, vt_scr):
    tq = ATT_TILE
    S = q_ref.shape[0]
    causal = _causal_mask_t(tq)
    lane = lax.broadcasted_iota(jnp.int32, (tq, LANES), 1)
    ones = [jnp.where((lane >= 3 * hh) & (lane < 3 * hh + 3), 1.0, 0.0).astype(BF16)
            for hh in range(2)]
    _transpose_values(v_ref, vt_scr)
    q_weights = {}

    def score_tile(qi, kb):
        if qi not in q_weights:
            qh = _split_heads(q_ref[qi * tq:(qi + 1) * tq, :])
            q_weights.clear()
            q_weights[qi] = [_to_weights(jnp.concatenate([qh[hh], ones[hh]], axis=1))
                             for hh in range(2)]
        rows = slice(kb * tq, (kb + 1) * tq)
        ka = jnp.concatenate([k_ref[rows, :], c_ref[rows, :]], axis=1)
        out = []
        for hh in range(2):
            s = jnp.dot(ka, q_weights[qi][hh], preferred_element_type=F32)
            out.append(jnp.where(causal, s, MASKED) if kb == qi else s)
        return tuple(out)

    _run_causal_tiles(S // tq, score_tile, vt_scr, o_ref)


def _fox_attention(qkv, c_aug, B, S, n_fox):
    T = B * S
    pairs = n_fox * HEAD_DIM // LANES
    kb = pairs
    vb = 2 * pairs
    return pl.pallas_call(
        _fox_kernel,
        grid=(B, pairs),
        in_specs=[pl.BlockSpec((S, LANES), lambda b, p: (b, p)),
                  pl.BlockSpec((S, LANES), lambda b, p: (b, kb + p)),
                  pl.BlockSpec((S, LANES), lambda b, p: (b, vb + p)),
                  pl.BlockSpec((S, LANES), lambda b, p: (b, p))],
        out_specs=pl.BlockSpec((S, LANES), lambda b, p: (b, p)),
        out_shape=jax.ShapeDtypeStruct((T, pairs * LANES), BF16),
        scratch_shapes=[pltpu.VMEM((2 * VT_ROWS, S), BF16)],
        compiler_params=_params("parallel", "parallel"),
        name="fox_attention",
    )(qkv, qkv, qkv, c_aug)


def _moba_kernel(q_ref, k_ref, v_ref, a_ref, bias_ref, o_ref, vt_scr):
    tq = ATT_TILE
    S = q_ref.shape[0]
    rows = -(-(S // tq) // 8) * 8
    causal = _causal_mask_t(tq)
    blk_row = lax.broadcasted_iota(jnp.int32, (rows, tq), 0)
    _transpose_values(v_ref, vt_scr)

    kmean = jnp.dot(a_ref[...], k_ref[...], preferred_element_type=F32)
    kmean_hi = kmean.astype(BF16)
    kmean_lo = (kmean - kmean_hi.astype(F32)).astype(BF16)
    per_block = {}

    def prepare(qi):
        qh = _split_heads(q_ref[qi * tq:(qi + 1) * tq, :])
        qw = [_to_weights(qh[hh]) for hh in range(2)]
        offsets = []
        for hh in range(2):
            gate = (jnp.dot(kmean_hi, qw[hh], preferred_element_type=F32)
                    + jnp.dot(kmean_lo, qw[hh], preferred_element_type=F32))[:rows, :]
            rank = jnp.zeros((rows, tq), jnp.int32)
            for mb in range(qi):
                g_m = gate[mb:mb + 1, :]
                tie = jnp.where(blk_row > mb, 1, 0)
                rank = rank + jnp.where(g_m > gate, 1, jnp.where(g_m == gate, tie, 0))
            offsets.append(jnp.where(rank < MOBA_TOPK, 0.0, MASKED))
        return qw, offsets

    def score_tile(qi, kb):
        if qi not in per_block:
            per_block.clear()
            per_block[qi] = prepare(qi)
        qw, offsets = per_block[qi]
        k = k_ref[kb * tq:(kb + 1) * tq, :]
        out = []
        for hh in range(2):
            s = jnp.dot(k, qw[hh], preferred_element_type=F32)
            if kb == qi:
                s = jnp.where(causal, s + bias_ref[hh, 0], MASKED)
            elif kb == qi - 1:
                s = s + bias_ref[hh, 1] + offsets[hh][kb:kb + 1, :]
            else:
                s = s + (bias_ref[hh, 2, 0:1, :] + offsets[hh][kb:kb + 1, :])
            out.append(s)
        return tuple(out)

    _run_causal_tiles(S // tq, score_tile, vt_scr, o_ref)


def _moba_attention(qkv, blk_avg, bias_tiles, B, S, n_fox, n_moba):
    T = B * S
    pairs = n_moba * HEAD_DIM // LANES
    qb = 3 * n_fox * HEAD_DIM // LANES
    kb = qb + pairs
    vb = qb + 2 * pairs
    tq = ATT_TILE
    return pl.pallas_call(
        _moba_kernel,
        grid=(B, pairs),
        in_specs=[pl.BlockSpec((S, LANES), lambda b, p: (b, qb + p)),
                  pl.BlockSpec((S, LANES), lambda b, p: (b, kb + p)),
                  pl.BlockSpec((S, LANES), lambda b, p: (b, vb + p)),
                  pl.BlockSpec((LANES, S), lambda b, p: (0, 0)),
                  pl.BlockSpec((2, 3, tq, tq), lambda b, p: (p, 0, 0, 0))],
        out_specs=pl.BlockSpec((S, LANES), lambda b, p: (b, p)),
        out_shape=jax.ShapeDtypeStruct((T, pairs * LANES), BF16),
        scratch_shapes=[pltpu.VMEM((2 * VT_ROWS, S), BF16)],
        compiler_params=_params("parallel", "parallel"),
        name="moba_attention",
    )(qkv, qkv, qkv, blk_avg, bias_tiles)


def _mix_out_kernel(of_ref, om_ref, x_ref, w_ref, g_ref, b_ref, y_ref, *, alpha):
    half = of_ref.shape[1]
    h = (jnp.dot(of_ref[...], w_ref[:half, :], preferred_element_type=F32)
         + jnp.dot(om_ref[...], w_ref[half:, :], preferred_element_type=F32))
    y_ref[...] = _layer_norm(alpha * x_ref[...] + h, g_ref[...], b_ref[...])


def _mix_out(o_f, o_m, x, w, g, b, alpha):
    T, D = x.shape
    half = o_f.shape[1]
    row = lambda i: (i, 0)
    fixed = lambda i: (0, 0)
    return pl.pallas_call(
        functools.partial(_mix_out_kernel, alpha=alpha),
        grid=(T // ROW_TILE,),
        in_specs=[pl.BlockSpec((ROW_TILE, half), row),
                  pl.BlockSpec((ROW_TILE, half), row),
                  pl.BlockSpec((ROW_TILE, D), row),
                  pl.BlockSpec((D, D), fixed),
                  pl.BlockSpec((1, D), fixed),
                  pl.BlockSpec((1, D), fixed)],
        out_specs=pl.BlockSpec((ROW_TILE, D), row),
        out_shape=jax.ShapeDtypeStruct((T, D), F32),
        compiler_params=_params("parallel"),
        name="mix_out_ln",
    )(o_f, o_m, x, w, g.reshape(1, D), b.reshape(1, D))


def _mem_proj_kernel(m_ref, w_ref, kv_ref):
    mb = m_ref[...].astype(BF16)
    n = kv_ref.shape[1]
    for n0 in range(0, n, 512):
        kv_ref[:, n0:n0 + 512] = jnp.dot(
            mb, w_ref[:, n0:n0 + 512], preferred_element_type=F32).astype(BF16)


def _mem_proj(mem2d, w_kv):
    M, D = mem2d.shape
    n = w_kv.shape[1]
    tm = min(ROW_TILE, M)
    return pl.pallas_call(
        _mem_proj_kernel,
        grid=(M // tm,),
        in_specs=[pl.BlockSpec((tm, D), lambda i: (i, 0)),
                  pl.BlockSpec((D, n), lambda i: (0, 0))],
        out_specs=pl.BlockSpec((tm, n), lambda i: (i, 0)),
        out_shape=jax.ShapeDtypeStruct((M, n), BF16),
        compiler_params=_params("parallel"),
        name="mem_proj",
    )(mem2d, w_kv)


def _cross_kernel(x_ref, wq_ref, kv_ref, wo_ref, g_ref, b_ref, wr_hi_ref, wr_lo_ref, br_ref,
                  y_ref, yt_ref, idx_ref, gate_ref, count_ref, *, alpha, n_experts):
    D = x_ref.shape[1]

    @pl.when(pl.program_id(0) == 0)
    def _():
        count_ref[...] = jnp.zeros(count_ref.shape, F32)

    dh = D // N_CROSS_HEADS
    x = x_ref[...]
    q = jnp.dot(x.astype(BF16), wq_ref[...], preferred_element_type=F32).astype(BF16)
    heads = []
    for h in range(N_CROSS_HEADS):
        k_h = kv_ref[:, h * dh:(h + 1) * dh]
        v_h = kv_ref[:, D + h * dh:D + (h + 1) * dh]
        s = _nt_dot(q[:, h * dh:(h + 1) * dh], k_h)
        p = jnp.exp(s - jnp.max(s, axis=1, keepdims=True))
        p = p * (1.0 / jnp.sum(p, axis=1, keepdims=True))
        heads.append(jnp.dot(p.astype(BF16), v_h, preferred_element_type=F32).astype(BF16))
    o = jnp.concatenate(heads, axis=1)
    hproj = jnp.dot(o, wo_ref[...], preferred_element_type=F32)
    y = _layer_norm(alpha * x + hproj, g_ref[...], b_ref[...])
    y_ref[...] = y
    for j in range(D // LANES):
        yt_ref[pl.ds(j, y.shape[0], stride=D // LANES), :] = y[:, j * LANES:(j + 1) * LANES]
    y_hi = y.astype(BF16)

    y_lo = (y - y_hi.astype(F32)).astype(BF16)
    logits = (jnp.dot(y_hi, wr_hi_ref[...], preferred_element_type=F32)
              + jnp.dot(y_lo, wr_hi_ref[...], preferred_element_type=F32)
              + jnp.dot(y_hi, wr_lo_ref[...], preferred_element_type=F32)
              + br_ref[...])
    lane = lax.broadcasted_iota(jnp.int32, logits.shape, 1)
    lane_f = lane.astype(F32)
    work = jnp.where(lane < n_experts, logits, -jnp.inf)
    idx_out = jnp.zeros(logits.shape, F32)
    val_out = jnp.zeros(logits.shape, F32)
    top = None
    for kk in range(TOP_K):
        best = jnp.max(work, axis=1, keepdims=True)
        arg = jnp.min(jnp.where(work == best, lane_f, float(LANES)), axis=1, keepdims=True)
        work = jnp.where(lane_f == arg, -jnp.inf, work)
        if top is None:
            top = best
        idx_out = jnp.where(lane == kk, arg, idx_out)
        val_out = jnp.where(lane == kk, jnp.exp(best - top), val_out)
    idx_ref[...] = idx_out.astype(jnp.int32)
    gate_ref[...] = val_out * (1.0 / jnp.sum(val_out, axis=1, keepdims=True))

    chosen = jnp.where(work == -jnp.inf, jnp.where(lane < n_experts, 1.0, 0.0), 0.0)
    count_ref[...] = count_ref[...] + jnp.sum(chosen, axis=0, keepdims=True)


def _cross_and_route(x, wq, kv, wo, g, b, wr_hi, wr_lo, br, alpha, S, n_mem, n_experts):
    T, D = x.shape
    tiles_per_batch = S // ROW_TILE
    row = lambda i: (i, 0)
    fixed = lambda i: (0, 0)
    return pl.pallas_call(
        functools.partial(_cross_kernel, alpha=alpha, n_experts=n_experts),
        grid=(T // ROW_TILE,),
        in_specs=[pl.BlockSpec((ROW_TILE, D), row),
                  pl.BlockSpec((D, D), fixed),
                  pl.BlockSpec((n_mem, 2 * D), lambda i: (i // tiles_per_batch, 0)),
                  pl.BlockSpec((D, D), fixed),
                  pl.BlockSpec((1, D), fixed),
                  pl.BlockSpec((1, D), fixed),
                  pl.BlockSpec((D, LANES), fixed),
                  pl.BlockSpec((D, LANES), fixed),
                  pl.BlockSpec((1, LANES), fixed)],
        out_specs=[pl.BlockSpec((ROW_TILE, D), row),
                   pl.BlockSpec((ROW_TILE * D // LANES, LANES), row),
                   pl.BlockSpec((ROW_TILE, LANES), row),
                   pl.BlockSpec((ROW_TILE, LANES), row),
                   pl.BlockSpec((8, LANES), fixed)],
        out_shape=[jax.ShapeDtypeStruct((T, D), F32),
                   jax.ShapeDtypeStruct((T * D // LANES, LANES), F32),
                   jax.ShapeDtypeStruct((T, LANES), jnp.int32),
                   jax.ShapeDtypeStruct((T, LANES), F32),
                   jax.ShapeDtypeStruct((8, LANES), F32)],
        compiler_params=_params("arbitrary"),
        name="cross_attn_router",
    )(x, wq, kv, wo, g.reshape(1, D), b.reshape(1, D), wr_hi, wr_lo, br)


def _moe_kernel(be_ref, nv_ref, idx_hbm, x_hbm, wgu_ref, bgu_ref, wd_ref, bd_ref, y_hbm,
                wgu_bf, wd_bf, xbuf, ybuf, idx_smem, isem, gsem, ssem, zsem, fence_sem):
    i = pl.program_id(0)
    nb = pl.num_programs(0)
    nv = nv_ref[0]
    D, F = wgu_ref.shape[0], wd_ref.shape[0]
    sub = D // LANES
    tile = xbuf.shape[1] // sub
    valid = i < nv
    cur = i & 1
    nxt = 1 - cur

    def idx_copy(step, s):
        return pltpu.make_async_copy(idx_hbm.at[step], idx_smem.at[s], isem.at[s])

    def start_gathers(par, segment, buf):
        for r in range(tile):
            src = pl.multiple_of(idx_smem[par, segment * tile + r], sub)
            pltpu.make_async_copy(x_hbm.at[pl.ds(src, sub), :],
                                  xbuf.at[buf, pl.ds(r * sub, sub), :],
                                  gsem.at[buf]).start(priority=r % 2)

    def wait_gathers(buf):
        pltpu.make_async_copy(xbuf.at[buf], xbuf.at[buf], gsem.at[buf]).wait()

    def start_scatters(par, buf):
        for r in range(tile):
            dst = pl.multiple_of(idx_smem[par, tile + r], sub)
            pltpu.make_async_copy(ybuf.at[buf, pl.ds(r * sub, sub), :],
                                  y_hbm.at[pl.ds(dst, sub), :],
                                  ssem.at[buf]).start(priority=r % 2)

    def wait_scatters(buf):
        pltpu.make_async_copy(ybuf.at[buf], ybuf.at[buf], ssem.at[buf]).wait()

    @pl.when(i == 0)
    def _():
        idx_copy(0, 0).start()
    idx_copy(i, cur).wait()

    @pl.when(i + 1 < nb)
    def _():
        idx_copy(i + 1, nxt).start()

    @pl.when(i == 0)
    def _():
        ybuf[1] = jnp.zeros(ybuf.shape[1:], ybuf.dtype)
        start_gathers(0, 2, 0)

    @pl.when(i <= nv)
    def _():
        wait_gathers(cur)

    @pl.when(jnp.logical_not(valid) & (i >= 1) & (i - 2 < nv))
    def _():
        wait_scatters(cur)

    @pl.when(valid & ((i == 0) | (be_ref[i] != be_ref[jnp.maximum(i - 1, 0)])))
    def _():
        def cast_rows(ref, out, r, carry):
            r0 = pl.multiple_of(r * LANES, LANES)
            out[pl.ds(r0, LANES), :] = ref[pl.ds(r0, LANES), :].astype(BF16)
            return carry
        lax.fori_loop(0, D // LANES, functools.partial(cast_rows, wgu_ref, wgu_bf), 0)
        lax.fori_loop(0, F // LANES, functools.partial(cast_rows, wd_ref, wd_bf), 0)

    def compute(par):
        x = jnp.concatenate([xbuf[par, pl.ds(j, tile, stride=sub), :].astype(BF16)
                             for j in range(sub)], axis=1)
        start_gathers(par, 0, 1 - par)
        acc = None
        for f0 in range(0, F, MOE_FCHUNK):
            f1 = f0 + MOE_FCHUNK
            if f0 == MOE_FCHUNK:
                pl.semaphore_signal(fence_sem, 1)
                pl.semaphore_wait(fence_sem, 1)
                start_scatters(par, 1 - par)
            g = jnp.dot(x, wgu_bf[:, f0:f1], preferred_element_type=F32) + bgu_ref[0, :, f0:f1]
            u = (jnp.dot(x, wgu_bf[:, F + f0:F + f1], preferred_element_type=F32)
                 + bgu_ref[0, :, F + f0:F + f1])
            g = jnp.minimum(g, SWIGLU_LIMIT)
            u = jnp.clip(u, -SWIGLU_LIMIT, SWIGLU_LIMIT)
            glu = g * jax.nn.sigmoid(g * SWIGLU_ALPHA)
            act = ((u + 1.0) * glu).astype(BF16)
            part = jnp.dot(act, wd_bf[f0:f1, :], preferred_element_type=F32)
            acc = part if acc is None else acc + part
        y = acc + bd_ref[0]
        pl.when(i >= 1)(functools.partial(wait_scatters, par))
        for j in range(sub):
            ybuf[par, pl.ds(j, tile, stride=sub), :] = y[:, j * LANES:(j + 1) * LANES]

    for par in range(2):
        pl.when(valid & (cur == par))(functools.partial(compute, par))
        pl.when((i == nv) & (cur == par))(functools.partial(start_scatters, par, 1 - par))

    @pl.when(jnp.logical_not(valid))
    def _():
        ybuf[cur] = jnp.zeros(ybuf.shape[1:], ybuf.dtype)
        row0 = pl.multiple_of(i * tile * sub, tile * sub)
        zero_fill = pltpu.make_async_copy(ybuf.at[cur], y_hbm.at[pl.ds(row0, tile * sub), :],
                                          zsem)
        zero_fill.start()
        zero_fill.wait()

    @pl.when((i == nb - 1) & (i - 1 < nv))
    def _():
        wait_scatters(nxt)


def _moe_experts(layer, block_expert, n_valid, step_idx, x, wgu, bgu, wd, bd):
    L, E, D, F2 = wgu.shape
    n_blocks = step_idx.shape[0]
    F = F2 // 2
    sub = D // LANES
    assert n_blocks >= 3 and x.shape[1] == LANES and F >= 2 * MOE_FCHUNK

    def weight(i, be, nv):
        return (layer, be[i], 0, 0)

    def bias(i, be, nv):
        return (layer * E + be[i], 0, 0)

    grid_spec = pltpu.PrefetchScalarGridSpec(
        num_scalar_prefetch=2,
        grid=(n_blocks,),
        in_specs=[pl.BlockSpec(memory_space=pl.ANY),
                  pl.BlockSpec(memory_space=pl.ANY),
                  pl.BlockSpec((None, None, D, F2), weight),
                  pl.BlockSpec((1, 1, F2), bias),
                  pl.BlockSpec((None, None, F, D), weight),
                  pl.BlockSpec((1, 1, D), bias)],
        out_specs=pl.BlockSpec(memory_space=pl.ANY),
        scratch_shapes=[pltpu.VMEM((D, F2), BF16), pltpu.VMEM((F, D), BF16),
                        pltpu.VMEM((2, MOE_TILE * sub, LANES), F32),
                        pltpu.VMEM((2, MOE_TILE * sub, LANES), F32),
                        pltpu.SMEM((2, 3 * MOE_TILE), jnp.int32),
                        pltpu.SemaphoreType.DMA((2,)), pltpu.SemaphoreType.DMA((2,)),
                        pltpu.SemaphoreType.DMA((2,)), pltpu.SemaphoreType.DMA(()),
                        pltpu.SemaphoreType.REGULAR],
    )
    return pl.pallas_call(
        _moe_kernel,
        grid_spec=grid_spec,
        out_shape=jax.ShapeDtypeStruct(((n_blocks + 1) * MOE_TILE * sub, LANES), F32),
        compiler_params=_params("arbitrary"),
        name="moe_experts",
    )(block_expert, n_valid, step_idx, x, wgu, bgu.reshape(L * E, 1, F2), wd,
      bd.reshape(L * E, 1, D))


def _combine_kernel(x_ref, *refs, alpha):
    ys_refs, (gate_ref, g_ref, b_ref, y_ref) = refs[:TOP_K], refs[TOP_K:]
    tm, D = x_ref.shape
    sub = D // LANES
    gates = gate_ref[...]
    cols = []
    for j in range(sub):
        y = alpha * x_ref[:, j * LANES:(j + 1) * LANES]
        for kk in range(TOP_K):
            y = y + ys_refs[kk][pl.ds(j, tm, stride=sub), :] * gates[:, kk:kk + 1]
        cols.append(y)
    y_ref[...] = _layer_norm(jnp.concatenate(cols, axis=1), g_ref[...], b_ref[...])


def _combine_ln(x, ys, gates, g, b, alpha):
    T, D = x.shape
    tm = ROW_TILE
    sub = D // LANES
    row = lambda i: (i, 0)
    fixed = lambda i: (0, 0)
    ys_specs = [pl.BlockSpec((tm * sub, LANES),
                             functools.partial(lambda k, i: (k * (T // tm) + i, 0), k))
                for k in range(TOP_K)]
    return pl.pallas_call(
        functools.partial(_combine_kernel, alpha=alpha),
        grid=(T // tm,),
        in_specs=[pl.BlockSpec((tm, D), row)] + ys_specs + [
            pl.BlockSpec((tm, LANES), row),
            pl.BlockSpec((1, D), fixed),
            pl.BlockSpec((1, D), fixed)],
        out_specs=pl.BlockSpec((tm, D), row),
        out_shape=jax.ShapeDtypeStruct((T, D), F32),
        compiler_params=_params("parallel"),
        name="moe_combine_ln",
    )(x, *([ys] * TOP_K), gates, g.reshape(1, D), b.reshape(1, D))


def _t5_bucket(dist):
    max_exact = N_BUCKETS // 2
    n = jnp.maximum(dist, 0)
    nf = jnp.maximum(n, 1).astype(F32)
    large = max_exact + (jnp.log(nf / max_exact) / math.log(MAX_DISTANCE / max_exact)
                         * (N_BUCKETS - max_exact)).astype(jnp.int32)
    large = jnp.minimum(large, N_BUCKETS - 1)
    return jnp.where(n < max_exact, n, large)


def _moba_bias_tables(rel_bias, S):
    n = ATT_TILE
    n_heads = rel_bias.shape[1]
    by_dist = rel_bias.T.astype(F32)[:, _t5_bucket(jnp.arange(2 * n + 1))] * LOG2E

    def toeplitz(v):
        return jnp.tile(v, (1, n))[:, :n * (2 * n - 1)].reshape(n_heads, n, 2 * n - 1)[:, :, :n]

    own = toeplitz(by_dist[:, :2 * n])
    prev = toeplitz(jnp.roll(by_dist[:, :2 * n], -n, axis=1))
    far = jnp.broadcast_to(by_dist[:, 2 * n][:, None, None], (n_heads, n, n))
    return jnp.stack([own, prev, far], axis=1)


def _block_average_matrix(S):
    a = np.zeros((LANES, S), np.float32)
    for n in range(S // MOBA_BLOCK):
        a[n, n * MOBA_BLOCK:(n + 1) * MOBA_BLOCK] = 1.0 / MOBA_BLOCK
    return jnp.asarray(a, BF16)


def _dispatch(top_idx, counts, rows_per_token):
    T = top_idx.shape[0]
    TK = T * TOP_K
    n_experts = counts.shape[0]
    padded = ((counts + MOE_TILE - 1) // MOE_TILE) * MOE_TILE
    start = jnp.cumsum(counts) - counts
    pend = jnp.cumsum(padded)
    pstart = pend - padded
    n_blocks = TK // MOE_TILE + n_experts
    n_valid = (pend[-1] // MOE_TILE).astype(jnp.int32)
    blk = jnp.minimum(jnp.arange(n_blocks, dtype=jnp.int32), n_valid - 1)
    block_expert = jnp.minimum(
        jnp.sum((pend[None, :] <= (blk * MOE_TILE)[:, None]).astype(jnp.int32), axis=1),
        n_experts - 1)
    order = jnp.argsort(top_idx.reshape(-1)).astype(jnp.int32)
    slot = jnp.arange(n_blocks * MOE_TILE, dtype=jnp.int32).reshape(n_blocks, MOE_TILE)
    within = slot - pstart[block_expert][:, None]
    real = within < counts[block_expert][:, None]
    src = jnp.clip(within + start[block_expert][:, None], 0, TK - 1)
    flat = order[src.reshape(-1)].reshape(n_blocks, MOE_TILE)
    token = flat // TOP_K
    out_row = jnp.where(real, (flat % TOP_K) * T + token,
                        slot + (TK - (start + counts)[block_expert])[:, None])
    spare = n_blocks * MOE_TILE + jnp.arange(MOE_TILE, dtype=jnp.int32)
    step_idx = jnp.concatenate(
        [jnp.concatenate([token[1:], token[-1:]], axis=0),
         jnp.concatenate([spare[None, :], out_row[:-1]], axis=0),
         token], axis=1)
    return step_idx * rows_per_token, block_expert, n_valid.reshape(1)


def kernel(x, mem, w_in, b_forget, w_mix_out, rel_bias, ln1_g, ln1_b, w_cq, w_ck, w_cv, w_co,
           ln2_g, ln2_b, w_router, b_router, w_gate_up, b_gate_up, w_down, b_down, ln3_g, ln3_b):
    B, S, D = x.shape
    depth = w_in.shape[0]
    n_mem = mem.shape[1]
    n_experts = w_router.shape[2]
    n_heads = D // HEAD_DIM
    n_fox = n_heads // 2
    n_moba = n_heads - n_fox
    fox_w, moba_w = n_fox * HEAD_DIM, n_moba * HEAD_DIM
    T = B * S
    alpha = (2 * depth) ** 0.25
    assert S % ROW_TILE == 0 and S % ATT_TILE == 0 and D % LANES == 0
    assert ATT_TILE == MOBA_BLOCK and MOBA_BLOCK >= MAX_DISTANCE
    assert n_experts <= LANES and (T * TOP_K) % MOE_TILE == 0

    scale = HEAD_DIM ** -0.5 * LOG2E
    c0 = 3 * fox_w
    c1 = c0 + n_fox
    w_fq = w_in[:, :, :fox_w] * scale
    w_fkv = w_in[:, :, fox_w:c0]
    w_fg = jnp.pad(w_in[:, :, c0:c1], ((0, 0), (0, 0), (0, LANES - n_fox)))
    w_mq = w_in[:, :, c1:c1 + moba_w] * scale
    w_mkv = w_in[:, :, c1 + moba_w:]
    w_in_b = jnp.concatenate([w_fq, w_fkv, w_mq, w_mkv, w_fg], axis=2).astype(BF16)

    w_out_b = w_mix_out.astype(BF16)
    w_cq_b = (w_cq * (D // N_CROSS_HEADS) ** -0.5).astype(BF16)
    w_ckv_b = jnp.concatenate([w_ck, w_cv], axis=2).astype(BF16)
    w_co_b = w_co.astype(BF16)
    w_r = jnp.pad(w_router, ((0, 0), (0, 0), (0, LANES - n_experts)))
    w_r_hi = w_r.astype(BF16)
    w_r_lo = (w_r - w_r_hi.astype(F32)).astype(BF16)
    b_r = jnp.pad(b_router, ((0, 0), (0, LANES - n_experts))).reshape(depth, 1, LANES)

    bias_tiles = _moba_bias_tables(rel_bias, S)
    blk_avg = _block_average_matrix(S)
    mem2d = mem.reshape(B * n_mem, D)
    xt = x.reshape(T, D)

    for l in range(depth):
        qkv, f_logit = _in_proj(xt, w_in_b[l])
        c_aug = _fox_decay(f_logit, b_forget[l], S, n_fox)
        o_f = _fox_attention(qkv, c_aug, B, S, n_fox)
        o_m = _moba_attention(qkv, blk_avg, bias_tiles, B, S, n_fox, n_moba)
        xt = _mix_out(o_f, o_m, xt, w_out_b[l], ln1_g[l], ln1_b[l], alpha)

        kv = _mem_proj(mem2d, w_ckv_b[l])
        xt, xt_tiles, top_idx, gates, counts = _cross_and_route(
            xt, w_cq_b[l], kv, w_co_b[l], ln2_g[l], ln2_b[l], w_r_hi[l], w_r_lo[l], b_r[l],
            alpha, S, n_mem, n_experts)

        step_idx, block_expert, n_valid = _dispatch(
            top_idx[:, :TOP_K], counts[0, :n_experts].astype(jnp.int32), D // LANES)
        ys = _moe_experts(l, block_expert, n_valid, step_idx, xt_tiles, w_gate_up, b_gate_up,
                          w_down, b_down)
        xt = _combine_ln(xt, ys, gates, ln3_g[l], ln3_b[l], alpha)

    return xt.reshape(B, S, D)
```

```python
import functools
import math

import jax
import jax.numpy as jnp
import numpy as np
from jax import lax
from jax.experimental import pallas as pl
from jax.experimental.pallas import tpu as pltpu

F32 = jnp.float32
BF16 = jnp.bfloat16

HEAD_DIM = 64
N_BUCKETS = 32
MAX_DISTANCE = 128
MOBA_BLOCK = 256
MOBA_TOPK = 3
N_CROSS_HEADS = 4
TOP_K = 4
SWIGLU_LIMIT = 7.0
SWIGLU_ALPHA = 1.702
LN_EPS = 1e-5

LANES = 128
VMEM_LIMIT = 56 * 1024 * 1024

ROW_TILE = 512
ATT_TILE = MOBA_BLOCK
MOE_TILE = 512
MOE_FCHUNK = 512
MASKED = -1e30
LOG2E = math.log2(math.e)


def _params(*sem):
    return pltpu.CompilerParams(dimension_semantics=sem, vmem_limit_bytes=VMEM_LIMIT)


def _nt_dot(a, b):
    return lax.dot_general(a, b, (((1,), (1,)), ((), ())), preferred_element_type=F32)


def _layer_norm(y, g, b):
    mu = jnp.mean(y, axis=-1, keepdims=True)
    d = y - mu
    var = jnp.mean(d * d, axis=-1, keepdims=True)
    return d * lax.rsqrt(var + LN_EPS) * g + b


def _in_proj_kernel(x_ref, w_ref, qkv_ref, f_ref):
    xb = x_ref[...].astype(BF16)
    n_qkv = qkv_ref.shape[1]
    for n0 in range(0, n_qkv, 512):
        qkv_ref[:, n0:n0 + 512] = jnp.dot(
            xb, w_ref[:, n0:n0 + 512], preferred_element_type=F32).astype(BF16)
    f_ref[...] = jnp.dot(xb, w_ref[:, n_qkv:], preferred_element_type=F32)


def _in_proj(x, w):
    T, D = x.shape
    n_all = w.shape[1]
    n_qkv = n_all - LANES
    return pl.pallas_call(
        _in_proj_kernel,
        grid=(T // ROW_TILE,),
        in_specs=[pl.BlockSpec((ROW_TILE, D), lambda i: (i, 0)),
                  pl.BlockSpec((D, n_all), lambda i: (0, 0))],
        out_specs=[pl.BlockSpec((ROW_TILE, n_qkv), lambda i: (i, 0)),
                   pl.BlockSpec((ROW_TILE, LANES), lambda i: (i, 0))],
        out_shape=[jax.ShapeDtypeStruct((T, n_qkv), BF16),
                   jax.ShapeDtypeStruct((T, LANES), F32)],
        compiler_params=_params("parallel"),
        name="in_proj",
    )(x, w)


def _split3(x):
    p1 = x.astype(BF16)
    r1 = x - p1.astype(F32)
    p2 = r1.astype(BF16)
    p3 = (r1 - p2.astype(F32)).astype(BF16)
    return p1, p2, p3


def _decay_kernel(f_ref, b_ref, tri_ref, place_ref, c_ref):
    S = f_ref.shape[0]
    blk = tri_ref.shape[0]
    carry = jnp.zeros((1, LANES), F32)
    for j in range(S // blk):
        z = f_ref[j * blk:(j + 1) * blk, :] + b_ref[...]
        ls = jnp.minimum(z, 0.0) - jnp.log1p(jnp.exp(-jnp.abs(z)))
        c = carry
        for piece in _split3(ls):
            c = c + jnp.dot(tri_ref[...], piece, preferred_element_type=F32)
        carry = c[blk - 1:blk, :]
        aug = None
        for i, piece in enumerate(_split3(c * LOG2E)):
            t = jnp.dot(piece, place_ref[i], preferred_element_type=F32)
            aug = t if aug is None else aug + t
        c_ref[j * blk:(j + 1) * blk, :] = aug.astype(BF16)


def _fox_decay(f_logit, b_forget, S, n_fox):
    T = f_logit.shape[0]
    pairs = n_fox * HEAD_DIM // LANES
    blk = ATT_TILE
    tri = jnp.asarray(np.tril(np.ones((blk, blk), np.float32)), BF16)
    place = np.zeros((3, LANES, pairs * LANES), np.float32)
    for h in range(n_fox):
        for i in range(3):
            place[i, h, (h // 2) * LANES + 3 * (h % 2) + i] = -1.0
    b_pad = jnp.pad(b_forget, (0, LANES - n_fox)).reshape(1, LANES)
    return pl.pallas_call(
        _decay_kernel,
        grid=(T // S,),
        in_specs=[pl.BlockSpec((S, LANES), lambda b: (b, 0)),
                  pl.BlockSpec((1, LANES), lambda b: (0, 0)),
                  pl.BlockSpec((blk, blk), lambda b: (0, 0)),
                  pl.BlockSpec((3, LANES, pairs * LANES), lambda b: (0, 0, 0))],
        out_specs=pl.BlockSpec((S, pairs * LANES), lambda b: (b, 0)),
        out_shape=jax.ShapeDtypeStruct((T, pairs * LANES), BF16),
        compiler_params=_params("parallel"),
        name="fox_decay",
    )(f_logit, b_pad, tri, jnp.asarray(place, BF16))


VT_ROWS = HEAD_DIM + 16
SCORE_LOOKAHEAD = 2


def _online_update_t(s, vt, m, acc):
    m_new = jnp.maximum(m, jnp.max(s, axis=0, keepdims=True))
    p = jnp.exp2(s - m_new)
    acc_new = jnp.exp2(m - m_new) * acc + jnp.dot(vt, p.astype(BF16), preferred_element_type=F32)
    return m_new, acc_new


def _init_state(tq):
    return jnp.full((1, tq), MASKED, F32), jnp.zeros((VT_ROWS, tq), F32)


def _split_heads(q):
    lane = lax.broadcasted_iota(jnp.int32, q.shape, 1)
    zero = jnp.zeros((), q.dtype)
    return jnp.where(lane < HEAD_DIM, q, zero), jnp.where(lane < HEAD_DIM, zero, q)


def _causal_mask_t(tq):
    key = lax.broadcasted_iota(jnp.int32, (tq, tq), 0)
    qry = lax.broadcasted_iota(jnp.int32, (tq, tq), 1)
    return key <= qry


def _transpose_values(v_ref, vt_scr):
    tq = ATT_TILE
    S = v_ref.shape[0]
    for hh in range(2):
        vt_scr[hh * VT_ROWS + HEAD_DIM:(hh + 1) * VT_ROWS, :] = jnp.ones(
            (VT_ROWS - HEAD_DIM, S), BF16)
    for j in range(S // tq):
        vt = v_ref[j * tq:(j + 1) * tq, :].astype(F32).T.astype(BF16)
        for hh in range(2):
            vt_scr[hh * VT_ROWS:hh * VT_ROWS + HEAD_DIM, j * tq:(j + 1) * tq] = (
                vt[hh * HEAD_DIM:(hh + 1) * HEAD_DIM, :])


def _store_heads(o_ref, q0, state):
    tq = ATT_TILE
    o_t = jnp.concatenate(
        [acc[:HEAD_DIM] * (1.0 / acc[HEAD_DIM:HEAD_DIM + 1]) for _, acc in state], axis=0)
    o_ref[pl.ds(q0, tq), :] = o_t.T.astype(o_ref.dtype)


def _to_weights(x):
    return x.astype(F32).T.astype(BF16)


def _run_causal_tiles(n_blocks, score_tile, vt_scr, o_ref):
    tq = ATT_TILE
    tiles = [(qi, kb) for qi in range(n_blocks) for kb in [qi] + list(range(qi))]
    ahead = [score_tile(*t) for t in tiles[:SCORE_LOOKAHEAD]]
    state = None
    for i, (qi, kb) in enumerate(tiles):
        if i + SCORE_LOOKAHEAD < len(tiles):
            ahead.append(score_tile(*tiles[i + SCORE_LOOKAHEAD]))
        cur = ahead.pop(0)
        if kb == qi:
            state = (_init_state(tq), _init_state(tq))
        new_state = []
        for hh in range(2):
            vt = vt_scr[hh * VT_ROWS:(hh + 1) * VT_ROWS, kb * tq:(kb + 1) * tq]
            new_state.append(_online_update_t(cur[hh], vt, *state[hh]))
        state = tuple(new_state)
        if kb == max(qi - 1, 0):
            _store_heads(o_ref, qi * tq, state)


def _fox_kernel(q_ref, k_ref, v_ref, c_ref, o_ref, vt_scr):
    tq = ATT_TILE
    S = q_ref.shape[0]
    causal = _causal_mask_t(tq)
    lane = lax.broadcasted_iota(jnp.int32, (tq, LANES), 1)
    ones = [jnp.where((lane >= 3 * hh) & (lane < 3 * hh + 3), 1.0, 0.0).astype(BF16)
            for hh in range(2)]
    _transpose_values(v_ref, vt_scr)
    q_weights = {}

    def score_tile(qi, kb):
        if qi not in q_weights:
            qh = _split_heads(q_ref[qi * tq:(qi + 1) * tq, :])
            q_weights.clear()
            q_weights[qi] = [_to_weights(jnp.concatenate([qh[hh], ones[hh]], axis=1))
                             for hh in range(2)]
        rows = slice(kb * tq, (kb + 1) * tq)
        ka = jnp.concatenate([k_ref[rows, :], c_ref[rows, :]], axis=1)
        out = []
        for hh in range(2):
            s = jnp.dot(ka, q_weights[qi][hh], preferred_element_type=F32)
            out.append(jnp.where(causal, s, MASKED) if kb == qi else s)
        return tuple(out)

    _run_causal_tiles(S // tq, score_tile, vt_scr, o_ref)


def _fox_attention(qkv, c_aug, B, S, n_fox):
    T = B * S
    pairs = n_fox * HEAD_DIM // LANES
    kb = pairs
    vb = 2 * pairs
    return pl.pallas_call(
        _fox_kernel,
        grid=(B, pairs),
        in_specs=[pl.BlockSpec((S, LANES), lambda b, p: (b, p)),
                  pl.BlockSpec((S, LANES), lambda b, p: (b, kb + p)),
                  pl.BlockSpec((S, LANES), lambda b, p: (b, vb + p)),
                  pl.BlockSpec((S, LANES), lambda b, p: (b, p))],
        out_specs=pl.BlockSpec((S, LANES), lambda b, p: (b, p)),
        out_shape=jax.ShapeDtypeStruct((T, pairs * LANES), BF16),
        scratch_shapes=[pltpu.VMEM((2 * VT_ROWS, S), BF16)],
        compiler_params=_params("parallel", "parallel"),
        name="fox_attention",
    )(qkv, qkv, qkv, c_aug)


def _moba_kernel(q_ref, k_ref, v_ref, a_ref, bias_ref, o_ref, vt_scr):
    tq = ATT_TILE
    S = q_ref.shape[0]
    rows = -(-(S // tq) // 8) * 8
    causal = _causal_mask_t(tq)
    blk_row = lax.broadcasted_iota(jnp.int32, (rows, tq), 0)
    _transpose_values(v_ref, vt_scr)

    kmean = jnp.dot(a_ref[...], k_ref[...], preferred_element_type=F32)
    kmean_hi = kmean.astype(BF16)
    kmean_lo = (kmean - kmean_hi.astype(F32)).astype(BF16)
    per_block = {}

    def prepare(qi):
        qh = _split_heads(q_ref[qi * tq:(qi + 1) * tq, :])
        qw = [_to_weights(qh[hh]) for hh in range(2)]
        offsets = []
        for hh in range(2):
            gate = (jnp.dot(kmean_hi, qw[hh], preferred_element_type=F32)
                    + jnp.dot(kmean_lo, qw[hh], preferred_element_type=F32))[:rows, :]
            rank = jnp.zeros((rows, tq), jnp.int32)
            for mb in range(qi):
                g_m = gate[mb:mb + 1, :]
                tie = jnp.where(blk_row > mb, 1, 0)
                rank = rank + jnp.where(g_m > gate, 1, jnp.where(g_m == gate, tie, 0))
            offsets.append(jnp.where(rank < MOBA_TOPK, 0.0, MASKED))
        return qw, offsets

    def score_tile(qi, kb):
        if qi not in per_block:
            per_block.clear()
            per_block[qi] = prepare(qi)
        qw, offsets = per_block[qi]
        k = k_ref[kb * tq:(kb + 1) * tq, :]
        out = []
        for hh in range(2):
            s = jnp.dot(k, qw[hh], preferred_element_type=F32)
            if kb == qi:
                s = jnp.where(causal, s + bias_ref[hh, 0], MASKED)
            elif kb == qi - 1:
                s = s + bias_ref[hh, 1] + offsets[hh][kb:kb + 1, :]
            else:
                s = s + (bias_ref[hh, 2, 0:1, :] + offsets[hh][kb:kb + 1, :])
            out.append(s)
        return tuple(out)

    _run_causal_tiles(S // tq, score_tile, vt_scr, o_ref)


def _moba_attention(qkv, blk_avg, bias_tiles, B, S, n_fox, n_moba):
    T = B * S
    pairs = n_moba * HEAD_DIM // LANES
    qb = 3 * n_fox * HEAD_DIM // LANES
    kb = qb + pairs
    vb = qb + 2 * pairs
    tq = ATT_TILE
    return pl.pallas_call(
        _moba_kernel,
        grid=(B, pairs),
        in_specs=[pl.BlockSpec((S, LANES), lambda b, p: (b, qb + p)),
                  pl.BlockSpec((S, LANES), lambda b, p: (b, kb + p)),
                  pl.BlockSpec((S, LANES), lambda b, p: (b, vb + p)),
                  pl.BlockSpec((LANES, S), lambda b, p: (0, 0)),
                  pl.BlockSpec((2, 3, tq, tq), lambda b, p: (p, 0, 0, 0))],
        out_specs=pl.BlockSpec((S, LANES), lambda b, p: (b, p)),
        out_shape=jax.ShapeDtypeStruct((T, pairs * LANES), BF16),
        scratch_shapes=[pltpu.VMEM((2 * VT_ROWS, S), BF16)],
        compiler_params=_params("parallel", "parallel"),
        name="moba_attention",
    )(qkv, qkv, qkv, blk_avg, bias_tiles)


def _mix_out_kernel(of_ref, om_ref, x_ref, w_ref, g_ref, b_ref, y_ref, *, alpha):
    half = of_ref.shape[1]
    rows = x_ref.shape[0] // 2
    parts = [slice(0, rows), slice(rows, 2 * rows)]
    hs = [jnp.dot(of_ref[r, :], w_ref[:half, :], preferred_element_type=F32)
          + jnp.dot(om_ref[r, :], w_ref[half:, :], preferred_element_type=F32) for r in parts]
    for r, h in zip(parts, hs):
        y_ref[r, :] = _layer_norm(alpha * x_ref[r, :] + h, g_ref[...], b_ref[...])


def _mix_out(o_f, o_m, x, w, g, b, alpha):
    T, D = x.shape
    half = o_f.shape[1]
    row = lambda i: (i, 0)
    fixed = lambda i: (0, 0)
    return pl.pallas_call(
        functools.partial(_mix_out_kernel, alpha=alpha),
        grid=(T // ROW_TILE,),
        in_specs=[pl.BlockSpec((ROW_TILE, half), row),
                  pl.BlockSpec((ROW_TILE, half), row),
                  pl.BlockSpec((ROW_TILE, D), row),
                  pl.BlockSpec((D, D), fixed),
                  pl.BlockSpec((1, D), fixed),
                  pl.BlockSpec((1, D), fixed)],
        out_specs=pl.BlockSpec((ROW_TILE, D), row),
        out_shape=jax.ShapeDtypeStruct((T, D), F32),
        compiler_params=_params("parallel"),
        name="mix_out_ln",
    )(o_f, o_m, x, w, g.reshape(1, D), b.reshape(1, D))


def _mem_proj_kernel(m_ref, w_ref, kv_ref):
    mb = m_ref[...].astype(BF16)
    n = kv_ref.shape[1]
    for n0 in range(0, n, 512):
        kv_ref[:, n0:n0 + 512] = jnp.dot(
            mb, w_ref[:, n0:n0 + 512], preferred_element_type=F32).astype(BF16)


def _mem_proj(mem2d, w_kv):
    M, D = mem2d.shape
    n = w_kv.shape[1]
    tm = min(ROW_TILE, M)
    return pl.pallas_call(
        _mem_proj_kernel,
        grid=(M // tm,),
        in_specs=[pl.BlockSpec((tm, D), lambda i: (i, 0)),
                  pl.BlockSpec((D, n), lambda i: (0, 0))],
        out_specs=pl.BlockSpec((tm, n), lambda i: (i, 0)),
        out_shape=jax.ShapeDtypeStruct((M, n), BF16),
        compiler_params=_params("parallel"),
        name="mem_proj",
    )(mem2d, w_kv)


def _cross_kernel(x_ref, wq_ref, kv_ref, wo_ref, g_ref, b_ref, wr_ref, br_ref,
                  y_ref, yt_ref, idx_ref, gate_ref, count_ref, *, alpha, n_experts):
    D = x_ref.shape[1]

    @pl.when(pl.program_id(0) == 0)
    def _():
        count_ref[...] = jnp.zeros(count_ref.shape, F32)

    dh = D // N_CROSS_HEADS
    sub = D // LANES
    half = x_ref.shape[0] // 2
    rows = [slice(0, half), slice(half, 2 * half)]
    xs = [x_ref[r, :] for r in rows]
    qs = [jnp.dot(x.astype(BF16), wq_ref[...], preferred_element_type=F32).astype(BF16)
          for x in xs]
    heads = [[], []]
    for h in range(N_CROSS_HEADS):
        k_h = kv_ref[:, h * dh:(h + 1) * dh]
        v_h = kv_ref[:, D + h * dh:D + (h + 1) * dh]
        scores = [_nt_dot(q[:, h * dh:(h + 1) * dh], k_h) for q in qs]
        for a, s in enumerate(scores):
            p = jnp.exp(s - jnp.max(s, axis=1, keepdims=True))
            p = p * (1.0 / jnp.sum(p, axis=1, keepdims=True))
            heads[a].append(
                jnp.dot(p.astype(BF16), v_h, preferred_element_type=F32).astype(BF16))
    hproj = [jnp.dot(jnp.concatenate(hs, axis=1), wo_ref[...], preferred_element_type=F32)
             for hs in heads]

    lane = lax.broadcasted_iota(jnp.int32, (half, LANES), 1)
    lane_f = lane.astype(F32)
    chosen_total = jnp.zeros((1, LANES), F32)
    for a, r in enumerate(rows):
        y = _layer_norm(alpha * xs[a] + hproj[a], g_ref[...], b_ref[...])
        y_ref[r, :] = y
        for j in range(sub):
            yt_ref[pl.ds(a * half * sub + j, half, stride=sub), :] = y[:, j * LANES:(j + 1) * LANES]

        y_hi = y.astype(BF16)
        y_lo = (y - y_hi.astype(F32)).astype(BF16)
        both = jnp.dot(y_hi, wr_ref[...], preferred_element_type=F32)
        logits = (both[:, :LANES] + both[:, LANES:]
                  + jnp.dot(y_lo, wr_ref[:, :LANES], preferred_element_type=F32)
                  + br_ref[...])
        work = jnp.where(lane < n_experts, logits, -jnp.inf)
        idx_out = jnp.zeros(logits.shape, F32)
        val_out = jnp.zeros(logits.shape, F32)
        top = None
        for kk in range(TOP_K):
            best = jnp.max(work, axis=1, keepdims=True)
            arg = jnp.min(jnp.where(work == best, lane_f, float(LANES)), axis=1, keepdims=True)
            work = jnp.where(lane_f == arg, -jnp.inf, work)
            if top is None:
                top = best
            idx_out = jnp.where(lane == kk, arg, idx_out)
            val_out = jnp.where(lane == kk, jnp.exp(best - top), val_out)
        idx_ref[r, :] = idx_out.astype(jnp.int32)
        gate_ref[r, :] = val_out * (1.0 / jnp.sum(val_out, axis=1, keepdims=True))
        chosen = jnp.where(work == -jnp.inf, jnp.where(lane < n_experts, 1.0, 0.0), 0.0)
        chosen_total = chosen_total + jnp.sum(chosen, axis=0, keepdims=True)

    count_ref[...] = count_ref[...] + chosen_total


def _cross_and_route(x, wq, kv, wo, g, b, wr_split, br, alpha, S, n_mem, n_experts):
    T, D = x.shape
    tiles_per_batch = S // ROW_TILE
    row = lambda i: (i, 0)
    fixed = lambda i: (0, 0)
    return pl.pallas_call(
        functools.partial(_cross_kernel, alpha=alpha, n_experts=n_experts),
        grid=(T // ROW_TILE,),
        in_specs=[pl.BlockSpec((ROW_TILE, D), row),
                  pl.BlockSpec((D, D), fixed),
                  pl.BlockSpec((n_mem, 2 * D), lambda i: (i // tiles_per_batch, 0)),
                  pl.BlockSpec((D, D), fixed),
                  pl.BlockSpec((1, D), fixed),
                  pl.BlockSpec((1, D), fixed),
                  pl.BlockSpec((D, 2 * LANES), fixed),
                  pl.BlockSpec((1, LANES), fixed)],
        out_specs=[pl.BlockSpec((ROW_TILE, D), row),
                   pl.BlockSpec((ROW_TILE * D // LANES, LANES), row),
                   pl.BlockSpec((ROW_TILE, LANES), row),
                   pl.BlockSpec((ROW_TILE, LANES), row),
                   pl.BlockSpec((8, LANES), fixed)],
        out_shape=[jax.ShapeDtypeStruct((T, D), F32),
                   jax.ShapeDtypeStruct((T * D // LANES, LANES), F32),
                   jax.ShapeDtypeStruct((T, LANES), jnp.int32),
                   jax.ShapeDtypeStruct((T, LANES), F32),
                   jax.ShapeDtypeStruct((8, LANES), F32)],
        compiler_params=_params("arbitrary"),
        name="cross_attn_router",
    )(x, wq, kv, wo, g.reshape(1, D), b.reshape(1, D), wr_split, br)


def _moe_kernel(be_ref, nv_ref, idx_hbm, x_hbm, wgu_ref, bgu_ref, wd_ref, bd_ref, y_hbm,
                wgu_bf, wd_bf, xbuf, ybuf, idx_smem, isem, gsem, ssem, zsem, fence_sem):
    i = pl.program_id(0)
    nb = pl.num_programs(0)
    nv = nv_ref[0]
    D, F = wgu_ref.shape[0], wd_ref.shape[0]
    sub = D // LANES
    tile = xbuf.shape[1] // sub
    valid = i < nv
    cur = i & 1
    nxt = 1 - cur

    def idx_copy(step, s):
        return pltpu.make_async_copy(idx_hbm.at[step], idx_smem.at[s], isem.at[s])

    def start_gathers(par, segment, buf):
        for r in range(tile):
            src = pl.multiple_of(idx_smem[par, segment * tile + r], sub)
            pltpu.make_async_copy(x_hbm.at[pl.ds(src, sub), :],
                                  xbuf.at[buf, pl.ds(r * sub, sub), :],
                                  gsem.at[buf]).start(priority=r % 2)

    def wait_gathers(buf):
        pltpu.make_async_copy(xbuf.at[buf], xbuf.at[buf], gsem.at[buf]).wait()

    def start_scatters(par, buf):
        for r in range(tile):
            dst = pl.multiple_of(idx_smem[par, tile + r], sub)
            pltpu.make_async_copy(ybuf.at[buf, pl.ds(r * sub, sub), :],
                                  y_hbm.at[pl.ds(dst, sub), :],
                                  ssem.at[buf]).start(priority=r % 2)

    def wait_scatters(buf):
        pltpu.make_async_copy(ybuf.at[buf], ybuf.at[buf], ssem.at[buf]).wait()

    @pl.when(i == 0)
    def _():
        idx_copy(0, 0).start()
    idx_copy(i, cur).wait()

    @pl.when(i + 1 < nb)
    def _():
        idx_copy(i + 1, nxt).start()

    @pl.when(i == 0)
    def _():
        ybuf[1] = jnp.zeros(ybuf.shape[1:], ybuf.dtype)
        start_gathers(0, 2, 0)

    @pl.when(i <= nv)
    def _():
        wait_gathers(cur)

    @pl.when((i >= 1) & (i - 2 < nv))
    def _():
        wait_scatters(cur)

    @pl.when(valid & ((i == 0) | (be_ref[i] != be_ref[jnp.maximum(i - 1, 0)])))
    def _():
        def cast_rows(ref, out, r, carry):
            r0 = pl.multiple_of(r * LANES, LANES)
            out[pl.ds(r0, LANES), :] = ref[pl.ds(r0, LANES), :].astype(BF16)
            return carry
        lax.fori_loop(0, D // LANES, functools.partial(cast_rows, wgu_ref, wgu_bf), 0)
        lax.fori_loop(0, F // LANES, functools.partial(cast_rows, wd_ref, wd_bf), 0)

    def compute(par):
        x = jnp.concatenate([xbuf[par, pl.ds(j, tile, stride=sub), :].astype(BF16)
                             for j in range(sub)], axis=1)
        start_gathers(par, 0, 1 - par)
        acc = None
        for f0 in range(0, F, MOE_FCHUNK):
            f1 = f0 + MOE_FCHUNK
            if f0 == MOE_FCHUNK:
                pl.semaphore_signal(fence_sem, 1)
                pl.semaphore_wait(fence_sem, 1)
                start_scatters(par, 1 - par)
            g = jnp.dot(x, wgu_bf[:, f0:f1], preferred_element_type=F32) + bgu_ref[0, :, f0:f1]
            u = (jnp.dot(x, wgu_bf[:, F + f0:F + f1], preferred_element_type=F32)
                 + bgu_ref[0, :, F + f0:F + f1])
            g = jnp.minimum(g, SWIGLU_LIMIT)
            u = jnp.clip(u, -SWIGLU_LIMIT, SWIGLU_LIMIT)
            glu = g * jax.nn.sigmoid(g * SWIGLU_ALPHA)
            act = ((u + 1.0) * glu).astype(BF16)
            part = jnp.dot(act, wd_bf[f0:f1, :], preferred_element_type=F32)
            acc = part if acc is None else acc + part
        y = acc + bd_ref[0]
        for j in range(sub):
            ybuf[par, pl.ds(j, tile, stride=sub), :] = y[:, j * LANES:(j + 1) * LANES]

    for par in range(2):
        pl.when(valid & (cur == par))(functools.partial(compute, par))
        pl.when((i == nv) & (cur == par))(functools.partial(start_scatters, par, 1 - par))

    @pl.when(jnp.logical_not(valid))
    def _():
        ybuf[cur] = jnp.zeros(ybuf.shape[1:], ybuf.dtype)
        row0 = pl.multiple_of(i * tile * sub, tile * sub)
        zero_fill = pltpu.make_async_copy(ybuf.at[cur], y_hbm.at[pl.ds(row0, tile * sub), :],
                                          zsem)
        zero_fill.start()
        zero_fill.wait()

    @pl.when((i == nb - 1) & (i - 1 < nv))
    def _():
        wait_scatters(nxt)


def _moe_experts(layer, block_expert, n_valid, step_idx, x, wgu, bgu, wd, bd):
    L, E, D, F2 = wgu.shape
    n_blocks = step_idx.shape[0]
    F = F2 // 2
    sub = D // LANES
    assert n_blocks >= 3 and x.shape[1] == LANES and F >= 2 * MOE_FCHUNK

    def weight(i, be, nv):
        return (layer, be[i], 0, 0)

    def bias(i, be, nv):
        return (layer * E + be[i], 0, 0)

    grid_spec = pltpu.PrefetchScalarGridSpec(
        num_scalar_prefetch=2,
        grid=(n_blocks,),
        in_specs=[pl.BlockSpec(memory_space=pl.ANY),
                  pl.BlockSpec(memory_space=pl.ANY),
                  pl.BlockSpec((None, None, D, F2), weight),
                  pl.BlockSpec((1, 1, F2), bias),
                  pl.BlockSpec((None, None, F, D), weight),
                  pl.BlockSpec((1, 1, D), bias)],
        out_specs=pl.BlockSpec(memory_space=pl.ANY),
        scratch_shapes=[pltpu.VMEM((D, F2), BF16), pltpu.VMEM((F, D), BF16),
                        pltpu.VMEM((2, MOE_TILE * sub, LANES), F32),
                        pltpu.VMEM((2, MOE_TILE * sub, LANES), F32),
                        pltpu.SMEM((2, 3 * MOE_TILE), jnp.int32),
                        pltpu.SemaphoreType.DMA((2,)), pltpu.SemaphoreType.DMA((2,)),
                        pltpu.SemaphoreType.DMA((2,)), pltpu.SemaphoreType.DMA(()),
                        pltpu.SemaphoreType.REGULAR],
    )
    return pl.pallas_call(
        _moe_kernel,
        grid_spec=grid_spec,
        out_shape=jax.ShapeDtypeStruct(((n_blocks + 1) * MOE_TILE * sub, LANES), F32),
        compiler_params=_params("arbitrary"),
        name="moe_experts",
    )(block_expert, n_valid, step_idx, x, wgu, bgu.reshape(L * E, 1, F2), wd,
      bd.reshape(L * E, 1, D))


def _combine_kernel(x_ref, *refs, alpha):
    ys_refs, (gate_ref, g_ref, b_ref, y_ref) = refs[:TOP_K], refs[TOP_K:]
    tm, D = x_ref.shape
    sub = D // LANES
    gates = gate_ref[...]
    cols = []
    for j in range(sub):
        y = alpha * x_ref[:, j * LANES:(j + 1) * LANES]
        for kk in range(TOP_K):
            y = y + ys_refs[kk][pl.ds(j, tm, stride=sub), :] * gates[:, kk:kk + 1]
        cols.append(y)
    y_ref[...] = _layer_norm(jnp.concatenate(cols, axis=1), g_ref[...], b_ref[...])


def _combine_ln(x, ys, gates, g, b, alpha):
    T, D = x.shape
    tm = ROW_TILE
    sub = D // LANES
    row = lambda i: (i, 0)
    fixed = lambda i: (0, 0)
    ys_specs = [pl.BlockSpec((tm * sub, LANES),
                             functools.partial(lambda k, i: (k * (T // tm) + i, 0), k))
                for k in range(TOP_K)]
    return pl.pallas_call(
        functools.partial(_combine_kernel, alpha=alpha),
        grid=(T // tm,),
        in_specs=[pl.BlockSpec((tm, D), row)] + ys_specs + [
            pl.BlockSpec((tm, LANES), row),
            pl.BlockSpec((1, D), fixed),
            pl.BlockSpec((1, D), fixed)],
        out_specs=pl.BlockSpec((tm, D), row),
        out_shape=jax.ShapeDtypeStruct((T, D), F32),
        compiler_params=_params("parallel"),
        name="moe_combine_ln",
    )(x, *([ys] * TOP_K), gates, g.reshape(1, D), b.reshape(1, D))


def _t5_bucket(dist):
    max_exact = N_BUCKETS // 2
    n = jnp.maximum(dist, 0)
    nf = jnp.maximum(n, 1).astype(F32)
    large = max_exact + (jnp.log(nf / max_exact) / math.log(MAX_DISTANCE / max_exact)
                         * (N_BUCKETS - max_exact)).astype(jnp.int32)
    large = jnp.minimum(large, N_BUCKETS - 1)
    return jnp.where(n < max_exact, n, large)


def _moba_bias_tables(rel_bias, S):
    n = ATT_TILE
    n_heads = rel_bias.shape[1]
    by_dist = rel_bias.T.astype(F32)[:, _t5_bucket(jnp.arange(2 * n + 1))] * LOG2E

    def toeplitz(v):
        return jnp.tile(v, (1, n))[:, :n * (2 * n - 1)].reshape(n_heads, n, 2 * n - 1)[:, :, :n]

    own = toeplitz(by_dist[:, :2 * n])
    prev = toeplitz(jnp.roll(by_dist[:, :2 * n], -n, axis=1))
    far = jnp.broadcast_to(by_dist[:, 2 * n][:, None, None], (n_heads, n, n))
    return jnp.stack([own, prev, far], axis=1)


def _block_average_matrix(S):
    a = np.zeros((LANES, S), np.float32)
    for n in range(S // MOBA_BLOCK):
        a[n, n * MOBA_BLOCK:(n + 1) * MOBA_BLOCK] = 1.0 / MOBA_BLOCK
    return jnp.asarray(a, BF16)


def _dispatch(top_idx, counts, rows_per_token):
    T = top_idx.shape[0]
    TK = T * TOP_K
    n_experts = counts.shape[0]
    padded = ((counts + MOE_TILE - 1) // MOE_TILE) * MOE_TILE
    start = jnp.cumsum(counts) - counts
    pend = jnp.cumsum(padded)
    pstart = pend - padded
    n_blocks = TK // MOE_TILE + n_experts
    n_valid = (pend[-1] // MOE_TILE).astype(jnp.int32)
    blk = jnp.minimum(jnp.arange(n_blocks, dtype=jnp.int32), n_valid - 1)
    block_expert = jnp.minimum(
        jnp.sum((pend[None, :] <= (blk * MOE_TILE)[:, None]).astype(jnp.int32), axis=1),
        n_experts - 1)
    order = jnp.argsort(top_idx.reshape(-1)).astype(jnp.int32)
    slot = jnp.arange(n_blocks * MOE_TILE, dtype=jnp.int32).reshape(n_blocks, MOE_TILE)
    within = slot - pstart[block_expert][:, None]
    real = within < counts[block_expert][:, None]
    src = jnp.clip(within + start[block_expert][:, None], 0, TK - 1)
    flat = order[src.reshape(-1)].reshape(n_blocks, MOE_TILE)
    token = flat // TOP_K
    out_row = jnp.where(real, (flat % TOP_K) * T + token,
                        slot + (TK - (start + counts)[block_expert])[:, None])
    spare = n_blocks * MOE_TILE + jnp.arange(MOE_TILE, dtype=jnp.int32)
    step_idx = jnp.concatenate(
        [jnp.concatenate([token[1:], token[-1:]], axis=0),
         jnp.concatenate([spare[None, :], out_row[:-1]], axis=0),
         token], axis=1)
    return step_idx * rows_per_token, block_expert, n_valid.reshape(1)


def kernel(x, mem, w_in, b_forget, w_mix_out, rel_bias, ln1_g, ln1_b, w_cq, w_ck, w_cv, w_co,
           ln2_g, ln2_b, w_router, b_router, w_gate_up, b_gate_up, w_down, b_down, ln3_g, ln3_b):
    B, S, D = x.shape
    depth = w_in.shape[0]
    n_mem = mem.shape[1]
    n_experts = w_router.shape[2]
    n_heads = D // HEAD_DIM
    n_fox = n_heads // 2
    n_moba = n_heads - n_fox
    fox_w, moba_w = n_fox * HEAD_DIM, n_moba * HEAD_DIM
    T = B * S
    alpha = (2 * depth) ** 0.25
    assert S % ROW_TILE == 0 and S % ATT_TILE == 0 and D % LANES == 0
    assert ATT_TILE == MOBA_BLOCK and MOBA_BLOCK >= MAX_DISTANCE
    assert n_experts <= LANES and (T * TOP_K) % MOE_TILE == 0

    scale = HEAD_DIM ** -0.5 * LOG2E
    c0 = 3 * fox_w
    c1 = c0 + n_fox
    w_fq = w_in[:, :, :fox_w] * scale
    w_fkv = w_in[:, :, fox_w:c0]
    w_fg = jnp.pad(w_in[:, :, c0:c1], ((0, 0), (0, 0), (0, LANES - n_fox)))
    w_mq = w_in[:, :, c1:c1 + moba_w] * scale
    w_mkv = w_in[:, :, c1 + moba_w:]
    w_in_b = jnp.concatenate([w_fq, w_fkv, w_mq, w_mkv, w_fg], axis=2).astype(BF16)

    w_out_b = w_mix_out.astype(BF16)
    w_cq_b = (w_cq * (D // N_CROSS_HEADS) ** -0.5).astype(BF16)
    w_ckv_b = jnp.concatenate([w_ck, w_cv], axis=2).astype(BF16)
    w_co_b = w_co.astype(BF16)
    w_r = jnp.pad(w_router, ((0, 0), (0, 0), (0, LANES - n_experts)))
    w_r_hi = w_r.astype(BF16)
    w_r_split = jnp.concatenate([w_r_hi, (w_r - w_r_hi.astype(F32)).astype(BF16)], axis=2)
    b_r = jnp.pad(b_router, ((0, 0), (0, LANES - n_experts))).reshape(depth, 1, LANES)

    bias_tiles = _moba_bias_tables(rel_bias, S)
    blk_avg = _block_average_matrix(S)
    mem2d = mem.reshape(B * n_mem, D)
    xt = x.reshape(T, D)

    for l in range(depth):
        qkv, f_logit = _in_proj(xt, w_in_b[l])
        c_aug = _fox_decay(f_logit, b_forget[l], S, n_fox)
        o_f = _fox_attention(qkv, c_aug, B, S, n_fox)
        o_m = _moba_attention(qkv, blk_avg, bias_tiles, B, S, n_fox, n_moba)
        xt = _mix_out(o_f, o_m, xt, w_out_b[l], ln1_g[l], ln1_b[l], alpha)

        kv = _mem_proj(mem2d, w_ckv_b[l])
        xt, xt_tiles, top_idx, gates, counts = _cross_and_route(
            xt, w_cq_b[l], kv, w_co_b[l], ln2_g[l], ln2_b[l], w_r_split[l], b_r[l],
            alpha, S, n_mem, n_experts)

        step_idx, block_expert, n_valid = _dispatch(
            top_idx[:, :TOP_K], counts[0, :n_experts].astype(jnp.int32), D // LANES)
        ys = _moe_experts(l, block_expert, n_valid, step_idx, xt_tiles, w_gate_up, b_gate_up,
                          w_down, b_down)
        xt = _combine_ln(xt, ys, gates, ln3_g[l], ln3_b[l], alpha)

    return xt.reshape(B, S, D)
```

```python
import functools
import math

import jax
import jax.numpy as jnp
import numpy as np
from jax import lax
from jax.experimental import pallas as pl
from jax.experimental.pallas import tpu as pltpu

F32 = jnp.float32
BF16 = jnp.bfloat16

HEAD_DIM = 64
N_BUCKETS = 32
MAX_DISTANCE = 128
MOBA_BLOCK = 256
MOBA_TOPK = 3
N_CROSS_HEADS = 4
TOP_K = 4
SWIGLU_LIMIT = 7.0
SWIGLU_ALPHA = 1.702
LN_EPS = 1e-5

LANES = 128
VMEM_LIMIT = 56 * 1024 * 1024

ROW_TILE = 512
ATT_TILE = MOBA_BLOCK
MOE_TILE = 512
MOE_FCHUNK = 512
MASKED = -1e30
LOG2E = math.log2(math.e)


def _params(*sem):
    return pltpu.CompilerParams(dimension_semantics=sem, vmem_limit_bytes=VMEM_LIMIT)


def _nt_dot(a, b):
    return lax.dot_general(a, b, (((1,), (1,)), ((), ())), preferred_element_type=F32)


def _layer_norm(y, g, b):
    mu = jnp.mean(y, axis=-1, keepdims=True)
    d = y - mu
    var = jnp.mean(d * d, axis=-1, keepdims=True)
    return d * lax.rsqrt(var + LN_EPS) * g + b


def _in_proj_kernel(x_ref, w_ref, qkv_ref, f_ref):
    xb = x_ref[...].astype(BF16)
    n_qkv = qkv_ref.shape[1]
    for n0 in range(0, n_qkv, 512):
        qkv_ref[:, n0:n0 + 512] = jnp.dot(
            xb, w_ref[:, n0:n0 + 512], preferred_element_type=F32).astype(BF16)
    f_ref[...] = jnp.dot(xb, w_ref[:, n_qkv:], preferred_element_type=F32)


def _in_proj(x, w):
    T, D = x.shape
    n_all = w.shape[1]
    n_qkv = n_all - LANES
    return pl.pallas_call(
        _in_proj_kernel,
        grid=(T // ROW_TILE,),
        in_specs=[pl.BlockSpec((ROW_TILE, D), lambda i: (i, 0)),
                  pl.BlockSpec((D, n_all), lambda i: (0, 0))],
        out_specs=[pl.BlockSpec((ROW_TILE, n_qkv), lambda i: (i, 0)),
                   pl.BlockSpec((ROW_TILE, LANES), lambda i: (i, 0))],
        out_shape=[jax.ShapeDtypeStruct((T, n_qkv), BF16),
                   jax.ShapeDtypeStruct((T, LANES), F32)],
        compiler_params=_params("parallel"),
        name="in_proj",
    )(x, w)


def _split3(x):
    p1 = x.astype(BF16)
    r1 = x - p1.astype(F32)
    p2 = r1.astype(BF16)
    p3 = (r1 - p2.astype(F32)).astype(BF16)
    return p1, p2, p3


def _decay_kernel(f_ref, b_ref, tri_ref, place_ref, c_ref):
    S = f_ref.shape[0]
    blk = tri_ref.shape[0]
    carry = jnp.zeros((1, LANES), F32)
    for j in range(S // blk):
        z = f_ref[j * blk:(j + 1) * blk, :] + b_ref[...]
        ls = jnp.minimum(z, 0.0) - jnp.log1p(jnp.exp(-jnp.abs(z)))
        c = carry
        for piece in _split3(ls):
            c = c + jnp.dot(tri_ref[...], piece, preferred_element_type=F32)
        carry = c[blk - 1:blk, :]
        aug = None
        for i, piece in enumerate(_split3(c * LOG2E)):
            t = jnp.dot(piece, place_ref[i], preferred_element_type=F32)
            aug = t if aug is None else aug + t
        c_ref[j * blk:(j + 1) * blk, :] = aug.astype(BF16)


def _fox_decay(f_logit, b_forget, S, n_fox):
    T = f_logit.shape[0]
    pairs = n_fox * HEAD_DIM // LANES
    blk = ATT_TILE
    tri = jnp.asarray(np.tril(np.ones((blk, blk), np.float32)), BF16)
    place = np.zeros((3, LANES, pairs * LANES), np.float32)
    for h in range(n_fox):
        for i in range(3):
            place[i, h, (h // 2) * LANES + 3 * (h % 2) + i] = -1.0
    b_pad = jnp.pad(b_forget, (0, LANES - n_fox)).reshape(1, LANES)
    return pl.pallas_call(
        _decay_kernel,
        grid=(T // S,),
        in_specs=[pl.BlockSpec((S, LANES), lambda b: (b, 0)),
                  pl.BlockSpec((1, LANES), lambda b: (0, 0)),
                  pl.BlockSpec((blk, blk), lambda b: (0, 0)),
                  pl.BlockSpec((3, LANES, pairs * LANES), lambda b: (0, 0, 0))],
        out_specs=pl.BlockSpec((S, pairs * LANES), lambda b: (b, 0)),
        out_shape=jax.ShapeDtypeStruct((T, pairs * LANES), BF16),
        compiler_params=_params("parallel"),
        name="fox_decay",
    )(f_logit, b_pad, tri, jnp.asarray(place, BF16))


VT_ROWS = HEAD_DIM + 16
SCORE_LOOKAHEAD = 2


def _online_update_t(s, vt, m, acc):
    m_new = jnp.maximum(m, jnp.max(s, axis=0, keepdims=True))
    p = jnp.exp2(s - m_new)
    acc_new = jnp.exp2(m - m_new) * acc + jnp.dot(vt, p.astype(BF16), preferred_element_type=F32)
    return m_new, acc_new


def _init_state(tq):
    return jnp.full((1, tq), MASKED, F32), jnp.zeros((VT_ROWS, tq), F32)


def _split_heads(q):
    lane = lax.broadcasted_iota(jnp.int32, q.shape, 1)
    zero = jnp.zeros((), q.dtype)
    return jnp.where(lane < HEAD_DIM, q, zero), jnp.where(lane < HEAD_DIM, zero, q)


def _causal_mask_t(tq):
    key = lax.broadcasted_iota(jnp.int32, (tq, tq), 0)
    qry = lax.broadcasted_iota(jnp.int32, (tq, tq), 1)
    return key <= qry


def _transpose_values(v_ref, vt_scr):
    tq = ATT_TILE
    S = v_ref.shape[0]
    for hh in range(2):
        vt_scr[hh * VT_ROWS + HEAD_DIM:(hh + 1) * VT_ROWS, :] = jnp.ones(
            (VT_ROWS - HEAD_DIM, S), BF16)
    for j in range(S // tq):
        vt = v_ref[j * tq:(j + 1) * tq, :].astype(F32).T.astype(BF16)
        for hh in range(2):
            vt_scr[hh * VT_ROWS:hh * VT_ROWS + HEAD_DIM, j * tq:(j + 1) * tq] = (
                vt[hh * HEAD_DIM:(hh + 1) * HEAD_DIM, :])


def _store_heads(o_ref, q0, state):
    tq = ATT_TILE
    o_t = jnp.concatenate(
        [acc[:HEAD_DIM] * (1.0 / acc[HEAD_DIM:HEAD_DIM + 1]) for _, acc in state], axis=0)
    o_ref[pl.ds(q0, tq), :] = o_t.T.astype(o_ref.dtype)


def _to_weights(x):
    return x.astype(F32).T.astype(BF16)


def _run_causal_tiles(n_blocks, score_tile, vt_scr, o_ref):
    tq = ATT_TILE
    tiles = [(qi, kb) for qi in range(n_blocks) for kb in [qi] + list(range(qi))]
    ahead = [score_tile(*t) for t in tiles[:SCORE_LOOKAHEAD]]
    state = None
    for i, (qi, kb) in enumerate(tiles):
        if i + SCORE_LOOKAHEAD < len(tiles):
            ahead.append(score_tile(*tiles[i + SCORE_LOOKAHEAD]))
        cur = ahead.pop(0)
        if kb == qi:
            state = (_init_state(tq), _init_state(tq))
        new_state = []
        for hh in range(2):
            vt = vt_scr[hh * VT_ROWS:(hh + 1) * VT_ROWS, kb * tq:(kb + 1) * tq]
            new_state.append(_online_update_t(cur[hh], vt, *state[hh]))
        state = tuple(new_state)
        if kb == max(qi - 1, 0):
            _store_heads(o_ref, qi * tq, state)


def _fox_kernel(q_ref, k_ref, v_ref, c_ref, o_ref, vt_scr):
    tq = ATT_TILE
    S = q_ref.shape[0]
    causal = _causal_mask_t(tq)
    lane = lax.broadcasted_iota(jnp.int32, (tq, LANES), 1)
    ones = [jnp.where((lane >= 3 * hh) & (lane < 3 * hh + 3), 1.0, 0.0).astype(BF16)
            for hh in range(2)]
    _transpose_values(v_ref, vt_scr)
    q_weights = {}

    def score_tile(qi, kb):
        if qi not in q_weights:
            qh = _split_heads(q_ref[qi * tq:(qi + 1) * tq, :])
            q_weights.clear()
            q_weights[qi] = [_to_weights(jnp.concatenate([qh[hh], ones[hh]], axis=1))
                             for hh in range(2)]
        rows = slice(kb * tq, (kb + 1) * tq)
        ka = jnp.concatenate([k_ref[rows, :], c_ref[rows, :]], axis=1)
        out = []
        for hh in range(2):
            s = jnp.dot(ka, q_weights[qi][hh], preferred_element_type=F32)
            out.append(jnp.where(causal, s, MASKED) if kb == qi else s)
        return tuple(out)

    _run_causal_tiles(S // tq, score_tile, vt_scr, o_ref)


def _fox_attention(qkv, c_aug, B, S, n_fox):
    T = B * S
    pairs = n_fox * HEAD_DIM // LANES
    kb = pairs
    vb = 2 * pairs
    return pl.pallas_call(
        _fox_kernel,
        grid=(B, pairs),
        in_specs=[pl.BlockSpec((S, LANES), lambda b, p: (b, p)),
                  pl.BlockSpec((S, LANES), lambda b, p: (b, kb + p)),
                  pl.BlockSpec((S, LANES), lambda b, p: (b, vb + p)),
                  pl.BlockSpec((S, LANES), lambda b, p: (b, p))],
        out_specs=pl.BlockSpec((S, LANES), lambda b, p: (b, p)),
        out_shape=jax.ShapeDtypeStruct((T, pairs * LANES), BF16),
        scratch_shapes=[pltpu.VMEM((2 * VT_ROWS, S), BF16)],
        compiler_params=_params("parallel", "parallel"),
        name="fox_attention",
    )(qkv, qkv, qkv, c_aug)


def _moba_kernel(q_ref, k_ref, v_ref, a_ref, bias_ref, o_ref, vt_scr):
    tq = ATT_TILE
    S = q_ref.shape[0]
    rows = -(-(S // tq) // 8) * 8
    causal = _causal_mask_t(tq)
    blk_row = lax.broadcasted_iota(jnp.int32, (rows, tq), 0)
    _transpose_values(v_ref, vt_scr)

    kmean = jnp.dot(a_ref[...], k_ref[...], preferred_element_type=F32)
    kmean_hi = kmean.astype(BF16)
    kmean_lo = (kmean - kmean_hi.astype(F32)).astype(BF16)
    per_block = {}

    def prepare(qi):
        qh = _split_heads(q_ref[qi * tq:(qi + 1) * tq, :])
        qw = [_to_weights(qh[hh]) for hh in range(2)]
        offsets = []
        for hh in range(2):
            gate = (jnp.dot(kmean_hi, qw[hh], preferred_element_type=F32)
                    + jnp.dot(kmean_lo, qw[hh], preferred_element_type=F32))[:rows, :]
            rank = jnp.zeros((rows, tq), jnp.int32)
            for mb in range(qi):
                g_m = gate[mb:mb + 1, :]
                tie = jnp.where(blk_row > mb, 1, 0)
                rank = rank + jnp.where(g_m > gate, 1, jnp.where(g_m == gate, tie, 0))
            offsets.append(jnp.where(rank < MOBA_TOPK, 0.0, MASKED))
        return qw, offsets

    def score_tile(qi, kb):
        if qi not in per_block:
            per_block.clear()
            per_block[qi] = prepare(qi)
        qw, offsets = per_block[qi]
        k = k_ref[kb * tq:(kb + 1) * tq, :]
        out = []
        for hh in range(2):
            s = jnp.dot(k, qw[hh], preferred_element_type=F32)
            if kb == qi:
                s = jnp.where(causal, s + bias_ref[hh, 0], MASKED)
            elif kb == qi - 1:
                s = s + bias_ref[hh, 1] + offsets[hh][kb:kb + 1, :]
            else:
                s = s + (bias_ref[hh, 2, 0:1, :] + offsets[hh][kb:kb + 1, :])
            out.append(s)
        return tuple(out)

    _run_causal_tiles(S // tq, score_tile, vt_scr, o_ref)


def _moba_attention(qkv, blk_avg, bias_tiles, B, S, n_fox, n_moba):
    T = B * S
    pairs = n_moba * HEAD_DIM // LANES
    qb = 3 * n_fox * HEAD_DIM // LANES
    kb = qb + pairs
    vb = qb + 2 * pairs
    tq = ATT_TILE
    return pl.pallas_call(
        _moba_kernel,
        grid=(B, pairs),
        in_specs=[pl.BlockSpec((S, LANES), lambda b, p: (b, qb + p)),
                  pl.BlockSpec((S, LANES), lambda b, p: (b, kb + p)),
                  pl.BlockSpec((S, LANES), lambda b, p: (b, vb + p)),
                  pl.BlockSpec((LANES, S), lambda b, p: (0, 0)),
                  pl.BlockSpec((2, 3, tq, tq), lambda b, p: (p, 0, 0, 0))],
        out_specs=pl.BlockSpec((S, LANES), lambda b, p: (b, p)),
        out_shape=jax.ShapeDtypeStruct((T, pairs * LANES), BF16),
        scratch_shapes=[pltpu.VMEM((2 * VT_ROWS, S), BF16)],
        compiler_params=_params("parallel", "parallel"),
        name="moba_attention",
    )(qkv, qkv, qkv, blk_avg, bias_tiles)


def _mix_out_kernel(of_ref, om_ref, x_ref, w_ref, g_ref, b_ref, y_ref, *, alpha):
    half = of_ref.shape[1]
    rows = x_ref.shape[0] // 2
    parts = [slice(0, rows), slice(rows, 2 * rows)]
    hs = [jnp.dot(of_ref[r, :], w_ref[:half, :], preferred_element_type=F32)
          + jnp.dot(om_ref[r, :], w_ref[half:, :], preferred_element_type=F32) for r in parts]
    for r, h in zip(parts, hs):
        y_ref[r, :] = _layer_norm(alpha * x_ref[r, :] + h, g_ref[...], b_ref[...])


def _mix_out(o_f, o_m, x, w, g, b, alpha):
    T, D = x.shape
    half = o_f.shape[1]
    row = lambda i: (i, 0)
    fixed = lambda i: (0, 0)
    return pl.pallas_call(
        functools.partial(_mix_out_kernel, alpha=alpha),
        grid=(T // ROW_TILE,),
        in_specs=[pl.BlockSpec((ROW_TILE, half), row),
                  pl.BlockSpec((ROW_TILE, half), row),
                  pl.BlockSpec((ROW_TILE, D), row),
                  pl.BlockSpec((D, D), fixed),
                  pl.BlockSpec((1, D), fixed),
                  pl.BlockSpec((1, D), fixed)],
        out_specs=pl.BlockSpec((ROW_TILE, D), row),
        out_shape=jax.ShapeDtypeStruct((T, D), F32),
        compiler_params=_params("parallel"),
        name="mix_out_ln",
    )(o_f, o_m, x, w, g.reshape(1, D), b.reshape(1, D))


def _mem_proj_kernel(m_ref, w_ref, kv_ref):
    mb = m_ref[...].astype(BF16)
    n = kv_ref.shape[1]
    for n0 in range(0, n, 512):
        kv_ref[:, n0:n0 + 512] = jnp.dot(
            mb, w_ref[:, n0:n0 + 512], preferred_element_type=F32).astype(BF16)


def _mem_proj(mem2d, w_kv):
    M, D = mem2d.shape
    n = w_kv.shape[1]
    tm = min(ROW_TILE, M)
    return pl.pallas_call(
        _mem_proj_kernel,
        grid=(M // tm,),
        in_specs=[pl.BlockSpec((tm, D), lambda i: (i, 0)),
                  pl.BlockSpec((D, n), lambda i: (0, 0))],
        out_specs=pl.BlockSpec((tm, n), lambda i: (i, 0)),
        out_shape=jax.ShapeDtypeStruct((M, n), BF16),
        compiler_params=_params("parallel"),
        name="mem_proj",
    )(mem2d, w_kv)


def _cross_kernel(x_ref, wq_ref, kv_ref, wo_ref, g_ref, b_ref, wr_ref, br_ref,
                  y_ref, yt_ref, idx_ref, gate_ref, count_ref, *, alpha, n_experts):
    D = x_ref.shape[1]

    @pl.when(pl.program_id(0) == 0)
    def _():
        count_ref[...] = jnp.zeros(count_ref.shape, F32)

    dh = D // N_CROSS_HEADS
    sub = D // LANES
    half = x_ref.shape[0] // 2
    rows = [slice(0, half), slice(half, 2 * half)]
    xs = [x_ref[r, :] for r in rows]
    qs = [jnp.dot(x.astype(BF16), wq_ref[...], preferred_element_type=F32).astype(BF16)
          for x in xs]
    heads = [[], []]
    for h in range(N_CROSS_HEADS):
        k_h = kv_ref[:, h * dh:(h + 1) * dh]
        v_h = kv_ref[:, D + h * dh:D + (h + 1) * dh]
        scores = [_nt_dot(q[:, h * dh:(h + 1) * dh], k_h) for q in qs]
        for a, s in enumerate(scores):
            p = jnp.exp(s - jnp.max(s, axis=1, keepdims=True))
            p = p * (1.0 / jnp.sum(p, axis=1, keepdims=True))
            heads[a].append(
                jnp.dot(p.astype(BF16), v_h, preferred_element_type=F32).astype(BF16))
    hproj = [jnp.dot(jnp.concatenate(hs, axis=1), wo_ref[...], preferred_element_type=F32)
             for hs in heads]

    lane = lax.broadcasted_iota(jnp.int32, (half, LANES), 1)
    lane_f = lane.astype(F32)
    chosen_total = jnp.zeros((1, LANES), F32)
    for a, r in enumerate(rows):
        y = _layer_norm(alpha * xs[a] + hproj[a], g_ref[...], b_ref[...])
        y_ref[r, :] = y
        for j in range(sub):
            yt_ref[pl.ds(a * half * sub + j, half, stride=sub), :] = y[:, j * LANES:(j + 1) * LANES]

        y_hi = y.astype(BF16)
        y_lo = (y - y_hi.astype(F32)).astype(BF16)
        both = jnp.dot(y_hi, wr_ref[...], preferred_element_type=F32)
        logits = (both[:, :LANES] + both[:, LANES:]
                  + jnp.dot(y_lo, wr_ref[:, :LANES], preferred_element_type=F32)
                  + br_ref[...])
        work = jnp.where(lane < n_experts, logits, -jnp.inf)
        idx_out = jnp.zeros(logits.shape, F32)
        val_out = jnp.zeros(logits.shape, F32)
        top = None
        for kk in range(TOP_K):
            best = jnp.max(work, axis=1, keepdims=True)
            arg = jnp.min(jnp.where(work == best, lane_f, float(LANES)), axis=1, keepdims=True)
            work = jnp.where(lane_f == arg, -jnp.inf, work)
            if top is None:
                top = best
            idx_out = jnp.where(lane == kk, arg, idx_out)
            val_out = jnp.where(lane == kk, jnp.exp(best - top), val_out)
        idx_ref[r, :] = idx_out.astype(jnp.int32)
        gate_ref[r, :] = val_out * (1.0 / jnp.sum(val_out, axis=1, keepdims=True))
        chosen = jnp.where(work == -jnp.inf, jnp.where(lane < n_experts, 1.0, 0.0), 0.0)
        chosen_total = chosen_total + jnp.sum(chosen, axis=0, keepdims=True)

    count_ref[...] = count_ref[...] + chosen_total


def _cross_and_route(x, wq, kv, wo, g, b, wr_split, br, alpha, S, n_mem, n_experts):
    T, D = x.shape
    tiles_per_batch = S // ROW_TILE
    row = lambda i: (i, 0)
    fixed = lambda i: (0, 0)
    return pl.pallas_call(
        functools.partial(_cross_kernel, alpha=alpha, n_experts=n_experts),
        grid=(T // ROW_TILE,),
        in_specs=[pl.BlockSpec((ROW_TILE, D), row),
                  pl.BlockSpec((D, D), fixed),
                  pl.BlockSpec((n_mem, 2 * D), lambda i: (i // tiles_per_batch, 0)),
                  pl.BlockSpec((D, D), fixed),
                  pl.BlockSpec((1, D), fixed),
                  pl.BlockSpec((1, D), fixed),
                  pl.BlockSpec((D, 2 * LANES), fixed),
                  pl.BlockSpec((1, LANES), fixed)],
        out_specs=[pl.BlockSpec((ROW_TILE, D), row),
                   pl.BlockSpec((ROW_TILE * D // LANES, LANES), row),
                   pl.BlockSpec((ROW_TILE, LANES), row),
                   pl.BlockSpec((ROW_TILE, LANES), row),
                   pl.BlockSpec((8, LANES), fixed)],
        out_shape=[jax.ShapeDtypeStruct((T, D), F32),
                   jax.ShapeDtypeStruct((T * D // LANES, LANES), F32),
                   jax.ShapeDtypeStruct((T, LANES), jnp.int32),
                   jax.ShapeDtypeStruct((T, LANES), F32),
                   jax.ShapeDtypeStruct((8, LANES), F32)],
        compiler_params=_params("arbitrary"),
        name="cross_attn_router",
    )(x, wq, kv, wo, g.reshape(1, D), b.reshape(1, D), wr_split, br)


def _moe_kernel(be_ref, nv_ref, idx_hbm, x_hbm, wgu_ref, bgu_ref, wd_ref, bd_ref, y_hbm,
                wgu_bf, wd_bf, xbuf, ybuf, idx_smem, isem, gsem, ssem, zsem, fence_sem):
    i = pl.program_id(0)
    nb = pl.num_programs(0)
    nv = nv_ref[0]
    D, F = wgu_ref.shape[0], wd_ref.shape[0]
    sub = D // LANES
    tile = xbuf.shape[1] // sub
    valid = i < nv
    cur = i & 1
    nxt = 1 - cur

    def idx_copy(step, s):
        return pltpu.make_async_copy(idx_hbm.at[step], idx_smem.at[s], isem.at[s])

    def start_gathers(par, segment, buf):
        for r in range(tile):
            src = pl.multiple_of(idx_smem[par, segment * tile + r], sub)
            pltpu.make_async_copy(x_hbm.at[pl.ds(src, sub), :],
                                  xbuf.at[buf, pl.ds(r * sub, sub), :],
                                  gsem.at[buf]).start(priority=0)

    def wait_gathers(buf):
        pltpu.make_async_copy(xbuf.at[buf], xbuf.at[buf], gsem.at[buf]).wait()

    def start_scatters(par, buf):
        for r in range(tile):
            dst = pl.multiple_of(idx_smem[par, tile + r], sub)
            pltpu.make_async_copy(ybuf.at[buf, pl.ds(r * sub, sub), :],
                                  y_hbm.at[pl.ds(dst, sub), :],
                                  ssem.at[buf]).start(priority=1)

    def wait_scatters(buf):
        pltpu.make_async_copy(ybuf.at[buf], ybuf.at[buf], ssem.at[buf]).wait()

    @pl.when(i == 0)
    def _():
        idx_copy(0, 0).start()
    idx_copy(i, cur).wait()

    @pl.when(i + 1 < nb)
    def _():
        idx_copy(i + 1, nxt).start()

    @pl.when(i == 0)
    def _():
        ybuf[1] = jnp.zeros(ybuf.shape[1:], ybuf.dtype)
        start_gathers(0, 2, 0)

    @pl.when(i <= nv)
    def _():
        wait_gathers(cur)

    @pl.when((i >= 1) & (i - 2 < nv))
    def _():
        wait_scatters(cur)

    @pl.when(valid & ((i == 0) | (be_ref[i] != be_ref[jnp.maximum(i - 1, 0)])))
    def _():
        def cast_rows(ref, out, r, carry):
            r0 = pl.multiple_of(r * LANES, LANES)
            out[pl.ds(r0, LANES), :] = ref[pl.ds(r0, LANES), :].astype(BF16)
            return carry
        lax.fori_loop(0, D // LANES, functools.partial(cast_rows, wgu_ref, wgu_bf), 0)
        lax.fori_loop(0, F // LANES, functools.partial(cast_rows, wd_ref, wd_bf), 0)

    def compute(par):
        x = jnp.concatenate([xbuf[par, pl.ds(j, tile, stride=sub), :].astype(BF16)
                             for j in range(sub)], axis=1)
        start_gathers(par, 0, 1 - par)
        acc = None
        for f0 in range(0, F, MOE_FCHUNK):
            f1 = f0 + MOE_FCHUNK
            if f0 == MOE_FCHUNK:
                pl.semaphore_signal(fence_sem, 1)
                pl.semaphore_wait(fence_sem, 1)
                start_scatters(par, 1 - par)
            g = jnp.dot(x, wgu_bf[:, f0:f1], preferred_element_type=F32) + bgu_ref[0, :, f0:f1]
            u = (jnp.dot(x, wgu_bf[:, F + f0:F + f1], preferred_element_type=F32)
                 + bgu_ref[0, :, F + f0:F + f1])
            g = jnp.minimum(g, SWIGLU_LIMIT)
            u = jnp.clip(u, -SWIGLU_LIMIT, SWIGLU_LIMIT)
            glu = g * jax.nn.sigmoid(g * SWIGLU_ALPHA)
            act = ((u + 1.0) * glu).astype(BF16)
            part = jnp.dot(act, wd_bf[f0:f1, :], preferred_element_type=F32)
            acc = part if acc is None else acc + part
        y = acc + bd_ref[0]
        for j in range(sub):
            ybuf[par, pl.ds(j, tile, stride=sub), :] = y[:, j * LANES:(j + 1) * LANES]

    for par in range(2):
        pl.when(valid & (cur == par))(functools.partial(compute, par))
        pl.when((i == nv) & (cur == par))(functools.partial(start_scatters, par, 1 - par))

    @pl.when(jnp.logical_not(valid))
    def _():
        ybuf[cur] = jnp.zeros(ybuf.shape[1:], ybuf.dtype)
        row0 = pl.multiple_of(i * tile * sub, tile * sub)
        zero_fill = pltpu.make_async_copy(ybuf.at[cur], y_hbm.at[pl.ds(row0, tile * sub), :],
                                          zsem)
        zero_fill.start()
        zero_fill.wait()

    @pl.when((i == nb - 1) & (i - 1 < nv))
    def _():
        wait_scatters(nxt)


def _moe_experts(layer, block_expert, n_valid, step_idx, x, wgu, bgu, wd, bd):
    L, E, D, F2 = wgu.shape
    n_blocks = step_idx.shape[0]
    F = F2 // 2
    sub = D // LANES
    assert n_blocks >= 3 and x.shape[1] == LANES and F >= 2 * MOE_FCHUNK

    def weight(i, be, nv):
        return (layer, be[i], 0, 0)

    def bias(i, be, nv):
        return (layer * E + be[i], 0, 0)

    grid_spec = pltpu.PrefetchScalarGridSpec(
        num_scalar_prefetch=2,
        grid=(n_blocks,),
        in_specs=[pl.BlockSpec(memory_space=pl.ANY),
                  pl.BlockSpec(memory_space=pl.ANY),
                  pl.BlockSpec((None, None, D, F2), weight),
                  pl.BlockSpec((1, 1, F2), bias),
                  pl.BlockSpec((None, None, F, D), weight),
                  pl.BlockSpec((1, 1, D), bias)],
        out_specs=pl.BlockSpec(memory_space=pl.ANY),
        scratch_shapes=[pltpu.VMEM((D, F2), BF16), pltpu.VMEM((F, D), BF16),
                        pltpu.VMEM((2, MOE_TILE * sub, LANES), F32),
                        pltpu.VMEM((2, MOE_TILE * sub, LANES), F32),
                        pltpu.SMEM((2, 3 * MOE_TILE), jnp.int32),
                        pltpu.SemaphoreType.DMA((2,)), pltpu.SemaphoreType.DMA((2,)),
                        pltpu.SemaphoreType.DMA((2,)), pltpu.SemaphoreType.DMA(()),
                        pltpu.SemaphoreType.REGULAR],
    )
    return pl.pallas_call(
        _moe_kernel,
        grid_spec=grid_spec,
        out_shape=jax.ShapeDtypeStruct(((n_blocks + 1) * MOE_TILE * sub, LANES), F32),
        compiler_params=_params("arbitrary"),
        name="moe_experts",
    )(block_expert, n_valid, step_idx, x, wgu, bgu.reshape(L * E, 1, F2), wd,
      bd.reshape(L * E, 1, D))


def _combine_kernel(x_ref, *refs, alpha):
    ys_refs, (gate_ref, g_ref, b_ref, y_ref) = refs[:TOP_K], refs[TOP_K:]
    tm, D = x_ref.shape
    sub = D // LANES
    gates = gate_ref[...]
    cols = []
    for j in range(sub):
        y = alpha * x_ref[:, j * LANES:(j + 1) * LANES]
        for kk in range(TOP_K):
            y = y + ys_refs[kk][pl.ds(j, tm, stride=sub), :] * gates[:, kk:kk + 1]
        cols.append(y)
    y_ref[...] = _layer_norm(jnp.concatenate(cols, axis=1), g_ref[...], b_ref[...])


def _combine_ln(x, ys, gates, g, b, alpha):
    T, D = x.shape
    tm = ROW_TILE
    sub = D // LANES
    row = lambda i: (i, 0)
    fixed = lambda i: (0, 0)
    ys_specs = [pl.BlockSpec((tm * sub, LANES),
                             functools.partial(lambda k, i: (k * (T // tm) + i, 0), k))
                for k in range(TOP_K)]
    return pl.pallas_call(
        functools.partial(_combine_kernel, alpha=alpha),
        grid=(T // tm,),
        in_specs=[pl.BlockSpec((tm, D), row)] + ys_specs + [
            pl.BlockSpec((tm, LANES), row),
            pl.BlockSpec((1, D), fixed),
            pl.BlockSpec((1, D), fixed)],
        out_specs=pl.BlockSpec((tm, D), row),
        out_shape=jax.ShapeDtypeStruct((T, D), F32),
        compiler_params=_params("parallel"),
        name="moe_combine_ln",
    )(x, *([ys] * TOP_K), gates, g.reshape(1, D), b.reshape(1, D))


def _t5_bucket(dist):
    max_exact = N_BUCKETS // 2
    n = jnp.maximum(dist, 0)
    nf = jnp.maximum(n, 1).astype(F32)
    large = max_exact + (jnp.log(nf / max_exact) / math.log(MAX_DISTANCE / max_exact)
                         * (N_BUCKETS - max_exact)).astype(jnp.int32)
    large = jnp.minimum(large, N_BUCKETS - 1)
    return jnp.where(n < max_exact, n, large)


def _moba_bias_tables(rel_bias, S):
    n = ATT_TILE
    n_heads = rel_bias.shape[1]
    by_dist = rel_bias.T.astype(F32)[:, _t5_bucket(jnp.arange(2 * n + 1))] * LOG2E

    def toeplitz(v):
        return jnp.tile(v, (1, n))[:, :n * (2 * n - 1)].reshape(n_heads, n, 2 * n - 1)[:, :, :n]

    own = toeplitz(by_dist[:, :2 * n])
    prev = toeplitz(jnp.roll(by_dist[:, :2 * n], -n, axis=1))
    far = jnp.broadcast_to(by_dist[:, 2 * n][:, None, None], (n_heads, n, n))
    return jnp.stack([own, prev, far], axis=1)


def _block_average_matrix(S):
    a = np.zeros((LANES, S), np.float32)
    for n in range(S // MOBA_BLOCK):
        a[n, n * MOBA_BLOCK:(n + 1) * MOBA_BLOCK] = 1.0 / MOBA_BLOCK
    return jnp.asarray(a, BF16)


def _dispatch(top_idx, counts, rows_per_token):
    T = top_idx.shape[0]
    TK = T * TOP_K
    n_experts = counts.shape[0]
    padded = ((counts + MOE_TILE - 1) // MOE_TILE) * MOE_TILE
    start = jnp.cumsum(counts) - counts
    pend = jnp.cumsum(padded)
    pstart = pend - padded
    n_blocks = TK // MOE_TILE + n_experts
    n_valid = (pend[-1] // MOE_TILE).astype(jnp.int32)
    blk = jnp.minimum(jnp.arange(n_blocks, dtype=jnp.int32), n_valid - 1)
    block_expert = jnp.minimum(
        jnp.sum((pend[None, :] <= (blk * MOE_TILE)[:, None]).astype(jnp.int32), axis=1),
        n_experts - 1)
    order = jnp.argsort(top_idx.reshape(-1)).astype(jnp.int32)
    slot = jnp.arange(n_blocks * MOE_TILE, dtype=jnp.int32).reshape(n_blocks, MOE_TILE)
    within = slot - pstart[block_expert][:, None]
    real = within < counts[block_expert][:, None]
    src = jnp.clip(within + start[block_expert][:, None], 0, TK - 1)
    flat = order[src.reshape(-1)].reshape(n_blocks, MOE_TILE)
    token = flat // TOP_K
    out_row = jnp.where(real, (flat % TOP_K) * T + token,
                        slot + (TK - (start + counts)[block_expert])[:, None])
    spare = n_blocks * MOE_TILE + jnp.arange(MOE_TILE, dtype=jnp.int32)
    step_idx = jnp.concatenate(
        [jnp.concatenate([token[1:], token[-1:]], axis=0),
         jnp.concatenate([spare[None, :], out_row[:-1]], axis=0),
         token], axis=1)
    return step_idx * rows_per_token, block_expert, n_valid.reshape(1)


def kernel(x, mem, w_in, b_forget, w_mix_out, rel_bias, ln1_g, ln1_b, w_cq, w_ck, w_cv, w_co,
           ln2_g, ln2_b, w_router, b_router, w_gate_up, b_gate_up, w_down, b_down, ln3_g, ln3_b):
    B, S, D = x.shape
    depth = w_in.shape[0]
    n_mem = mem.shape[1]
    n_experts = w_router.shape[2]
    n_heads = D // HEAD_DIM
    n_fox = n_heads // 2
    n_moba = n_heads - n_fox
    fox_w, moba_w = n_fox * HEAD_DIM, n_moba * HEAD_DIM
    T = B * S
    alpha = (2 * depth) ** 0.25
    assert S % ROW_TILE == 0 and S % ATT_TILE == 0 and D % LANES == 0
    assert ATT_TILE == MOBA_BLOCK and MOBA_BLOCK >= MAX_DISTANCE
    assert n_experts <= LANES and (T * TOP_K) % MOE_TILE == 0

    scale = HEAD_DIM ** -0.5 * LOG2E
    c0 = 3 * fox_w
    c1 = c0 + n_fox
    w_fq = w_in[:, :, :fox_w] * scale
    w_fkv = w_in[:, :, fox_w:c0]
    w_fg = jnp.pad(w_in[:, :, c0:c1], ((0, 0), (0, 0), (0, LANES - n_fox)))
    w_mq = w_in[:, :, c1:c1 + moba_w] * scale
    w_mkv = w_in[:, :, c1 + moba_w:]
    w_in_b = jnp.concatenate([w_fq, w_fkv, w_mq, w_mkv, w_fg], axis=2).astype(BF16)

    w_out_b = w_mix_out.astype(BF16)
    w_cq_b = (w_cq * (D // N_CROSS_HEADS) ** -0.5).astype(BF16)
    w_ckv_b = jnp.concatenate([w_ck, w_cv], axis=2).astype(BF16)
    w_co_b = w_co.astype(BF16)
    w_r = jnp.pad(w_router, ((0, 0), (0, 0), (0, LANES - n_experts)))
    w_r_hi = w_r.astype(BF16)
    w_r_split = jnp.concatenate([w_r_hi, (w_r - w_r_hi.astype(F32)).astype(BF16)], axis=2)
    b_r = jnp.pad(b_router, ((0, 0), (0, LANES - n_experts))).reshape(depth, 1, LANES)

    bias_tiles = _moba_bias_tables(rel_bias, S)
    blk_avg = _block_average_matrix(S)
    mem2d = mem.reshape(B * n_mem, D)
    xt = x.reshape(T, D)

    for l in range(depth):
        qkv, f_logit = _in_proj(xt, w_in_b[l])
        c_aug = _fox_decay(f_logit, b_forget[l], S, n_fox)
        o_f = _fox_attention(qkv, c_aug, B, S, n_fox)
        o_m = _moba_attention(qkv, blk_avg, bias_tiles, B, S, n_fox, n_moba)
        xt = _mix_out(o_f, o_m, xt, w_out_b[l], ln1_g[l], ln1_b[l], alpha)

        kv = _mem_proj(mem2d, w_ckv_b[l])
        xt, xt_tiles, top_idx, gates, counts = _cross_and_route(
            xt, w_cq_b[l], kv, w_co_b[l], ln2_g[l], ln2_b[l], w_r_split[l], b_r[l],
            alpha, S, n_mem, n_experts)

        step_idx, block_expert, n_valid = _dispatch(
            top_idx[:, :TOP_K], counts[0, :n_experts].astype(jnp.int32), D // LANES)
        ys = _moe_experts(l, block_expert, n_valid, step_idx, xt_tiles, w_gate_up, b_gate_up,
                          w_down, b_down)
        xt = _combine_ln(xt, ys, gates, ln3_g[l], ln3_b[l], alpha)

    return xt.reshape(B, S, D)
```

```python
import functools
import math

import jax
import jax.numpy as jnp
import numpy as np
from jax import lax
from jax.experimental import pallas as pl
from jax.experimental.pallas import tpu as pltpu

F32 = jnp.float32
BF16 = jnp.bfloat16

HEAD_DIM = 64
N_BUCKETS = 32
MAX_DISTANCE = 128
MOBA_BLOCK = 256
MOBA_TOPK = 3
N_CROSS_HEADS = 4
TOP_K = 4
SWIGLU_LIMIT = 7.0
SWIGLU_ALPHA = 1.702
LN_EPS = 1e-5

LANES = 128
VMEM_LIMIT = 56 * 1024 * 1024

ROW_TILE = 512
ATT_TILE = MOBA_BLOCK
MOE_TILE = 512
MOE_FCHUNK = 512
MASKED = -1e30
LOG2E = math.log2(math.e)


def _params(*sem):
    return pltpu.CompilerParams(dimension_semantics=sem, vmem_limit_bytes=VMEM_LIMIT)


def _nt_dot(a, b):
    return lax.dot_general(a, b, (((1,), (1,)), ((), ())), preferred_element_type=F32)


def _pack_pair(a, b):
    ua = lax.bitcast_convert_type(a.astype(BF16).astype(F32), jnp.uint32)
    ub = lax.bitcast_convert_type(b.astype(BF16).astype(F32), jnp.uint32)
    return ua | (ub >> 16)


def _unpack_pair(p):
    a = lax.bitcast_convert_type(p & jnp.uint32(0xFFFF0000), F32)
    b = lax.bitcast_convert_type(p << 16, F32)
    return a, b


def _pack_rows(y):
    half = y.shape[1] // 2
    return _pack_pair(y[:, :half], y[:, half:])


def _layer_norm(y, g, b):
    mu = jnp.mean(y, axis=-1, keepdims=True)
    d = y - mu
    var = jnp.mean(d * d, axis=-1, keepdims=True)
    return d * lax.rsqrt(var + LN_EPS) * g + b


def _in_proj_kernel(x_ref, w_ref, qkv_ref, f_ref):
    xb = x_ref[...].astype(BF16)
    n_qkv = qkv_ref.shape[1]
    for n0 in range(0, n_qkv, 512):
        qkv_ref[:, n0:n0 + 512] = jnp.dot(
            xb, w_ref[:, n0:n0 + 512], preferred_element_type=F32).astype(BF16)
    f_ref[...] = jnp.dot(xb, w_ref[:, n_qkv:], preferred_element_type=F32)


def _in_proj(x, w):
    T, D = x.shape
    n_all = w.shape[1]
    n_qkv = n_all - LANES
    return pl.pallas_call(
        _in_proj_kernel,
        grid=(T // ROW_TILE,),
        in_specs=[pl.BlockSpec((ROW_TILE, D), lambda i: (i, 0)),
                  pl.BlockSpec((D, n_all), lambda i: (0, 0))],
        out_specs=[pl.BlockSpec((ROW_TILE, n_qkv), lambda i: (i, 0)),
                   pl.BlockSpec((ROW_TILE, LANES), lambda i: (i, 0))],
        out_shape=[jax.ShapeDtypeStruct((T, n_qkv), BF16),
                   jax.ShapeDtypeStruct((T, LANES), F32)],
        compiler_params=_params("parallel"),
        name="in_proj",
    )(x, w)


def _split3(x):
    p1 = x.astype(BF16)
    r1 = x - p1.astype(F32)
    p2 = r1.astype(BF16)
    p3 = (r1 - p2.astype(F32)).astype(BF16)
    return p1, p2, p3


def _decay_kernel(f_ref, b_ref, tri_ref, place_ref, c_ref):
    S = f_ref.shape[0]
    blk = tri_ref.shape[0]
    carry = jnp.zeros((1, LANES), F32)
    for j in range(S // blk):
        z = f_ref[j * blk:(j + 1) * blk, :] + b_ref[...]
        ls = jnp.minimum(z, 0.0) - jnp.log1p(jnp.exp(-jnp.abs(z)))
        c = carry
        for piece in _split3(ls):
            c = c + jnp.dot(tri_ref[...], piece, preferred_element_type=F32)
        carry = c[blk - 1:blk, :]
        aug = None
        for i, piece in enumerate(_split3(c * LOG2E)):
            t = jnp.dot(piece, place_ref[i], preferred_element_type=F32)
            aug = t if aug is None else aug + t
        c_ref[j * blk:(j + 1) * blk, :] = aug.astype(BF16)


def _fox_decay(f_logit, b_forget, S, n_fox):
    T = f_logit.shape[0]
    pairs = n_fox * HEAD_DIM // LANES
    blk = ATT_TILE
    tri = jnp.asarray(np.tril(np.ones((blk, blk), np.float32)), BF16)
    place = np.zeros((3, LANES, pairs * LANES), np.float32)
    for h in range(n_fox):
        for i in range(3):
            place[i, h, (h // 2) * LANES + 3 * (h % 2) + i] = -1.0
    b_pad = jnp.pad(b_forget, (0, LANES - n_fox)).reshape(1, LANES)
    return pl.pallas_call(
        _decay_kernel,
        grid=(T // S,),
        in_specs=[pl.BlockSpec((S, LANES), lambda b: (b, 0)),
                  pl.BlockSpec((1, LANES), lambda b: (0, 0)),
                  pl.BlockSpec((blk, blk), lambda b: (0, 0)),
                  pl.BlockSpec((3, LANES, pairs * LANES), lambda b: (0, 0, 0))],
        out_specs=pl.BlockSpec((S, pairs * LANES), lambda b: (b, 0)),
        out_shape=jax.ShapeDtypeStruct((T, pairs * LANES), BF16),
        compiler_params=_params("parallel"),
        name="fox_decay",
    )(f_logit, b_pad, tri, jnp.asarray(place, BF16))


VT_ROWS = HEAD_DIM + 16
SCORE_LOOKAHEAD = 2


def _online_update_t(s, vt, m, acc):
    m_new = jnp.maximum(m, jnp.max(s, axis=0, keepdims=True))
    p = jnp.exp2(s - m_new)
    acc_new = jnp.exp2(m - m_new) * acc + jnp.dot(vt, p.astype(BF16), preferred_element_type=F32)
    return m_new, acc_new


def _init_state(tq):
    return jnp.full((1, tq), MASKED, F32), jnp.zeros((VT_ROWS, tq), F32)


def _split_heads(q):
    lane = lax.broadcasted_iota(jnp.int32, q.shape, 1)
    zero = jnp.zeros((), q.dtype)
    return jnp.where(lane < HEAD_DIM, q, zero), jnp.where(lane < HEAD_DIM, zero, q)


def _causal_mask_t(tq):
    key = lax.broadcasted_iota(jnp.int32, (tq, tq), 0)
    qry = lax.broadcasted_iota(jnp.int32, (tq, tq), 1)
    return key <= qry


def _transpose_values(v_ref, vt_scr):
    tq = ATT_TILE
    S = v_ref.shape[0]
    for hh in range(2):
        vt_scr[hh * VT_ROWS + HEAD_DIM:(hh + 1) * VT_ROWS, :] = jnp.ones(
            (VT_ROWS - HEAD_DIM, S), BF16)
    for j in range(S // tq):
        vt = v_ref[j * tq:(j + 1) * tq, :].astype(F32).T.astype(BF16)
        for hh in range(2):
            vt_scr[hh * VT_ROWS:hh * VT_ROWS + HEAD_DIM, j * tq:(j + 1) * tq] = (
                vt[hh * HEAD_DIM:(hh + 1) * HEAD_DIM, :])


def _store_heads(o_ref, q0, state):
    tq = ATT_TILE
    o_t = jnp.concatenate(
        [acc[:HEAD_DIM] * (1.0 / acc[HEAD_DIM:HEAD_DIM + 1]) for _, acc in state], axis=0)
    o_ref[pl.ds(q0, tq), :] = o_t.T.astype(o_ref.dtype)


def _to_weights(x):
    return x.astype(F32).T.astype(BF16)


def _run_causal_tiles(n_blocks, score_tile, vt_scr, o_ref):
    tq = ATT_TILE
    tiles = [(qi, kb) for qi in range(n_blocks) for kb in [qi] + list(range(qi))]
    ahead = [score_tile(*t) for t in tiles[:SCORE_LOOKAHEAD]]
    state = None
    for i, (qi, kb) in enumerate(tiles):
        if i + SCORE_LOOKAHEAD < len(tiles):
            ahead.append(score_tile(*tiles[i + SCORE_LOOKAHEAD]))
        cur = ahead.pop(0)
        if kb == qi:
            state = (_init_state(tq), _init_state(tq))
        new_state = []
        for hh in range(2):
            vt = vt_scr[hh * VT_ROWS:(hh + 1) * VT_ROWS, kb * tq:(kb + 1) * tq]
            new_state.append(_online_update_t(cur[hh], vt, *state[hh]))
        state = tuple(new_state)
        if kb == max(qi - 1, 0):
            _store_heads(o_ref, qi * tq, state)


def _fox_kernel(q_ref, k_ref, v_ref, c_ref, o_ref, vt_scr):
    tq = ATT_TILE
    S = q_ref.shape[0]
    causal = _causal_mask_t(tq)
    lane = lax.broadcasted_iota(jnp.int32, (tq, LANES), 1)
    ones = [jnp.where((lane >= 3 * hh) & (lane < 3 * hh + 3), 1.0, 0.0).astype(BF16)
            for hh in range(2)]
    _transpose_values(v_ref, vt_scr)
    q_weights = {}

    def score_tile(qi, kb):
        if qi not in q_weights:
            qh = _split_heads(q_ref[qi * tq:(qi + 1) * tq, :])
            q_weights.clear()
            q_weights[qi] = [_to_weights(jnp.concatenate([qh[hh], ones[hh]], axis=1))
                             for hh in range(2)]
        rows = slice(kb * tq, (kb + 1) * tq)
        ka = jnp.concatenate([k_ref[rows, :], c_ref[rows, :]], axis=1)
        out = []
        for hh in range(2):
            s = jnp.dot(ka, q_weights[qi][hh], preferred_element_type=F32)
            out.append(jnp.where(causal, s, MASKED) if kb == qi else s)
        return tuple(out)

    _run_causal_tiles(S // tq, score_tile, vt_scr, o_ref)


def _fox_attention(qkv, c_aug, B, S, n_fox):
    T = B * S
    pairs = n_fox * HEAD_DIM // LANES
    kb = pairs
    vb = 2 * pairs
    return pl.pallas_call(
        _fox_kernel,
        grid=(B, pairs),
        in_specs=[pl.BlockSpec((S, LANES), lambda b, p: (b, p)),
                  pl.BlockSpec((S, LANES), lambda b, p: (b, kb + p)),
                  pl.BlockSpec((S, LANES), lambda b, p: (b, vb + p)),
                  pl.BlockSpec((S, LANES), lambda b, p: (b, p))],
        out_specs=pl.BlockSpec((S, LANES), lambda b, p: (b, p)),
        out_shape=jax.ShapeDtypeStruct((T, pairs * LANES), BF16),
        scratch_shapes=[pltpu.VMEM((2 * VT_ROWS, S), BF16)],
        compiler_params=_params("parallel", "parallel"),
        name="fox_attention",
    )(qkv, qkv, qkv, c_aug)


def _moba_kernel(q_ref, k_ref, v_ref, a_ref, bias_ref, o_ref, vt_scr):
    tq = ATT_TILE
    S = q_ref.shape[0]
    rows = -(-(S // tq) // 8) * 8
    causal = _causal_mask_t(tq)
    blk_row = lax.broadcasted_iota(jnp.int32, (rows, tq), 0)
    _transpose_values(v_ref, vt_scr)

    kmean = jnp.dot(a_ref[...], k_ref[...], preferred_element_type=F32)
    kmean_hi = kmean.astype(BF16)
    kmean_lo = (kmean - kmean_hi.astype(F32)).astype(BF16)
    per_block = {}

    def prepare(qi):
        qh = _split_heads(q_ref[qi * tq:(qi + 1) * tq, :])
        qw = [_to_weights(qh[hh]) for hh in range(2)]
        offsets = []
        for hh in range(2):
            gate = (jnp.dot(kmean_hi, qw[hh], preferred_element_type=F32)
                    + jnp.dot(kmean_lo, qw[hh], preferred_element_type=F32))[:rows, :]
            rank = jnp.zeros((rows, tq), jnp.int32)
            for mb in range(qi):
                g_m = gate[mb:mb + 1, :]
                tie = jnp.where(blk_row > mb, 1, 0)
                rank = rank + jnp.where(g_m > gate, 1, jnp.where(g_m == gate, tie, 0))
            offsets.append(jnp.where(rank < MOBA_TOPK, 0.0, MASKED))
        return qw, offsets

    def score_tile(qi, kb):
        if qi not in per_block:
            per_block.clear()
            per_block[qi] = prepare(qi)
        qw, offsets = per_block[qi]
        k = k_ref[kb * tq:(kb + 1) * tq, :]
        out = []
        for hh in range(2):
            s = jnp.dot(k, qw[hh], preferred_element_type=F32)
            if kb == qi:
                s = jnp.where(causal, s + bias_ref[hh, 0], MASKED)
            elif kb == qi - 1:
                s = s + bias_ref[hh, 1] + offsets[hh][kb:kb + 1, :]
            else:
                s = s + (bias_ref[hh, 2, 0:1, :] + offsets[hh][kb:kb + 1, :])
            out.append(s)
        return tuple(out)

    _run_causal_tiles(S // tq, score_tile, vt_scr, o_ref)


def _moba_attention(qkv, blk_avg, bias_tiles, B, S, n_fox, n_moba):
    T = B * S
    pairs = n_moba * HEAD_DIM // LANES
    qb = 3 * n_fox * HEAD_DIM // LANES
    kb = qb + pairs
    vb = qb + 2 * pairs
    tq = ATT_TILE
    return pl.pallas_call(
        _moba_kernel,
        grid=(B, pairs),
        in_specs=[pl.BlockSpec((S, LANES), lambda b, p: (b, qb + p)),
                  pl.BlockSpec((S, LANES), lambda b, p: (b, kb + p)),
                  pl.BlockSpec((S, LANES), lambda b, p: (b, vb + p)),
                  pl.BlockSpec((LANES, S), lambda b, p: (0, 0)),
                  pl.BlockSpec((2, 3, tq, tq), lambda b, p: (p, 0, 0, 0))],
        out_specs=pl.BlockSpec((S, LANES), lambda b, p: (b, p)),
        out_shape=jax.ShapeDtypeStruct((T, pairs * LANES), BF16),
        scratch_shapes=[pltpu.VMEM((2 * VT_ROWS, S), BF16)],
        compiler_params=_params("parallel", "parallel"),
        name="moba_attention",
    )(qkv, qkv, qkv, blk_avg, bias_tiles)


def _mix_out_kernel(of_ref, om_ref, x_ref, w_ref, g_ref, b_ref, y_ref, *, alpha):
    half = of_ref.shape[1]
    rows = x_ref.shape[0] // 2
    parts = [slice(0, rows), slice(rows, 2 * rows)]
    hs = [jnp.dot(of_ref[r, :], w_ref[:half, :], preferred_element_type=F32)
          + jnp.dot(om_ref[r, :], w_ref[half:, :], preferred_element_type=F32) for r in parts]
    for r, h in zip(parts, hs):
        y_ref[r, :] = _layer_norm(alpha * x_ref[r, :] + h, g_ref[...], b_ref[...])


def _mix_out(o_f, o_m, x, w, g, b, alpha):
    T, D = x.shape
    half = o_f.shape[1]
    row = lambda i: (i, 0)
    fixed = lambda i: (0, 0)
    return pl.pallas_call(
        functools.partial(_mix_out_kernel, alpha=alpha),
        grid=(T // ROW_TILE,),
        in_specs=[pl.BlockSpec((ROW_TILE, half), row),
                  pl.BlockSpec((ROW_TILE, half), row),
                  pl.BlockSpec((ROW_TILE, D), row),
                  pl.BlockSpec((D, D), fixed),
                  pl.BlockSpec((1, D), fixed),
                  pl.BlockSpec((1, D), fixed)],
        out_specs=pl.BlockSpec((ROW_TILE, D), row),
        out_shape=jax.ShapeDtypeStruct((T, D), F32),
        compiler_params=_params("parallel"),
        name="mix_out_ln",
    )(o_f, o_m, x, w, g.reshape(1, D), b.reshape(1, D))


def _mem_proj_kernel(m_ref, w_ref, kv_ref):
    mb = m_ref[...].astype(BF16)
    n = kv_ref.shape[1]
    for n0 in range(0, n, 512):
        kv_ref[:, n0:n0 + 512] = jnp.dot(
            mb, w_ref[:, n0:n0 + 512], preferred_element_type=F32).astype(BF16)


def _mem_proj(mem2d, w_kv):
    M, D = mem2d.shape
    n = w_kv.shape[1]
    tm = min(ROW_TILE, M)
    return pl.pallas_call(
        _mem_proj_kernel,
        grid=(M // tm,),
        in_specs=[pl.BlockSpec((tm, D), lambda i: (i, 0)),
                  pl.BlockSpec((D, n), lambda i: (0, 0))],
        out_specs=pl.BlockSpec((tm, n), lambda i: (i, 0)),
        out_shape=jax.ShapeDtypeStruct((M, n), BF16),
        compiler_params=_params("parallel"),
        name="mem_proj",
    )(mem2d, w_kv)


def _cross_kernel(x_ref, wq_ref, kv_ref, wo_ref, g_ref, b_ref, wr_ref, br_ref,
                  y_ref, yt_ref, idx_ref, gate_ref, count_ref, *, alpha, n_experts):
    D = x_ref.shape[1]

    @pl.when(pl.program_id(0) == 0)
    def _():
        count_ref[...] = jnp.zeros(count_ref.shape, F32)

    dh = D // N_CROSS_HEADS
    sub = D // (2 * LANES)
    half = x_ref.shape[0] // 2
    rows = [slice(0, half), slice(half, 2 * half)]
    xs = [x_ref[r, :] for r in rows]
    qs = [jnp.dot(x.astype(BF16), wq_ref[...], preferred_element_type=F32).astype(BF16)
          for x in xs]
    heads = [[], []]
    for h in range(N_CROSS_HEADS):
        k_h = kv_ref[:, h * dh:(h + 1) * dh]
        v_h = kv_ref[:, D + h * dh:D + (h + 1) * dh]
        scores = [_nt_dot(q[:, h * dh:(h + 1) * dh], k_h) for q in qs]
        for a, s in enumerate(scores):
            p = jnp.exp(s - jnp.max(s, axis=1, keepdims=True))
            p = p * (1.0 / jnp.sum(p, axis=1, keepdims=True))
            heads[a].append(
                jnp.dot(p.astype(BF16), v_h, preferred_element_type=F32).astype(BF16))
    hproj = [jnp.dot(jnp.concatenate(hs, axis=1), wo_ref[...], preferred_element_type=F32)
             for hs in heads]

    lane = lax.broadcasted_iota(jnp.int32, (half, LANES), 1)
    lane_f = lane.astype(F32)
    chosen_total = jnp.zeros((1, LANES), F32)
    for a, r in enumerate(rows):
        y = _layer_norm(alpha * xs[a] + hproj[a], g_ref[...], b_ref[...])
        y_ref[r, :] = y
        packed = _pack_rows(y)
        for j in range(sub):
            yt_ref[pl.ds(a * half * sub + j, half, stride=sub), :] = (
                packed[:, j * LANES:(j + 1) * LANES])

        y_hi = y.astype(BF16)
        y_lo = (y - y_hi.astype(F32)).astype(BF16)
        both = jnp.dot(y_hi, wr_ref[...], preferred_element_type=F32)
        logits = (both[:, :LANES] + both[:, LANES:]
                  + jnp.dot(y_lo, wr_ref[:, :LANES], preferred_element_type=F32)
                  + br_ref[...])
        work = jnp.where(lane < n_experts, logits, -jnp.inf)
        idx_out = jnp.zeros(logits.shape, F32)
        val_out = jnp.zeros(logits.shape, F32)
        top = None
        for kk in range(TOP_K):
            best = jnp.max(work, axis=1, keepdims=True)
            arg = jnp.min(jnp.where(work == best, lane_f, float(LANES)), axis=1, keepdims=True)
            work = jnp.where(lane_f == arg, -jnp.inf, work)
            if top is None:
                top = best
            idx_out = jnp.where(lane == kk, arg, idx_out)
            val_out = jnp.where(lane == kk, jnp.exp(best - top), val_out)
        idx_ref[r, :] = idx_out.astype(jnp.int32)
        gate_ref[r, :] = val_out * (1.0 / jnp.sum(val_out, axis=1, keepdims=True))
        chosen = jnp.where(work == -jnp.inf, jnp.where(lane < n_experts, 1.0, 0.0), 0.0)
        chosen_total = chosen_total + jnp.sum(chosen, axis=0, keepdims=True)

    count_ref[...] = count_ref[...] + chosen_total


def _cross_and_route(x, wq, kv, wo, g, b, wr_split, br, alpha, S, n_mem, n_experts):
    T, D = x.shape
    tiles_per_batch = S // ROW_TILE
    row = lambda i: (i, 0)
    fixed = lambda i: (0, 0)
    return pl.pallas_call(
        functools.partial(_cross_kernel, alpha=alpha, n_experts=n_experts),
        grid=(T // ROW_TILE,),
        in_specs=[pl.BlockSpec((ROW_TILE, D), row),
                  pl.BlockSpec((D, D), fixed),
                  pl.BlockSpec((n_mem, 2 * D), lambda i: (i // tiles_per_batch, 0)),
                  pl.BlockSpec((D, D), fixed),
                  pl.BlockSpec((1, D), fixed),
                  pl.BlockSpec((1, D), fixed),
                  pl.BlockSpec((D, 2 * LANES), fixed),
                  pl.BlockSpec((1, LANES), fixed)],
        out_specs=[pl.BlockSpec((ROW_TILE, D), row),
                   pl.BlockSpec((ROW_TILE * D // (2 * LANES), LANES), row),
                   pl.BlockSpec((ROW_TILE, LANES), row),
                   pl.BlockSpec((ROW_TILE, LANES), row),
                   pl.BlockSpec((8, LANES), fixed)],
        out_shape=[jax.ShapeDtypeStruct((T, D), F32),
                   jax.ShapeDtypeStruct((T * D // (2 * LANES), LANES), jnp.uint32),
                   jax.ShapeDtypeStruct((T, LANES), jnp.int32),
                   jax.ShapeDtypeStruct((T, LANES), F32),
                   jax.ShapeDtypeStruct((8, LANES), F32)],
        compiler_params=_params("arbitrary"),
        name="cross_attn_router",
    )(x, wq, kv, wo, g.reshape(1, D), b.reshape(1, D), wr_split, br)


def _moe_kernel(be_ref, nv_ref, idx_hbm, x_hbm, wgu_ref, bgu_ref, wd_ref, bd_ref, y_hbm,
                wgu_bf, wd_bf, xbuf, ybuf, idx_smem, isem, gsem, ssem, zsem, fence_sem):
    i = pl.program_id(0)
    nb = pl.num_programs(0)
    nv = nv_ref[0]
    D, F = wgu_ref.shape[0], wd_ref.shape[0]
    sub = D // (2 * LANES)
    tile = xbuf.shape[1] // sub
    valid = i < nv
    cur = i & 1
    nxt = 1 - cur

    def idx_copy(step, s):
        return pltpu.make_async_copy(idx_hbm.at[step], idx_smem.at[s], isem.at[s])

    def start_gathers(par, segment, buf):
        for r in range(tile):
            src = pl.multiple_of(idx_smem[par, segment * tile + r], sub)
            pltpu.make_async_copy(x_hbm.at[pl.ds(src, sub), :],
                                  xbuf.at[buf, pl.ds(r * sub, sub), :],
                                  gsem.at[buf]).start(priority=r % 2)

    def wait_gathers(buf):
        pltpu.make_async_copy(xbuf.at[buf], xbuf.at[buf], gsem.at[buf]).wait()

    def start_scatters(par, buf):
        for r in range(tile):
            dst = pl.multiple_of(idx_smem[par, tile + r], sub)
            pltpu.make_async_copy(ybuf.at[buf, pl.ds(r * sub, sub), :],
                                  y_hbm.at[pl.ds(dst, sub), :],
                                  ssem.at[buf]).start(priority=r % 2)

    def wait_scatters(buf):
        pltpu.make_async_copy(ybuf.at[buf], ybuf.at[buf], ssem.at[buf]).wait()

    @pl.when(i == 0)
    def _():
        idx_copy(0, 0).start()
    idx_copy(i, cur).wait()

    @pl.when(i + 1 < nb)
    def _():
        idx_copy(i + 1, nxt).start()

    @pl.when(i == 0)
    def _():
        ybuf[1] = jnp.zeros(ybuf.shape[1:], ybuf.dtype)
        start_gathers(0, 2, 0)

    @pl.when(i <= nv)
    def _():
        wait_gathers(cur)

    @pl.when((i >= 1) & (i - 2 < nv))
    def _():
        wait_scatters(cur)

    @pl.when(valid & ((i == 0) | (be_ref[i] != be_ref[jnp.maximum(i - 1, 0)])))
    def _():
        def cast_rows(ref, out, r, carry):
            r0 = pl.multiple_of(r * LANES, LANES)
            out[pl.ds(r0, LANES), :] = ref[pl.ds(r0, LANES), :].astype(BF16)
            return carry
        lax.fori_loop(0, D // LANES, functools.partial(cast_rows, wgu_ref, wgu_bf), 0)
        lax.fori_loop(0, F // LANES, functools.partial(cast_rows, wd_ref, wd_bf), 0)

    def compute(par):
        pairs = [_unpack_pair(xbuf[par, pl.ds(j, tile, stride=sub), :]) for j in range(sub)]
        x = jnp.concatenate([a for a, _ in pairs] + [b for _, b in pairs],
                            axis=1).astype(BF16)
        start_gathers(par, 0, 1 - par)
        acc = None
        for f0 in range(0, F, MOE_FCHUNK):
            f1 = f0 + MOE_FCHUNK
            if f0 == MOE_FCHUNK:
                pl.semaphore_signal(fence_sem, 1)
                pl.semaphore_wait(fence_sem, 1)
                start_scatters(par, 1 - par)
            g = jnp.dot(x, wgu_bf[:, f0:f1], preferred_element_type=F32) + bgu_ref[0, :, f0:f1]
            u = (jnp.dot(x, wgu_bf[:, F + f0:F + f1], preferred_element_type=F32)
                 + bgu_ref[0, :, F + f0:F + f1])
            g = jnp.minimum(g, SWIGLU_LIMIT)
            u = jnp.clip(u, -SWIGLU_LIMIT, SWIGLU_LIMIT)
            glu = g * jax.nn.sigmoid(g * SWIGLU_ALPHA)
            act = ((u + 1.0) * glu).astype(BF16)
            part = jnp.dot(act, wd_bf[f0:f1, :], preferred_element_type=F32)
            acc = part if acc is None else acc + part
        y = _pack_rows(acc + bd_ref[0])
        for j in range(sub):
            ybuf[par, pl.ds(j, tile, stride=sub), :] = y[:, j * LANES:(j + 1) * LANES]

    for par in range(2):
        pl.when(valid & (cur == par))(functools.partial(compute, par))
        pl.when((i == nv) & (cur == par))(functools.partial(start_scatters, par, 1 - par))

    @pl.when(jnp.logical_not(valid))
    def _():
        ybuf[cur] = jnp.zeros(ybuf.shape[1:], ybuf.dtype)
        row0 = pl.multiple_of(i * tile * sub, tile * sub)
        zero_fill = pltpu.make_async_copy(ybuf.at[cur], y_hbm.at[pl.ds(row0, tile * sub), :],
                                          zsem)
        zero_fill.start()
        zero_fill.wait()

    @pl.when((i == nb - 1) & (i - 1 < nv))
    def _():
        wait_scatters(nxt)


def _moe_experts(layer, block_expert, n_valid, step_idx, x, wgu, bgu, wd, bd):
    L, E, D, F2 = wgu.shape
    n_blocks = step_idx.shape[0]
    F = F2 // 2
    sub = D // (2 * LANES)
    assert n_blocks >= 3 and x.shape[1] == LANES and F >= 2 * MOE_FCHUNK

    def weight(i, be, nv):
        return (layer, be[i], 0, 0)

    def bias(i, be, nv):
        return (layer * E + be[i], 0, 0)

    grid_spec = pltpu.PrefetchScalarGridSpec(
        num_scalar_prefetch=2,
        grid=(n_blocks,),
        in_specs=[pl.BlockSpec(memory_space=pl.ANY),
                  pl.BlockSpec(memory_space=pl.ANY),
                  pl.BlockSpec((None, None, D, F2), weight),
                  pl.BlockSpec((1, 1, F2), bias),
                  pl.BlockSpec((None, None, F, D), weight),
                  pl.BlockSpec((1, 1, D), bias)],
        out_specs=pl.BlockSpec(memory_space=pl.ANY),
        scratch_shapes=[pltpu.VMEM((D, F2), BF16), pltpu.VMEM((F, D), BF16),
                        pltpu.VMEM((2, MOE_TILE * sub, LANES), jnp.uint32),
                        pltpu.VMEM((2, MOE_TILE * sub, LANES), jnp.uint32),
                        pltpu.SMEM((2, 3 * MOE_TILE), jnp.int32),
                        pltpu.SemaphoreType.DMA((2,)), pltpu.SemaphoreType.DMA((2,)),
                        pltpu.SemaphoreType.DMA((2,)), pltpu.SemaphoreType.DMA(()),
                        pltpu.SemaphoreType.REGULAR],
    )
    return pl.pallas_call(
        _moe_kernel,
        grid_spec=grid_spec,
        out_shape=jax.ShapeDtypeStruct(((n_blocks + 1) * MOE_TILE * sub, LANES), jnp.uint32),
        compiler_params=_params("arbitrary"),
        name="moe_experts",
    )(block_expert, n_valid, step_idx, x, wgu, bgu.reshape(L * E, 1, F2), wd,
      bd.reshape(L * E, 1, D))


def _combine_kernel(x_ref, *refs, alpha):
    ys_refs, (gate_ref, g_ref, b_ref, y_ref) = refs[:TOP_K], refs[TOP_K:]
    tm, D = x_ref.shape
    sub = D // (2 * LANES)
    gates = gate_ref[...]
    lo_cols, hi_cols = [], []
    for j in range(sub):
        y_lo = alpha * x_ref[:, j * LANES:(j + 1) * LANES]
        y_hi = alpha * x_ref[:, D // 2 + j * LANES:D // 2 + (j + 1) * LANES]
        for kk in range(TOP_K):
            a, b = _unpack_pair(ys_refs[kk][pl.ds(j, tm, stride=sub), :])
            y_lo = y_lo + a * gates[:, kk:kk + 1]
            y_hi = y_hi + b * gates[:, kk:kk + 1]
        lo_cols.append(y_lo)
        hi_cols.append(y_hi)
    y_ref[...] = _layer_norm(jnp.concatenate(lo_cols + hi_cols, axis=1), g_ref[...], b_ref[...])


def _combine_ln(x, ys, gates, g, b, alpha):
    T, D = x.shape
    tm = ROW_TILE
    sub = D // (2 * LANES)
    row = lambda i: (i, 0)
    fixed = lambda i: (0, 0)
    ys_specs = [pl.BlockSpec((tm * sub, LANES),
                             functools.partial(lambda k, i: (k * (T // tm) + i, 0), k))
                for k in range(TOP_K)]
    return pl.pallas_call(
        functools.partial(_combine_kernel, alpha=alpha),
        grid=(T // tm,),
        in_specs=[pl.BlockSpec((tm, D), row)] + ys_specs + [
            pl.BlockSpec((tm, LANES), row),
            pl.BlockSpec((1, D), fixed),
            pl.BlockSpec((1, D), fixed)],
        out_specs=pl.BlockSpec((tm, D), row),
        out_shape=jax.ShapeDtypeStruct((T, D), F32),
        compiler_params=_params("parallel"),
        name="moe_combine_ln",
    )(x, *([ys] * TOP_K), gates, g.reshape(1, D), b.reshape(1, D))


def _t5_bucket(dist):
    max_exact = N_BUCKETS // 2
    n = jnp.maximum(dist, 0)
    nf = jnp.maximum(n, 1).astype(F32)
    large = max_exact + (jnp.log(nf / max_exact) / math.log(MAX_DISTANCE / max_exact)
                         * (N_BUCKETS - max_exact)).astype(jnp.int32)
    large = jnp.minimum(large, N_BUCKETS - 1)
    return jnp.where(n < max_exact, n, large)


def _moba_bias_tables(rel_bias, S):
    n = ATT_TILE
    n_heads = rel_bias.shape[1]
    by_dist = rel_bias.T.astype(F32)[:, _t5_bucket(jnp.arange(2 * n + 1))] * LOG2E

    def toeplitz(v):
        return jnp.tile(v, (1, n))[:, :n * (2 * n - 1)].reshape(n_heads, n, 2 * n - 1)[:, :, :n]

    own = toeplitz(by_dist[:, :2 * n])
    prev = toeplitz(jnp.roll(by_dist[:, :2 * n], -n, axis=1))
    far = jnp.broadcast_to(by_dist[:, 2 * n][:, None, None], (n_heads, n, n))
    return jnp.stack([own, prev, far], axis=1)


def _block_average_matrix(S):
    a = np.zeros((LANES, S), np.float32)
    for n in range(S // MOBA_BLOCK):
        a[n, n * MOBA_BLOCK:(n + 1) * MOBA_BLOCK] = 1.0 / MOBA_BLOCK
    return jnp.asarray(a, BF16)


def _dispatch(top_idx, counts, rows_per_token):
    T = top_idx.shape[0]
    TK = T * TOP_K
    n_experts = counts.shape[0]
    padded = ((counts + MOE_TILE - 1) // MOE_TILE) * MOE_TILE
    start = jnp.cumsum(counts) - counts
    pend = jnp.cumsum(padded)
    pstart = pend - padded
    n_blocks = TK // MOE_TILE + n_experts
    n_valid = (pend[-1] // MOE_TILE).astype(jnp.int32)
    blk = jnp.minimum(jnp.arange(n_blocks, dtype=jnp.int32), n_valid - 1)
    block_expert = jnp.minimum(
        jnp.sum((pend[None, :] <= (blk * MOE_TILE)[:, None]).astype(jnp.int32), axis=1),
        n_experts - 1)
    order = jnp.argsort(top_idx.reshape(-1)).astype(jnp.int32)
    slot = jnp.arange(n_blocks * MOE_TILE, dtype=jnp.int32).reshape(n_blocks, MOE_TILE)
    within = slot - pstart[block_expert][:, None]
    real = within < counts[block_expert][:, None]
    src = jnp.clip(within + start[block_expert][:, None], 0, TK - 1)
    flat = order[src.reshape(-1)].reshape(n_blocks, MOE_TILE)
    token = flat // TOP_K
    out_row = jnp.where(real, (flat % TOP_K) * T + token,
                        slot + (TK - (start + counts)[block_expert])[:, None])
    spare = n_blocks * MOE_TILE + jnp.arange(MOE_TILE, dtype=jnp.int32)
    step_idx = jnp.concatenate(
        [jnp.concatenate([token[1:], token[-1:]], axis=0),
         jnp.concatenate([spare[None, :], out_row[:-1]], axis=0),
         token], axis=1)
    return step_idx * rows_per_token, block_expert, n_valid.reshape(1)


def kernel(x, mem, w_in, b_forget, w_mix_out, rel_bias, ln1_g, ln1_b, w_cq, w_ck, w_cv, w_co,
           ln2_g, ln2_b, w_router, b_router, w_gate_up, b_gate_up, w_down, b_down, ln3_g, ln3_b):
    B, S, D = x.shape
    depth = w_in.shape[0]
    n_mem = mem.shape[1]
    n_experts = w_router.shape[2]
    n_heads = D // HEAD_DIM
    n_fox = n_heads // 2
    n_moba = n_heads - n_fox
    fox_w, moba_w = n_fox * HEAD_DIM, n_moba * HEAD_DIM
    T = B * S
    alpha = (2 * depth) ** 0.25
    assert S % ROW_TILE == 0 and S % ATT_TILE == 0 and D % LANES == 0
    assert ATT_TILE == MOBA_BLOCK and MOBA_BLOCK >= MAX_DISTANCE
    assert n_experts <= LANES and (T * TOP_K) % MOE_TILE == 0

    scale = HEAD_DIM ** -0.5 * LOG2E
    c0 = 3 * fox_w
    c1 = c0 + n_fox
    w_fq = w_in[:, :, :fox_w] * scale
    w_fkv = w_in[:, :, fox_w:c0]
    w_fg = jnp.pad(w_in[:, :, c0:c1], ((0, 0), (0, 0), (0, LANES - n_fox)))
    w_mq = w_in[:, :, c1:c1 + moba_w] * scale
    w_mkv = w_in[:, :, c1 + moba_w:]
    w_in_b = jnp.concatenate([w_fq, w_fkv, w_mq, w_mkv, w_fg], axis=2).astype(BF16)

    w_out_b = w_mix_out.astype(BF16)
    w_cq_b = (w_cq * (D // N_CROSS_HEADS) ** -0.5).astype(BF16)
    w_ckv_b = jnp.concatenate([w_ck, w_cv], axis=2).astype(BF16)
    w_co_b = w_co.astype(BF16)
    w_r = jnp.pad(w_router, ((0, 0), (0, 0), (0, LANES - n_experts)))
    w_r_hi = w_r.astype(BF16)
    w_r_split = jnp.concatenate([w_r_hi, (w_r - w_r_hi.astype(F32)).astype(BF16)], axis=2)
    b_r = jnp.pad(b_router, ((0, 0), (0, LANES - n_experts))).reshape(depth, 1, LANES)

    bias_tiles = _moba_bias_tables(rel_bias, S)
    blk_avg = _block_average_matrix(S)
    mem2d = mem.reshape(B * n_mem, D)
    xt = x.reshape(T, D)

    for l in range(depth):
        qkv, f_logit = _in_proj(xt, w_in_b[l])
        c_aug = _fox_decay(f_logit, b_forget[l], S, n_fox)
        o_f = _fox_attention(qkv, c_aug, B, S, n_fox)
        o_m = _moba_attention(qkv, blk_avg, bias_tiles, B, S, n_fox, n_moba)
        xt = _mix_out(o_f, o_m, xt, w_out_b[l], ln1_g[l], ln1_b[l], alpha)

        kv = _mem_proj(mem2d, w_ckv_b[l])
        xt, xt_tiles, top_idx, gates, counts = _cross_and_route(
            xt, w_cq_b[l], kv, w_co_b[l], ln2_g[l], ln2_b[l], w_r_split[l], b_r[l],
            alpha, S, n_mem, n_experts)

        step_idx, block_expert, n_valid = _dispatch(
            top_idx[:, :TOP_K], counts[0, :n_experts].astype(jnp.int32), D // (2 * LANES))
        ys = _moe_experts(l, block_expert, n_valid, step_idx, xt_tiles, w_gate_up, b_gate_up,
                          w_down, b_down)
        xt = _combine_ln(xt, ys, gates, ln3_g[l], ln3_b[l], alpha)

    return xt.reshape(B, S, D)
```
